```python
import jax
import jax.numpy as jnp
from jax import lax
import numpy as np


D_MODEL = 1024
BATCH = 4
SEQ = 8192
DEPTH = 2
DEC_BATCH = 1
DEC_SEQ = 16384
PAST_LEN = 128

PLE_DIM = 256
GRID_W = 64
N_MIXERS = 4
D_MIX = D_MODEL
W_GROUP = D_MIX // N_MIXERS
HEAD_DIM = 64
N_HEADS_GROUP = W_GROUP // HEAD_DIM
LRU_BLOCK = W_GROUP // N_HEADS_GROUP
LRU_C = 8.0
SHORT_CONV = 4
SHORT_CONV_LEFT = 2
CHUNK = 64
ROPE_BASE = 10000.0
NA_KH = 8
NA_KW = 16
D_FF = 2816
FFN_CONV = 3
EPS = 1e-6
SPLIT_SIZES = (W_GROUP, W_GROUP, 3 * W_GROUP, W_GROUP, 2 * N_HEADS_GROUP, 2 * N_HEADS_GROUP, 3 * W_GROUP, W_GROUP, 3 * W_GROUP)
IN_COLS = 13 * W_GROUP + 4 * N_HEADS_GROUP

kernel_name = 'hybrid_bidir_encoder_parallel_heads'


def rms_norm(x, gain):
    xf = x.astype(jnp.float32)
    y = xf * lax.rsqrt(jnp.mean(xf * xf, axis=-1, keepdims=True) + EPS)
    return (y * gain.astype(jnp.float32)).astype(x.dtype)


def head_layer_norm(t, gain):
    mu = jnp.mean(t, axis=-1, keepdims=True)
    tc = t - mu
    return tc * lax.rsqrt(jnp.mean(tc * tc, axis=-1, keepdims=True) + EPS) * gain


def l2_normalize(t):
    return t * lax.rsqrt(jnp.sum(t * t, axis=-1, keepdims=True) + EPS)


def depthwise_conv(x, w, left):
    k = w.shape[0]
    s = x.shape[1]
    xp = jnp.pad(x, ((0, 0), (left, k - 1 - left), (0, 0)))
    out = xp[:, 0:s] * w[0]
    for j in range(1, k):
        out = out + xp[:, j:j + s] * w[j]
    return out


def flip_seq(t):
    return jnp.flip(t, axis=1)


def to_chunks(t):
    b, s, h = t.shape[:3]
    return jnp.moveaxis(t.reshape(b, s // CHUNK, CHUNK, h, -1), 3, 1)


def from_chunks(t):
    b, h, n, c, d = t.shape
    return jnp.moveaxis(t, 1, 3).reshape(b, n * c, h, d)


def rotary(t):
    s, d = t.shape[1], t.shape[-1]
    half = d // 2
    inv_freq = ROPE_BASE ** (-jnp.arange(half, dtype=jnp.float32) / half)
    ang = jnp.arange(s, dtype=jnp.float32)[:, None] * inv_freq[None, :]
    cos = jnp.cos(ang)[None, :, None, :]
    sin = jnp.sin(ang)[None, :, None, :]
    t1, t2 = t[..., :half], t[..., half:]
    return jnp.concatenate([t1 * cos - t2 * sin, t1 * sin + t2 * cos], axis=-1)


def linear_combine(left, right):
    a_l, b_l = left
    a_r, b_r = right
    return a_l * a_r, a_r * b_l + b_r


def rglru_direction(x, w_r, b_r, w_i, b_i, lam, reverse):
    b, s, w = x.shape
    xb = x.reshape(b, s, N_HEADS_GROUP, LRU_BLOCK)
    r = jax.nn.sigmoid(jnp.einsum('bshi,hij->bshj', xb, w_r).reshape(b, s, w) + b_r)
    i = jax.nn.sigmoid(jnp.einsum('bshi,hij->bshj', xb, w_i).reshape(b, s, w) + b_i)
    log_a = -LRU_C * r * jax.nn.softplus(-lam)
    gated_x = jnp.sqrt(-jnp.expm1(2.0 * log_a)) * (i * x)
    _, h = lax.associative_scan(linear_combine, (jnp.exp(log_a), gated_x), reverse=reverse, axis=1)
    return h


def gated_delta_chunked(q, k, v, beta, g):
    q, k, v = to_chunks(q), to_chunks(k), to_chunks(v)
    beta = to_chunks(beta[..., None])[..., 0]
    gcum = jnp.cumsum(to_chunks(g[..., None])[..., 0], axis=-1)
    c = CHUNK
    lower = jnp.tril(jnp.ones((c, c), dtype=bool))
    strict = jnp.tril(jnp.ones((c, c), dtype=bool), -1)
    decay = jnp.exp(jnp.where(lower, gcum[..., :, None] - gcum[..., None, :], -jnp.inf))
    kk = jnp.einsum('bhnid,bhnjd->bhnij', k, k)
    l_mat = jnp.where(strict, beta[..., :, None] * kk * decay, 0.0)
    dv = v.shape[-1]
    rhs = jnp.concatenate([v * beta[..., None], k * (beta * jnp.exp(gcum))[..., None]], axis=-1)
    sol = lax.linalg.triangular_solve(l_mat, rhs, left_side=True, lower=True, unit_diagonal=True)
    u_val, w_key = sol[..., :dv], sol[..., dv:]
    attn = jnp.einsum('bhnid,bhnjd->bhnij', q, k) * decay
    q_g = q * jnp.exp(gcum)[..., None]
    g_last = gcum[..., -1]
    k_g = k * jnp.exp(g_last[..., None] - gcum)[..., None]

    def step(state, xs):
        u_n, w_n, q_n, a_n, k_n, gl_n = xs
        v_new = u_n - jnp.einsum('bhck,bhkv->bhcv', w_n, state)
        o_n = jnp.einsum('bhck,bhkv->bhcv', q_n, state) + jnp.einsum('bhij,bhjv->bhiv', a_n, v_new)
        state = state * jnp.exp(gl_n)[..., None, None] + jnp.einsum('bhck,bhcv->bhkv', k_n, v_new)
        return state, o_n

    xs = tuple(jnp.moveaxis(t, 2, 0) for t in (u_val, w_key, q_g, attn, k_g, g_last))
    b, h, _, _, dk = q.shape
    state0 = jnp.zeros((b, h, dk, dv), q.dtype)
    _, o = lax.scan(step, state0, xs)
    return from_chunks(jnp.moveaxis(o, 0, 2))


def retention_chunked(q, k, v, log_gamma):
    q, k, v = to_chunks(q), to_chunks(k), to_chunks(v)
    c = CHUNK
    pos = jnp.arange(c, dtype=jnp.float32)
    lower = jnp.tril(jnp.ones((c, c), dtype=bool))
    lg = log_gamma[:, None]
    dmask = jnp.exp(jnp.where(lower, (pos[:, None] - pos[None, :]) * log_gamma[:, None, None], -jnp.inf))
    scores = jnp.einsum('bhnid,bhnjd->bhnij', q, k) * dmask[None, :, None]
    inner = jnp.einsum('bhnij,bhnje->bhnie', scores, v)
    q_dec = q * jnp.exp((pos[None, :] + 1.0) * lg)[None, :, None, :, None]
    k_dec = k * jnp.exp((c - 1.0 - pos[None, :]) * lg)[None, :, None, :, None]
    kv = jnp.einsum('bhncd,bhnce->nbhde', k_dec, v)
    chunk_decay = jnp.exp(c * log_gamma)[None, :, None, None]

    def step(state, kv_n):
        return state * chunk_decay + kv_n, state

    _, prev = lax.scan(step, jnp.zeros_like(kv[0]), kv)
    cross = jnp.einsum('bhncd,nbhde->bhnce', q_dec, prev)
    return from_chunks(inner + cross)


def neighborhood_attention(q, k, v, rpb):
    b, s, h, d = q.shape
    rows = s // GRID_W
    kh = min(NA_KH, rows)
    qg = q.reshape(b, rows, GRID_W, h, d)
    kg = k.reshape(b, rows, GRID_W, h, d)
    vg = v.reshape(b, rows, GRID_W, h, d)
    cols = jnp.arange(GRID_W)
    c0 = jnp.clip(cols - NA_KW // 2, 0, GRID_W - NA_KW)
    col_idx = c0[:, None] + jnp.arange(NA_KW)[None, :]
    dc = col_idx - cols[:, None] + (NA_KW - 1)
    scale = d ** -0.5

    def row_block(r):
        r0 = jnp.clip(r - kh // 2, 0, rows - kh)
        k_win = lax.dynamic_slice_in_dim(kg, r0, kh, axis=1)[:, :, col_idx]
        v_win = lax.dynamic_slice_in_dim(vg, r0, kh, axis=1)[:, :, col_idx]
        q_r = lax.dynamic_index_in_dim(qg, r, axis=1, keepdims=False)
        dr = r0 + jnp.arange(kh) - r + (NA_KH - 1)
        bias = rpb[:, dr[:, None, None], dc[None, :, :]]
        scores = jnp.einsum('bchd,bkcwhd->bhckw', q_r, k_win) * scale + jnp.transpose(bias, (0, 2, 1, 3))[None]
        probs = jax.nn.softmax(scores.reshape(b, h, GRID_W, kh * NA_KW), axis=-1).reshape(scores.shape)
        return jnp.einsum('bhckw,bkcwhd->bchd', probs, v_win)

    out = lax.map(row_block, jnp.arange(rows))
    return jnp.moveaxis(out, 0, 1).reshape(b, s, h, d)


def encoder_layer(x, p_l, lw):
    b, s, _ = x.shape
    f32 = jnp.float32
    heads = lambda t: t.reshape(b, s, N_HEADS_GROUP, HEAD_DIM)
    hn = rms_norm(x, lw['norm1'])
    u = (hn @ lw['w_in']).astype(f32)
    xa, ya, qkv_b, z_b, beta_b, alpha_b, qkv_c, gate_c, qkv_d = jnp.split(u, np.cumsum(SPLIT_SIZES)[:-1].tolist(), axis=-1)

    xa = depthwise_conv(xa, lw['conv_a_w'].astype(f32), SHORT_CONV_LEFT) + lw['conv_a_b'].astype(f32)
    wr, br, wi, bi, lam = (lw[n].astype(f32) for n in ('lru_wr', 'lru_br', 'lru_wi', 'lru_bi', 'lru_lambda'))
    h_a = rglru_direction(xa, wr[0], br[0], wi[0], bi[0], lam[0], False) + rglru_direction(xa, wr[1], br[1], wi[1], bi[1], lam[1], True)
    out_a = h_a * jax.nn.gelu(ya)

    qkv_b = jax.nn.silu(depthwise_conv(qkv_b, lw['gdn_conv'].astype(f32), SHORT_CONV_LEFT))
    q_b, k_b, v_b = (heads(t) for t in jnp.split(qkv_b, 3, axis=-1))
    q_b = l2_normalize(q_b) * (HEAD_DIM ** -0.5)
    k_b = l2_normalize(k_b)
    beta = jax.nn.sigmoid(beta_b).reshape(b, s, 2, N_HEADS_GROUP)
    g = -jnp.exp(lw['gdn_a_log'].astype(f32)) * jax.nn.softplus(alpha_b.reshape(b, s, 2, N_HEADS_GROUP) + lw['gdn_dt_bias'].astype(f32))
    o_b = gated_delta_chunked(q_b, k_b, v_b, beta[:, :, 0], g[:, :, 0]) + flip_seq(gated_delta_chunked(flip_seq(q_b), flip_seq(k_b), flip_seq(v_b), flip_seq(beta[:, :, 1]), flip_seq(g[:, :, 1])))
    out_b = rms_norm(o_b, lw['gdn_norm']) * jax.nn.silu(heads(z_b))

    q_c, k_c, v_c = (heads(t) for t in jnp.split(qkv_c, 3, axis=-1))
    q_c = rotary(q_c)
    k_c = rotary(k_c) * (HEAD_DIM ** -0.5)
    log_gamma = jnp.log1p(-jnp.exp2(-lw['ret_decay'].astype(f32)))
    o_c = retention_chunked(q_c, k_c, v_c, log_gamma[0]) + flip_seq(retention_chunked(flip_seq(q_c), flip_seq(k_c), flip_seq(v_c), log_gamma[1]))
    out_c = head_layer_norm(o_c, lw['ret_norm'].astype(f32)) * jax.nn.silu(heads(gate_c))

    q_d, k_d, v_d = (heads(t) for t in jnp.split(qkv_d, 3, axis=-1))
    q_d = rms_norm(q_d, lw['na_qnorm'])
    k_d = rms_norm(k_d, lw['na_knorm'])
    out_d = neighborhood_attention(q_d, k_d, v_d, lw['na_rpb'].astype(f32))

    mix = jnp.concatenate([out_a, out_b.reshape(b, s, W_GROUP), out_c.reshape(b, s, W_GROUP), out_d.reshape(b, s, W_GROUP)], axis=-1)
    x = x + mix.astype(x.dtype) @ lw['w_out']

    h2 = rms_norm(x, lw['norm2'])
    gate = depthwise_conv(h2 @ lw['ffn_wg'], lw['ffn_conv_w'], FFN_CONV // 2) + lw['ffn_conv_b']
    x = x + (jax.nn.gelu(gate) * (h2 @ lw['ffn_wu'])) @ lw['ffn_wd']

    h3 = rms_norm(x, lw['norm3'])
    x = x + jax.nn.sigmoid(h3 @ lw['ple_gate']) * (p_l @ lw['ple_proj'])
    return x


def setup_inputs(seed: int = 0) -> dict:
    key = jax.random.key(seed)
    ks = iter(jax.random.split(key, 40))
    H = N_HEADS_GROUP

    def nrm(shape, scale):
        return jax.random.normal(next(ks), shape, jnp.float32) * scale

    def gain(shape):
        return 1.0 + nrm(shape, 0.02)

    x_prompt = nrm((BATCH, SEQ, D_MODEL), 1.0)
    x_sample = nrm((DEC_BATCH, DEC_SEQ, D_MODEL), 1.0)
    p_prompt = nrm((DEPTH, BATCH, SEQ, PLE_DIM), 1.0)
    p_sample = nrm((DEPTH, DEC_BATCH, DEC_SEQ, PLE_DIM), 1.0)
    norm1 = gain((DEPTH, D_MODEL))
    norm2 = gain((DEPTH, D_MODEL))
    norm3 = gain((DEPTH, D_MODEL))
    w_in = nrm((DEPTH, D_MODEL, IN_COLS), D_MODEL ** -0.5)
    w_out = nrm((DEPTH, D_MIX, D_MODEL), D_MIX ** -0.5)
    conv_a_w = nrm((DEPTH, SHORT_CONV, W_GROUP), SHORT_CONV ** -0.5)
    conv_a_b = nrm((DEPTH, W_GROUP), 0.02)
    lru_wr = nrm((DEPTH, 2, H, LRU_BLOCK, LRU_BLOCK), LRU_BLOCK ** -0.5)
    lru_br = nrm((DEPTH, 2, W_GROUP), 0.02)
    lru_wi = nrm((DEPTH, 2, H, LRU_BLOCK, LRU_BLOCK), LRU_BLOCK ** -0.5)
    lru_bi = nrm((DEPTH, 2, W_GROUP), 0.02)
    a_target = jax.random.uniform(next(ks), (DEPTH, 2, W_GROUP), jnp.float32, minval=0.9, maxval=0.999)
    sig = a_target ** (1.0 / LRU_C)
    lru_lambda = jnp.log(sig) - jnp.log1p(-sig)
    gdn_conv = nrm((DEPTH, SHORT_CONV, 3 * W_GROUP), SHORT_CONV ** -0.5)
    gdn_a_log = jnp.log(jax.random.uniform(next(ks), (DEPTH, 2, H), jnp.float32, minval=1.0, maxval=16.0))
    dt = jnp.exp(jax.random.uniform(next(ks), (DEPTH, 2, H), jnp.float32, minval=float(np.log(1e-3)), maxval=float(np.log(1e-1))))
    gdn_dt_bias = dt + jnp.log(-jnp.expm1(-dt))
    gdn_norm = gain((DEPTH, HEAD_DIM))
    ret_decay = 5.0 + jnp.arange(H, dtype=jnp.float32)[None, None, :] + nrm((DEPTH, 2, H), 0.1)
    ret_norm = gain((DEPTH, HEAD_DIM))
    na_qnorm = gain((DEPTH, HEAD_DIM))
    na_knorm = gain((DEPTH, HEAD_DIM))
    na_rpb = nrm((DEPTH, H, 2 * NA_KH - 1, 2 * NA_KW - 1), 0.02)
    ffn_wg = nrm((DEPTH, D_MODEL, D_FF), D_MODEL ** -0.5)
    ffn_wu = nrm((DEPTH, D_MODEL, D_FF), D_MODEL ** -0.5)
    ffn_conv_w = nrm((DEPTH, FFN_CONV, D_FF), FFN_CONV ** -0.5)
    ffn_conv_b = nrm((DEPTH, D_FF), 0.02)
    ffn_wd = nrm((DEPTH, D_FF, D_MODEL), D_FF ** -0.5)
    ple_proj = nrm((DEPTH, PLE_DIM, D_MODEL), PLE_DIM ** -0.5)
    ple_gate = nrm((DEPTH, D_MODEL, D_MODEL), D_MODEL ** -0.5)
    return {'x_prompt': x_prompt, 'x_sample': x_sample, 'p_prompt': p_prompt, 'p_sample': p_sample,
            'norm1': norm1, 'norm2': norm2, 'norm3': norm3, 'w_in': w_in, 'w_out': w_out,
            'conv_a_w': conv_a_w, 'conv_a_b': conv_a_b, 'lru_wr': lru_wr, 'lru_br': lru_br,
            'lru_wi': lru_wi, 'lru_bi': lru_bi, 'lru_lambda': lru_lambda,
            'gdn_conv': gdn_conv, 'gdn_a_log': gdn_a_log, 'gdn_dt_bias': gdn_dt_bias, 'gdn_norm': gdn_norm,
            'ret_decay': ret_decay, 'ret_norm': ret_norm,
            'na_qnorm': na_qnorm, 'na_knorm': na_knorm, 'na_rpb': na_rpb,
            'ffn_wg': ffn_wg, 'ffn_wu': ffn_wu, 'ffn_conv_w': ffn_conv_w, 'ffn_conv_b': ffn_conv_b, 'ffn_wd': ffn_wd,
            'ple_proj': ple_proj, 'ple_gate': ple_gate}


def reference(x_prompt, x_sample, p_prompt, p_sample, norm1, norm2, norm3, w_in, w_out,
              conv_a_w, conv_a_b, lru_wr, lru_br, lru_wi, lru_bi, lru_lambda,
              gdn_conv, gdn_a_log, gdn_dt_bias, gdn_norm, ret_decay, ret_norm,
              na_qnorm, na_knorm, na_rpb, ffn_wg, ffn_wu, ffn_conv_w, ffn_conv_b, ffn_wd,
              ple_proj, ple_gate):
    def run_trunk(x, p):
        for i in range(DEPTH):
            lw = {'norm1': norm1[i], 'norm2': norm2[i], 'norm3': norm3[i], 'w_in': w_in[i], 'w_out': w_out[i],
                  'conv_a_w': conv_a_w[i], 'conv_a_b': conv_a_b[i], 'lru_wr': lru_wr[i], 'lru_br': lru_br[i],
                  'lru_wi': lru_wi[i], 'lru_bi': lru_bi[i], 'lru_lambda': lru_lambda[i],
                  'gdn_conv': gdn_conv[i], 'gdn_a_log': gdn_a_log[i], 'gdn_dt_bias': gdn_dt_bias[i], 'gdn_norm': gdn_norm[i],
                  'ret_decay': ret_decay[i], 'ret_norm': ret_norm[i],
                  'na_qnorm': na_qnorm[i], 'na_knorm': na_knorm[i], 'na_rpb': na_rpb[i],
                  'ffn_wg': ffn_wg[i], 'ffn_wu': ffn_wu[i], 'ffn_conv_w': ffn_conv_w[i], 'ffn_conv_b': ffn_conv_b[i],
                  'ffn_wd': ffn_wd[i], 'ple_proj': ple_proj[i], 'ple_gate': ple_gate[i]}
            x = encoder_layer(x, p[i], lw)
        return x

    y_prompt = run_trunk(x_prompt, p_prompt)
    y_sample = run_trunk(x_sample, p_sample)
    return (y_prompt, y_sample)
```

```python
import functools

import jax
import jax.numpy as jnp
import numpy as np
from jax import lax
from jax.experimental import pallas as pl
from jax.experimental.pallas import tpu as pltpu

F32 = jnp.float32
BF16 = jnp.bfloat16

D_MODEL = 1024
DEPTH = 2
PLE_DIM = 256
GRID_W = 64
W_GROUP = 256
HEAD_DIM = 64
N_HEADS = 4
LRU_C = 8.0
CHUNK = 64
ROPE_BASE = 10000.0
NA_KH = 8
NA_KW = 16
D_FF = 2816
EPS = 1e-6
NEG_BIG = -1e30

SUBLANES = 8
LANES = 128
VMEM_LIMIT = 56 * 1024 * 1024

CB_XA, CB_YA, CB_QB, CB_KB, CB_VB, CB_ZB, CB_QC, CB_KC, CB_VC, CB_GC, CB_QD, CB_KD, CB_VD = range(13)
CB128_AB = 26
IN_COLS_PAD = 13 * W_GROUP + LANES
IN_COLS = 13 * W_GROUP + 4 * N_HEADS

HIGHEST = lax.Precision.HIGHEST


def _in_perm():
    rot = [h * 64 + half * 32 + d for half in range(2) for h in range(N_HEADS) for d in range(32)]
    qc0 = 1552
    idx = list(range(0, 512)) + list(range(512, 1280)) + list(range(1280, 1536))
    idx += [qc0 + r for r in rot] + [qc0 + 256 + r for r in rot]
    idx += list(range(qc0 + 512, qc0 + 768)) + list(range(2320, 2576)) + list(range(2576, 3344))
    idx += list(range(1536, 1552)) + [IN_COLS] * (LANES - 4 * N_HEADS)
    assert len(idx) == IN_COLS_PAD
    return np.asarray(idx, np.int32)


def _sigmoid(x):
    return 1.0 / (1.0 + jnp.exp(-x))


def _silu(x):
    return x * _sigmoid(x)


def _softplus(x):
    return jnp.maximum(x, 0.0) + jnp.log1p(jnp.exp(-jnp.abs(x)))


def _gelu_tanh(x):
    return x * (0.5 * (1.0 + jnp.tanh(0.7978845608028654 * (x + 0.044715 * (x * x * x)))))


def _rms(x, gain):
    ms = jnp.mean(x * x, axis=-1, keepdims=True)
    return x * lax.rsqrt(ms + EPS) * gain


def _dot(a, b):
    return jnp.dot(a, b, preferred_element_type=F32)


def _dot_f32(a, b):
    return jnp.dot(a, b, preferred_element_type=F32, precision=HIGHEST)


def _dot_nt(a, b):
    return lax.dot_general(a, b, (((1,), (1,)), ((), ())), preferred_element_type=F32)


def _dot_tn(a, b):
    return lax.dot_general(a, b, (((0,), (0,)), ((), ())), preferred_element_type=F32)


def _iota(shape, dim):
    return lax.broadcasted_iota(jnp.int32, shape, dim)


def _head_ones():
    return (_iota((W_GROUP, W_GROUP), 0) // HEAD_DIM == _iota((W_GROUP, W_GROUP), 1) // HEAD_DIM).astype(F32)


def _stack4(x, mask):
    return jnp.where(mask, jnp.concatenate([x] * N_HEADS, axis=0), jnp.zeros((), x.dtype))


def _conv4(xm, xp, xn, w_ref, first, last):
    ts = xm.shape[0]
    xp = jnp.where(first, 0.0, xp)
    xn = jnp.where(last, 0.0, xn)
    xe = jnp.concatenate([xp, xm, xn], axis=0)
    out = w_ref[0:1, :] * xe[6:6 + ts]
    out = out + w_ref[1:2, :] * xe[7:7 + ts]
    out = out + w_ref[2:3, :] * xe[8:8 + ts]
    out = out + w_ref[3:4, :] * xe[9:9 + ts]
    return out


def _time_block(reverse):
    i = pl.program_id(1)
    n = pl.num_programs(1)
    tb = (n - 1 - i) if reverse else i
    return tb, tb == 0, tb == n - 1


def _row_specs(ts, ns, total_rows, col_block, reverse, width=W_GROUP):
    r8 = ts // SUBLANES
    last8 = total_rows // SUBLANES - 1

    def tb(i):
        return (ns - 1 - i) if reverse else i

    main = pl.BlockSpec((ts, width), lambda b, i: (b * ns + tb(i), col_block))
    prev = pl.BlockSpec((SUBLANES, width), lambda b, i: (jnp.maximum((b * ns + tb(i)) * r8 - 1, 0), col_block))
    nxt = pl.BlockSpec((SUBLANES, width), lambda b, i: (jnp.minimum((b * ns + tb(i) + 1) * r8, last8), col_block))
    return main, prev, nxt


def _const_spec(shape):
    nd = len(shape)
    return pl.BlockSpec(shape, lambda *_: (0,) * nd)


def _params(sem):
    return pltpu.CompilerParams(dimension_semantics=sem, vmem_limit_bytes=VMEM_LIMIT)


def _inproj_kernel(x_ref, g_ref, w_ref, o_ref):
    h = _rms(x_ref[...], g_ref[...]).astype(BF16)
    o_ref[...] = _dot(h, w_ref[...])


def _inproj(x, gain, w):
    t = x.shape[0]
    tm = 256
    return pl.pallas_call(
        _inproj_kernel,
        grid=(t // tm,),
        in_specs=[pl.BlockSpec((tm, D_MODEL), lambda i: (i, 0)),
                  _const_spec((1, D_MODEL)),
                  _const_spec((D_MODEL, IN_COLS_PAD))],
        out_specs=pl.BlockSpec((tm, IN_COLS_PAD), lambda i: (i, 0)),
        out_shape=jax.ShapeDtypeStruct((t, IN_COLS_PAD), F32),
        compiler_params=_params(("parallel",)),
        name="inproj",
    )(x, gain, w)


def _rglru_kernel(reverse, *refs):
    if reverse:
        (xm_ref, xp_ref, xn_ref, cw_ref, cb_ref, wr_ref, br_ref, wi_ref, bi_ref, lam_ref,
         o_ref, a_s, b_s, p_s, ac_s, c_s, carry, h_s) = refs
    else:
        (xm_ref, xp_ref, xn_ref, cw_ref, cb_ref, wr_ref, br_ref, wi_ref, bi_ref, lam_ref, ya_ref, hb_ref,
         o_ref, a_s, b_s, p_s, ac_s, c_s, carry, h_s) = refs
    ts = xm_ref.shape[0]
    ng = ts // SUBLANES
    tb, first, last = _time_block(reverse)

    @pl.when(pl.program_id(1) == 0)
    def _():
        carry[...] = jnp.zeros_like(carry)

    x = _conv4(xm_ref[...], xp_ref[...], xn_ref[...], cw_ref, first, last) + cb_ref[...]
    xb = x.astype(BF16)
    r = _sigmoid(_dot(xb, wr_ref[...]) + br_ref[...])
    ig = _sigmoid(_dot(xb, wi_ref[...]) + bi_ref[...])
    log_a = (-LRU_C) * r * _softplus(-lam_ref[...])
    a = jnp.exp(log_a)
    th = jnp.tanh(-log_a)
    bx = jnp.sqrt(2.0 * th / (1.0 + th)) * (ig * x)
    nl = W_GROUP // LANES
    for hf in range(nl):
        a_s[hf] = a[:, hf * LANES:(hf + 1) * LANES]
        b_s[hf] = bx[:, hf * LANES:(hf + 1) * LANES]

    def slab(ref, k):
        return jnp.concatenate([ref[hf, pl.ds(k, ng, stride=SUBLANES), :] for hf in range(nl)], axis=-1)

    order = list(range(SUBLANES))[::-1] if reverse else list(range(SUBLANES))
    p = None
    ac = None
    for k in order:
        a_k = slab(a_s, k)
        b_k = slab(b_s, k)
        if p is None:
            p, ac = b_k, a_k
        else:
            p = a_k * p + b_k
            ac = a_k * ac
        p_s[k] = p
        ac_s[k] = ac

    k_last = order[-1]

    def body(g, c):
        gi = (ng - 1 - g) if reverse else g
        c_s[pl.ds(gi, 1), :] = c
        return ac_s[k_last, pl.ds(gi, 1), :] * c + p_s[k_last, pl.ds(gi, 1), :]

    carry[...] = lax.fori_loop(0, ng, body, carry[...])
    c_in = c_s[...]
    for k in order:
        h_k = p_s[k] + ac_s[k] * c_in
        for hf in range(nl):
            h_s[hf, pl.ds(k, ng, stride=SUBLANES), :] = h_k[:, hf * LANES:(hf + 1) * LANES]
    h = jnp.concatenate([h_s[hf] for hf in range(nl)], axis=-1)
    if reverse:
        o_ref[...] = h
    else:
        o_ref[...] = ((h + hb_ref[...]) * _gelu_tanh(ya_ref[...])).astype(o_ref.dtype)


def _rglru(u, bsz, seq, reverse, cw, cb, wr, br, wi, bi, lam, hb=None):
    ts = 512
    ns = seq // ts
    t = bsz * seq
    ng = ts // SUBLANES
    xm, xp, xn = _row_specs(ts, ns, t, CB_XA, reverse)
    in_specs = [xm, xp, xn, _const_spec((4, W_GROUP)), _const_spec((1, W_GROUP)),
                _const_spec((W_GROUP, W_GROUP)), _const_spec((1, W_GROUP)),
                _const_spec((W_GROUP, W_GROUP)), _const_spec((1, W_GROUP)), _const_spec((1, W_GROUP))]
    args = [u, u, u, cw, cb, wr, br, wi, bi, lam]
    halves = (W_GROUP // LANES, ts, LANES)
    scratch = [pltpu.VMEM(halves, F32), pltpu.VMEM(halves, F32),
               pltpu.VMEM((SUBLANES, ng, W_GROUP), F32), pltpu.VMEM((SUBLANES, ng, W_GROUP), F32),
               pltpu.VMEM((ng, W_GROUP), F32), pltpu.VMEM((1, W_GROUP), F32), pltpu.VMEM(halves, F32)]
    if reverse:
        out_dtype = F32
        out_spec = pl.BlockSpec((ts, W_GROUP), lambda b, i: (b * ns + ns - 1 - i, 0))
    else:
        ya = _row_specs(ts, ns, t, CB_YA, False)[0]
        in_specs += [ya, pl.BlockSpec((ts, W_GROUP), lambda b, i: (b * ns + i, 0))]
        args += [u, hb]
        out_dtype = BF16
        out_spec = pl.BlockSpec((ts, W_GROUP), lambda b, i: (b * ns + i, 0))
    return pl.pallas_call(
        functools.partial(_rglru_kernel, reverse),
        grid=(bsz, ns),
        in_specs=in_specs,
        out_specs=out_spec,
        out_shape=jax.ShapeDtypeStruct((t, W_GROUP), out_dtype),
        scratch_shapes=scratch,
        compiler_params=_params(("parallel", "arbitrary")),
        name="rglru_bwd" if reverse else "rglru_fwd",
    )(*args)


def _gdn_kernel(reverse, *refs):
    if reverse:
        (qm, qp, qn, km, kp, kn, vm, vp, vn, ab_ref, cwq, cwk, cwv, alog_ref, dtb_ref, eb_ref, eg_ref,
         o_ref, s_ref) = refs
    else:
        (qm, qp, qn, km, kp, kn, vm, vp, vn, ab_ref, cwq, cwk, cwv, alog_ref, dtb_ref, eb_ref, eg_ref,
         z_ref, ob_ref, gn_ref, o_ref, s_ref) = refs
    ts = qm.shape[0]
    nc = ts // CHUNK
    c = CHUNK
    tb, first, last = _time_block(reverse)

    @pl.when(pl.program_id(1) == 0)
    def _():
        s_ref[...] = jnp.zeros_like(s_ref)

    ones_h = _head_ones()
    q = _silu(_conv4(qm[...], qp[...], qn[...], cwq, first, last))
    k = _silu(_conv4(km[...], kp[...], kn[...], cwk, first, last))
    v = _silu(_conv4(vm[...], vp[...], vn[...], cwv, first, last))
    q = q * lax.rsqrt(_dot_f32(q * q, ones_h) + EPS) * (HEAD_DIM ** -0.5)
    k = k * lax.rsqrt(_dot_f32(k * k, ones_h) + EPS)
    ab = ab_ref[...]
    beta_b = _dot_f32(_sigmoid(ab), eb_ref[...])
    g_b = _dot_f32(-jnp.exp(alog_ref[...]) * _softplus(ab + dtb_ref[...]), eg_ref[...])

    ii = _iota((c, W_GROUP), 0)
    jj = _iota((c, W_GROUP), 1) % c
    if reverse:
        incl, strict = jj >= ii, jj > ii
        tri = (_iota((c, c), 1) >= _iota((c, c), 0)).astype(F32)
        ucat = (ii < jj).astype(F32)
    else:
        incl, strict = jj <= ii, jj < ii
        tri = (_iota((c, c), 1) <= _iota((c, c), 0)).astype(F32)
        ucat = (ii > jj).astype(F32)
    eye_cat = (ii == jj).astype(F32)
    blk = _iota((W_GROUP, W_GROUP), 0) // HEAD_DIM == _iota((W_GROUP, W_GROUP), 1) // HEAD_DIM

    s = s_ref[...]
    chunks = list(range(nc))[::-1] if reverse else list(range(nc))
    outs = [None] * nc
    for ci in chunks:
        sl = slice(ci * c, (ci + 1) * c)
        qc, kc, vc, bc, gc_ = q[sl], k[sl], v[sl], beta_b[sl], g_b[sl]
        gcum = _dot_f32(tri, gc_)
        dmat = _dot_f32(tri, gc_ * ucat)
        decay = jnp.where(incl, jnp.exp(jnp.where(incl, dmat, 0.0)), 0.0)
        kst = _stack4(kc.astype(BF16), blk)
        kq = _dot_nt(jnp.concatenate([kc, qc], axis=0).astype(BF16), kst)
        lmat = jnp.where(strict, bc * kq[:c] * decay, 0.0)
        attn = kq[c:] * decay
        pinv = eye_cat - lmat
        lpow = _dot(lmat.astype(BF16), _stack4(lmat.astype(BF16), blk))
        for lvl in range(5):
            z = _stack4(lpow.astype(BF16), blk)
            if lvl < 4:
                r2 = _dot(jnp.concatenate([pinv, lpow], axis=0).astype(BF16), z)
                pinv = pinv + r2[:c]
                lpow = r2[c:]
            else:
                pinv = pinv + _dot(pinv.astype(BF16), z)
        egc = jnp.exp(gcum)
        vb = (vc * bc).astype(BF16)
        kb = (kc * bc * egc).astype(BF16)
        pb = pinv.astype(BF16)
        u_val = _dot(pb, _stack4(vb, blk))
        w_key = _dot(pb, _stack4(kb, blk))
        q_g = qc * egc
        g_last = gcum[0:1, :] if reverse else gcum[c - 1:c, :]
        k_g = kc * jnp.exp(g_last - gcum)
        ws = _dot(jnp.concatenate([w_key, q_g], axis=0).astype(BF16), s.astype(BF16))
        v_new = u_val - ws[:c]
        vnb = v_new.astype(BF16)
        outs[ci] = ws[c:] + _dot(attn.astype(BF16), _stack4(vnb, blk))
        s = s * jnp.exp(g_last) + jnp.where(blk, _dot_tn(k_g.astype(BF16), vnb), 0.0)
    s_ref[...] = s
    o = jnp.concatenate(outs, axis=0)
    if reverse:
        o_ref[...] = o
    else:
        o = o + ob_ref[...]
        ms = _dot_f32(o * o, ones_h) * (1.0 / HEAD_DIM)
        o_ref[...] = (o * lax.rsqrt(ms + EPS) * gn_ref[...] * _silu(z_ref[...])).astype(o_ref.dtype)


def _gdn(u, bsz, seq, reverse, cwq, cwk, cwv, alog, dtb, eb, eg, gn=None, ob=None):
    ts = 256
    ns = seq // ts
    t = bsz * seq
    specs = []
    for cb in (CB_QB, CB_KB, CB_VB):
        specs += list(_row_specs(ts, ns, t, cb, reverse))
    tbf = (lambda i: ns - 1 - i) if reverse else (lambda i: i)
    specs.append(pl.BlockSpec((ts, LANES), lambda b, i: (b * ns + tbf(i), CB128_AB)))
    specs += [_const_spec((4, W_GROUP))] * 3
    specs += [_const_spec((1, LANES))] * 2 + [_const_spec((LANES, W_GROUP))] * 2
    args = [u] * 10 + [cwq, cwk, cwv, alog, dtb, eb, eg]
    if reverse:
        out_dtype = F32
    else:
        specs += [pl.BlockSpec((ts, W_GROUP), lambda b, i: (b * ns + i, CB_ZB)),
                  pl.BlockSpec((ts, W_GROUP), lambda b, i: (b * ns + i, 0)),
                  _const_spec((1, W_GROUP))]
        args += [u, ob, gn]
        out_dtype = BF16
    return pl.pallas_call(
        functools.partial(_gdn_kernel, reverse),
        grid=(bsz, ns),
        in_specs=specs,
        out_specs=pl.BlockSpec((ts, W_GROUP), lambda b, i: (b * ns + tbf(i), 0)),
        out_shape=jax.ShapeDtypeStruct((t, W_GROUP), out_dtype),
        scratch_shapes=[pltpu.VMEM((W_GROUP, W_GROUP), F32)],
        compiler_params=_params(("parallel", "arbitrary")),
        name="gdn_bwd" if reverse else "gdn_fwd",
    )(*args)


RET_CHUNK = 128


def _ret_kernel(reverse, *refs):
    if reverse:
        (q_ref, k_ref, v_ref, cos_ref, sin_ref, lgq_ref, lgv_ref, o_ref, s_ref) = refs
    else:
        (q_ref, k_ref, v_ref, cos_ref, sin_ref, lgq_ref, lgv_ref, lgfc_ref, lgbc_ref, g_ref, ob_ref, rn_ref,
         o_ref, s_ref) = refs
    cr = q_ref.shape[0]

    @pl.when(pl.program_id(1) == 0)
    def _():
        s_ref[...] = jnp.zeros_like(s_ref)

    cos = cos_ref[...]
    sin = sin_ref[...]

    def rope(t):
        t1, t2 = t[:, :LANES], t[:, LANES:]
        return jnp.concatenate([t1 * cos - t2 * sin, t1 * sin + t2 * cos], axis=-1)

    q = rope(q_ref[...])
    k = rope(k_ref[...]) * (HEAD_DIM ** -0.5)
    v = v_ref[...]
    lgq = lgq_ref[...]
    lgv = lgv_ref[...]
    pos = _iota((cr, W_GROUP), 0).astype(F32)
    rows_qk = (_iota((W_GROUP, W_GROUP), 0) % LANES) // 32
    blk_state = rows_qk == _iota((W_GROUP, W_GROUP), 1) // HEAD_DIM
    s = s_ref[...]
    if reverse:
        q_dec = q * jnp.exp((cr - pos) * lgq)
        k_dec = k * jnp.exp(pos * lgq)
    else:
        q_dec = q * jnp.exp((pos + 1.0) * lgq)
        k_dec = k * jnp.exp((cr - 1.0 - pos) * lgq)
    cross = _dot(q_dec.astype(BF16), s.astype(BF16))
    s_ref[...] = s * jnp.exp(cr * lgv) + jnp.where(blk_state, _dot_tn(k_dec.astype(BF16), v.astype(BF16)), 0.0)
    if reverse:
        o_ref[...] = cross
        return
    nk = N_HEADS * cr
    stack_head = _iota((nk, W_GROUP), 0) // cr
    kst = _stack4(k.astype(BF16), stack_head == (_iota((nk, W_GROUP), 1) % LANES) // 32)
    vst = _stack4(v.astype(BF16), stack_head == _iota((nk, W_GROUP), 1) // HEAD_DIM)
    qk = _dot_nt(q.astype(BF16), kst)
    ii = _iota((cr, nk), 0)
    jj = _iota((cr, nk), 1) % cr
    dist = (ii - jj).astype(F32)
    lower, upper = jj <= ii, jj >= ii
    dm = (jnp.where(lower, jnp.exp(jnp.where(lower, dist * lgfc_ref[...], 0.0)), 0.0)
          + jnp.where(upper, jnp.exp(jnp.where(upper, -dist * lgbc_ref[...], 0.0)), 0.0))
    o = _dot((qk * dm).astype(BF16), vst) + cross + ob_ref[...]
    avg = _head_ones() * (1.0 / HEAD_DIM)
    oc = o - _dot_f32(o, avg)
    var = _dot_f32(oc * oc, avg)
    o_ref[...] = (oc * lax.rsqrt(var + EPS) * rn_ref[...] * _silu(g_ref[...])).astype(o_ref.dtype)


def _ret(u, bsz, seq, reverse, cos, sin, lgq, lgv, lgfc=None, lgbc=None, rn=None, ob=None):
    cr = RET_CHUNK
    ns = seq // cr
    t = bsz * seq
    tbf = (lambda i: ns - 1 - i) if reverse else (lambda i: i)

    def col(cb):
        return pl.BlockSpec((cr, W_GROUP), lambda b, i: (b * ns + tbf(i), cb))

    tab = pl.BlockSpec((cr, LANES), lambda b, i: (tbf(i), 0))
    specs = [col(CB_QC), col(CB_KC), col(CB_VC), tab, tab, _const_spec((1, W_GROUP)), _const_spec((1, W_GROUP))]
    args = [u, u, u, cos, sin, lgq, lgv]
    if reverse:
        out_dtype = F32
    else:
        specs += [_const_spec((1, N_HEADS * cr)), _const_spec((1, N_HEADS * cr)), col(CB_GC),
                  pl.BlockSpec((cr, W_GROUP), lambda b, i: (b * ns + i, 0)), _const_spec((1, W_GROUP))]
        args += [lgfc, lgbc, u, ob, rn]
        out_dtype = BF16
    return pl.pallas_call(
        functools.partial(_ret_kernel, reverse),
        grid=(bsz, ns),
        in_specs=specs,
        out_specs=pl.BlockSpec((cr, W_GROUP), lambda b, i: (b * ns + tbf(i), 0)),
        out_shape=jax.ShapeDtypeStruct((t, W_GROUP), out_dtype),
        scratch_shapes=[pltpu.VMEM((W_GROUP, W_GROUP), F32)],
        compiler_params=_params(("parallel", "arbitrary")),
        name="ret_bwd" if reverse else "ret_fwd",
    )(*args)


def _na_prep_kernel(q_ref, k_ref, v_ref, gq_ref, gk_ref, qo_ref, ko_ref, vo_ref):
    avg = _head_ones() * (1.0 / HEAD_DIM)
    q = q_ref[...]
    k = k_ref[...]
    qo_ref[...] = (q * lax.rsqrt(_dot_f32(q * q, avg) + EPS) * gq_ref[...] * (HEAD_DIM ** -0.5)).astype(BF16)
    ko_ref[...] = (k * lax.rsqrt(_dot_f32(k * k, avg) + EPS) * gk_ref[...]).astype(BF16)
    vo_ref[...] = v_ref[...].astype(BF16)


def _na_prep(u, gq, gk):
    t = u.shape[0]
    tm = 512

    def col(cb):
        return pl.BlockSpec((tm, W_GROUP), lambda i: (i, cb))

    out = pl.BlockSpec((tm, W_GROUP), lambda i: (i, 0))
    shp = jax.ShapeDtypeStruct((t, W_GROUP), BF16)
    return pl.pallas_call(
        _na_prep_kernel,
        grid=(t // tm,),
        in_specs=[col(CB_QD), col(CB_KD), col(CB_VD), _const_spec((1, W_GROUP)), _const_spec((1, W_GROUP))],
        out_specs=[out, out, out],
        out_shape=[shp, shp, shp],
        compiler_params=_params(("parallel",)),
        name="na_prep",
    )(u, u, u, gq, gk)


NA_GROUP = 8
NA_WIN = NA_KH * GRID_W


def _na_kernel(rows, q_ref, kp_ref, kc_ref, kn_ref, vp_ref, vc_ref, vn_ref, tab_ref, o_ref, kw, vw):
    g = pl.program_id(1)
    blk = NA_GROUP * GRID_W
    kw[0:blk, :] = kp_ref[...]
    kw[blk:2 * blk, :] = kc_ref[...]
    kw[2 * blk:3 * blk, :] = kn_ref[...]
    vw[0:blk, :] = vp_ref[...]
    vw[blk:2 * blk, :] = vc_ref[...]
    vw[2 * blk:3 * blk, :] = vn_ref[...]
    for j in range(NA_GROUP):
        r = g * NA_GROUP + j
        r0 = jnp.clip(r - NA_KH // 2, 0, rows - NA_KH)
        off = pl.multiple_of((r0 - g * NA_GROUP + NA_GROUP) * GRID_W, GRID_W)
        didx = r - r0
        qr = q_ref[j * GRID_W:(j + 1) * GRID_W, :]
        kwin = kw[pl.ds(off, NA_WIN), :]
        vwin = vw[pl.ds(off, NA_WIN), :]
        heads = []
        for h in range(N_HEADS):
            hs = slice(h * HEAD_DIM, (h + 1) * HEAD_DIM)
            sc = _dot_nt(qr[:, hs], kwin[:, hs]) + tab_ref[didx, h]
            m = jnp.max(sc, axis=-1, keepdims=True)
            p = jnp.exp(sc - m)
            l = jnp.sum(p, axis=-1, keepdims=True)
            heads.append(_dot(p.astype(BF16), vwin[:, hs]) / l)
        o_ref[j * GRID_W:(j + 1) * GRID_W, :] = jnp.concatenate(heads, axis=-1).astype(o_ref.dtype)


def _na(qn, kn, vn, table, bsz, seq):
    rows = seq // GRID_W
    ng = rows // NA_GROUP
    blk = NA_GROUP * GRID_W
    t = bsz * seq
    cur = pl.BlockSpec((blk, W_GROUP), lambda b, g: (b * ng + g, 0))
    prev = pl.BlockSpec((blk, W_GROUP), lambda b, g: (b * ng + jnp.maximum(g - 1, 0), 0))
    nxt = pl.BlockSpec((blk, W_GROUP), lambda b, g: (b * ng + jnp.minimum(g + 1, ng - 1), 0))
    return pl.pallas_call(
        functools.partial(_na_kernel, rows),
        grid=(bsz, ng),
        in_specs=[cur, prev, cur, nxt, prev, cur, nxt, _const_spec((NA_KH, N_HEADS, GRID_W, NA_WIN))],
        out_specs=cur,
        out_shape=jax.ShapeDtypeStruct((t, W_GROUP), BF16),
        scratch_shapes=[pltpu.VMEM((3 * blk, W_GROUP), BF16), pltpu.VMEM((3 * blk, W_GROUP), BF16)],
        compiler_params=_params(("parallel", "parallel")),
        name="na",
    )(qn, kn, kn, kn, vn, vn, vn, table)


def _na_table(rpb):
    didx = np.arange(NA_KH)[:, None, None, None]
    kk = np.arange(NA_KH)[None, :, None, None]
    cc = np.arange(GRID_W)[None, None, :, None]
    kc = np.arange(GRID_W)[None, None, None, :]
    c0 = np.clip(cc - NA_KW // 2, 0, GRID_W - NA_KW)
    valid = np.broadcast_to((kc >= c0) & (kc < c0 + NA_KW), (NA_KH, NA_KH, GRID_W, GRID_W))
    dr = np.broadcast_to(kk + (NA_KH - 1) - didx, valid.shape)
    dc = np.broadcast_to(np.clip(kc - cc + (NA_KW - 1), 0, 2 * NA_KW - 2), valid.shape)
    bias = rpb[:, dr, dc]
    tab = jnp.where(valid[None], bias, NEG_BIG)
    return jnp.transpose(tab, (1, 0, 3, 2, 4)).reshape(NA_KH, N_HEADS, GRID_W, NA_WIN)


def _outproj_kernel(x_ref, a_ref, b_ref, c_ref, d_ref, w_ref, o_ref):
    acc = x_ref[...]
    for gi, m_ref in enumerate((a_ref, b_ref, c_ref, d_ref)):
        acc = acc + _dot(m_ref[...], w_ref[gi * W_GROUP:(gi + 1) * W_GROUP, :])
    o_ref[...] = acc


def _outproj(x, oa, ob, oc, od, w):
    t = x.shape[0]
    tm = 512
    xs = pl.BlockSpec((tm, D_MODEL), lambda i: (i, 0))
    ms = pl.BlockSpec((tm, W_GROUP), lambda i: (i, 0))
    return pl.pallas_call(
        _outproj_kernel,
        grid=(t // tm,),
        in_specs=[xs, ms, ms, ms, ms, _const_spec((D_MODEL, D_MODEL))],
        out_specs=xs,
        out_shape=jax.ShapeDtypeStruct((t, D_MODEL), F32),
        compiler_params=_params(("parallel",)),
        name="outproj",
    )(x, oa, ob, oc, od, w)


FF_CHUNK = 256


def _ffn_kernel(xm_ref, xp_ref, xn_ref, p_ref, n2_ref, wg_ref, wu_ref, cw_ref, cb_ref, wd_ref,
                n3_ref, pg_ref, pp_ref, o_ref):
    tm = xm_ref.shape[0]
    i = pl.program_id(1)
    first = i == 0
    last = i == pl.num_programs(1) - 1
    x = xm_ref[...]
    n2 = n2_ref[...]
    xp = jnp.where(first, 0.0, xp_ref[...])
    xn = jnp.where(last, 0.0, xn_ref[...])
    h2f = _rms(x, n2)
    h2 = h2f.astype(BF16)
    h2e = jnp.concatenate([_rms(xp, n2), h2f, _rms(xn, n2)], axis=0).astype(BF16)
    acc = x
    for f in range(D_FF // FF_CHUNK):
        fs = slice(f * FF_CHUNK, (f + 1) * FF_CHUNK)
        ge = _dot(h2e, wg_ref[:, fs])
        gate = (cw_ref[0:1, fs] * ge[7:7 + tm] + cw_ref[1:2, fs] * ge[8:8 + tm]
                + cw_ref[2:3, fs] * ge[9:9 + tm] + cb_ref[:, fs])
        up = _dot(h2, wu_ref[:, fs])
        acc = acc + _dot((_gelu_tanh(gate) * up).astype(BF16), wd_ref[fs, :])
    h3 = _rms(acc, n3_ref[...]).astype(BF16)
    gate3 = _sigmoid(_dot(h3, pg_ref[...]))
    o_ref[...] = acc + gate3 * _dot(p_ref[...].astype(BF16), pp_ref[...])


def _ffn(x, p, bsz, seq, n2, wg, wu, cw, cb, wd, n3, pg, pp):
    tm = 256
    ns = seq // tm
    t = bsz * seq
    xm, xp, xn = _row_specs(tm, ns, t, 0, False, width=D_MODEL)
    return pl.pallas_call(
        _ffn_kernel,
        grid=(bsz, ns),
        in_specs=[xm, xp, xn, pl.BlockSpec((tm, PLE_DIM), lambda b, i: (b * ns + i, 0)),
                  _const_spec((1, D_MODEL)), _const_spec((D_MODEL, D_FF)), _const_spec((D_MODEL, D_FF)),
                  _const_spec((3, D_FF)), _const_spec((1, D_FF)), _const_spec((D_FF, D_MODEL)),
                  _const_spec((1, D_MODEL)), _const_spec((D_MODEL, D_MODEL)), _const_spec((PLE_DIM, D_MODEL))],
        out_specs=pl.BlockSpec((tm, D_MODEL), lambda b, i: (b * ns + i, 0)),
        out_shape=jax.ShapeDtypeStruct((t, D_MODEL), F32),
        compiler_params=_params(("parallel", "parallel")),
        name="ffn_ple",
    )(x, x, x, p, n2, wg, wu, cw, cb, wd, n3, pg, pp)


def _block_diag(w):
    h, n, _ = w.shape
    eye = jnp.eye(h, dtype=w.dtype)
    return (eye[:, None, :, None] * w[:, :, None, :]).reshape(h * n, h * n)


def _row(v):
    return v.reshape(1, -1).astype(F32)


def _lane_pad(v8, offset):
    return jnp.zeros((1, LANES), F32).at[0, offset:offset + 2 * N_HEADS].set(v8.reshape(-1).astype(F32))


def _expander(offset):
    e = np.zeros((LANES, W_GROUP), np.float32)
    for h in range(N_HEADS):
        e[offset + h, h * HEAD_DIM:(h + 1) * HEAD_DIM] = 1.0
    return jnp.asarray(e)


def _layer(x, p, lw, bsz, seq):
    perm = _in_perm()
    w_in = jnp.concatenate([lw['w_in'], jnp.zeros((D_MODEL, 1), F32)], axis=1)[:, perm].astype(BF16)
    u = _inproj(x, _row(lw['norm1']), w_in)

    cw_a, cb_a = lw['conv_a_w'].astype(F32), _row(lw['conv_a_b'])
    h_dir = None
    for d in (1, 0):
        h_dir = _rglru(u, bsz, seq, d == 1, cw_a, cb_a,
                       _block_diag(lw['lru_wr'][d]).astype(BF16), _row(lw['lru_br'][d]),
                       _block_diag(lw['lru_wi'][d]).astype(BF16), _row(lw['lru_bi'][d]),
                       _row(lw['lru_lambda'][d]), hb=h_dir)
    out_a = h_dir

    gconv = lw['gdn_conv'].astype(F32)
    cwq, cwk, cwv = gconv[:, :W_GROUP], gconv[:, W_GROUP:2 * W_GROUP], gconv[:, 2 * W_GROUP:]
    alog = _lane_pad(lw['gdn_a_log'], 2 * N_HEADS)
    dtb = _lane_pad(lw['gdn_dt_bias'], 2 * N_HEADS)
    gn = _row(jnp.tile(lw['gdn_norm'], N_HEADS))
    o_dir = None
    for d in (1, 0):
        o_dir = _gdn(u, bsz, seq, d == 1, cwq, cwk, cwv, alog, dtb,
                     _expander(d * N_HEADS), _expander(2 * N_HEADS + d * N_HEADS), gn=gn, ob=o_dir)
    out_b = o_dir

    half = HEAD_DIM // 2
    inv_freq = ROPE_BASE ** (-jnp.arange(half, dtype=F32) / half)
    ang = jnp.arange(seq, dtype=F32)[:, None] * inv_freq[None, :]
    cos = jnp.tile(jnp.cos(ang), (1, N_HEADS))
    sin = jnp.tile(jnp.sin(ang), (1, N_HEADS))
    log_gamma = jnp.log1p(-jnp.exp2(-lw['ret_decay'].astype(F32)))
    rn = _row(jnp.tile(lw['ret_norm'], N_HEADS))
    o_dir = None
    for d in (1, 0):
        lg = log_gamma[d]
        lgq = _row(jnp.tile(jnp.repeat(lg, half), 2))
        lgv = _row(jnp.repeat(lg, HEAD_DIM))
        if d == 1:
            o_dir = _ret(u, bsz, seq, True, cos, sin, lgq, lgv)
        else:
            o_dir = _ret(u, bsz, seq, False, cos, sin, lgq, lgv,
                         lgfc=_row(jnp.repeat(log_gamma[0], RET_CHUNK)), lgbc=_row(jnp.repeat(log_gamma[1], RET_CHUNK)),
                         rn=rn, ob=o_dir)
    out_c = o_dir

    qn, kn, vn = _na_prep(u, _row(jnp.tile(lw['na_qnorm'], N_HEADS)), _row(jnp.tile(lw['na_knorm'], N_HEADS)))
    out_d = _na(qn, kn, vn, _na_table(lw['na_rpb'].astype(F32)), bsz, seq)

    x = _outproj(x, out_a, out_b, out_c, out_d, lw['w_out'].astype(BF16))
    return _ffn(x, p, bsz, seq, _row(lw['norm2']), lw['ffn_wg'].astype(BF16), lw['ffn_wu'].astype(BF16),
                lw['ffn_conv_w'].astype(F32), _row(lw['ffn_conv_b']), lw['ffn_wd'].astype(BF16),
                _row(lw['norm3']), lw['ple_gate'].astype(BF16), lw['ple_proj'].astype(BF16))


_WEIGHT_NAMES = ('norm1', 'norm2', 'norm3', 'w_in', 'w_out', 'conv_a_w', 'conv_a_b', 'lru_wr', 'lru_br',
                 'lru_wi', 'lru_bi', 'lru_lambda', 'gdn_conv', 'gdn_a_log', 'gdn_dt_bias', 'gdn_norm',
                 'ret_decay', 'ret_norm', 'na_qnorm', 'na_knorm', 'na_rpb', 'ffn_wg', 'ffn_wu',
                 'ffn_conv_w', 'ffn_conv_b', 'ffn_wd', 'ple_proj', 'ple_gate')


def _trunk(x, p, weights):
    bsz, seq, _ = x.shape
    xf = x.reshape(bsz * seq, D_MODEL)
    for i in range(DEPTH):
        lw = {n: w[i] for n, w in weights.items()}
        xf = _layer(xf, p[i].reshape(bsz * seq, PLE_DIM), lw, bsz, seq)
    return xf.reshape(bsz, seq, D_MODEL)


def kernel(x_prompt, x_sample, p_prompt, p_sample, norm1, norm2, norm3, w_in, w_out, conv_a_w, conv_a_b, lru_wr, lru_br, lru_wi, lru_bi, lru_lambda, gdn_conv, gdn_a_log, gdn_dt_bias, gdn_norm, ret_decay, ret_norm, na_qnorm, na_knorm, na_rpb, ffn_wg, ffn_wu, ffn_conv_w, ffn_conv_b, ffn_wd, ple_proj, ple_gate):
    weights = dict(zip(_WEIGHT_NAMES, (norm1, norm2, norm3, w_in, w_out, conv_a_w, conv_a_b, lru_wr, lru_br,
                                       lru_wi, lru_bi, lru_lambda, gdn_conv, gdn_a_log, gdn_dt_bias, gdn_norm,
                                       ret_decay, ret_norm, na_qnorm, na_knorm, na_rpb, ffn_wg, ffn_wu,
                                       ffn_conv_w, ffn_conv_b, ffn_wd, ple_proj, ple_gate)))
    return (_trunk(x_prompt, p_prompt, weights), _trunk(x_sample, p_sample, weights))
```

```python
import functools

import jax
import jax.numpy as jnp
import numpy as np
from jax import lax
from jax.experimental import pallas as pl
from jax.experimental.pallas import tpu as pltpu

F32 = jnp.float32
BF16 = jnp.bfloat16

D_MODEL = 1024
DEPTH = 2
PLE_DIM = 256
GRID_W = 64
W_GROUP = 256
HEAD_DIM = 64
N_HEADS = 4
LRU_C = 8.0
CHUNK = 64
ROPE_BASE = 10000.0
NA_KH = 8
NA_KW = 16
D_FF = 2816
EPS = 1e-6
NEG_BIG = -1e30

SUBLANES = 8
LANES = 128
VMEM_LIMIT = 56 * 1024 * 1024

CB_XA, CB_YA, CB_QB, CB_KB, CB_VB, CB_ZB, CB_QC, CB_KC, CB_VC, CB_GC, CB_QD, CB_KD, CB_VD = range(13)
CB128_AB = 26
IN_COLS_PAD = 13 * W_GROUP + LANES
IN_COLS = 13 * W_GROUP + 4 * N_HEADS


def _reorder_in_cols(w):
    rows = w.shape[0]
    half = HEAD_DIM // 2

    def halves_first(cols):
        return cols.reshape(rows, N_HEADS, 2, half).transpose(0, 2, 1, 3).reshape(rows, W_GROUP)

    ab0 = 6 * W_GROUP
    qc0 = ab0 + 4 * N_HEADS
    parts = [w[:, :ab0], halves_first(w[:, qc0:qc0 + W_GROUP]), halves_first(w[:, qc0 + W_GROUP:qc0 + 2 * W_GROUP]),
             w[:, qc0 + 2 * W_GROUP:], w[:, ab0:qc0], jnp.zeros((rows, LANES - 4 * N_HEADS), w.dtype)]
    out = jnp.concatenate(parts, axis=1)
    assert out.shape[1] == IN_COLS_PAD
    return out


def _sigmoid(x):
    return 1.0 / (1.0 + jnp.exp(-x))


def _silu(x):
    return x * _sigmoid(x)


def _softplus(x):
    return jnp.maximum(x, 0.0) + jnp.log1p(jnp.exp(-jnp.abs(x)))


def _gelu_tanh(x):
    return x * (0.5 * (1.0 + jnp.tanh(0.7978845608028654 * (x + 0.044715 * (x * x * x)))))


def _rms(x, gain):
    ms = jnp.mean(x * x, axis=-1, keepdims=True)
    return x * lax.rsqrt(ms + EPS) * gain


def _dot(a, b):
    return jnp.dot(a, b, preferred_element_type=F32)


def _split_bf16(x, terms):
    parts = []
    for _ in range(terms):
        p = x.astype(BF16)
        parts.append(p)
        x = x - p.astype(F32)
    return parts


def _dot_sel(x, sel, terms):
    return sum(_dot(p, sel) for p in _split_bf16(x, terms))


def _sel_dot(sel, x, terms):
    return sum(_dot(sel, p) for p in _split_bf16(x, terms))


def _dot_nt(a, b):
    return lax.dot_general(a, b, (((1,), (1,)), ((), ())), preferred_element_type=F32)


def _dot_tn(a, b):
    return lax.dot_general(a, b, (((0,), (0,)), ((), ())), preferred_element_type=F32)


def _iota(shape, dim):
    return lax.broadcasted_iota(jnp.int32, shape, dim)


def _head_ones():
    return (_iota((W_GROUP, W_GROUP), 0) // HEAD_DIM == _iota((W_GROUP, W_GROUP), 1) // HEAD_DIM).astype(BF16)


def _stack4(x, mask):
    return jnp.where(mask, jnp.concatenate([x] * N_HEADS, axis=0), jnp.zeros((), x.dtype))


def _conv4(xm, xp, xn, w_ref, first, last):
    ts = xm.shape[0]
    xp = jnp.where(first, 0.0, xp)
    xn = jnp.where(last, 0.0, xn)
    xe = jnp.concatenate([xp, xm, xn], axis=0)
    out = w_ref[0:1, :] * xe[6:6 + ts]
    out = out + w_ref[1:2, :] * xe[7:7 + ts]
    out = out + w_ref[2:3, :] * xe[8:8 + ts]
    out = out + w_ref[3:4, :] * xe[9:9 + ts]
    return out


def _time_block(reverse):
    i = pl.program_id(1)
    n = pl.num_programs(1)
    tb = (n - 1 - i) if reverse else i
    return tb, tb == 0, tb == n - 1


def _row_specs(ts, ns, total_rows, col_block, reverse, width=W_GROUP):
    r8 = ts // SUBLANES
    last8 = total_rows // SUBLANES - 1

    def tb(i):
        return (ns - 1 - i) if reverse else i

    main = pl.BlockSpec((ts, width), lambda b, i: (b * ns + tb(i), col_block))
    prev = pl.BlockSpec((SUBLANES, width), lambda b, i: (jnp.maximum((b * ns + tb(i)) * r8 - 1, 0), col_block))
    nxt = pl.BlockSpec((SUBLANES, width), lambda b, i: (jnp.minimum((b * ns + tb(i) + 1) * r8, last8), col_block))
    return main, prev, nxt


def _const_spec(shape):
    nd = len(shape)
    return pl.BlockSpec(shape, lambda *_: (0,) * nd)


def _params(sem):
    return pltpu.CompilerParams(dimension_semantics=sem, vmem_limit_bytes=VMEM_LIMIT)


def _inproj_kernel(x_ref, g_ref, w_ref, o_ref):
    h = _rms(x_ref[...], g_ref[...]).astype(BF16)
    o_ref[...] = _dot(h, w_ref[...])


def _inproj(x, gain, w):
    t = x.shape[0]
    tm = 256
    return pl.pallas_call(
        _inproj_kernel,
        grid=(t // tm,),
        in_specs=[pl.BlockSpec((tm, D_MODEL), lambda i: (i, 0)),
                  _const_spec((1, D_MODEL)),
                  _const_spec((D_MODEL, IN_COLS_PAD))],
        out_specs=pl.BlockSpec((tm, IN_COLS_PAD), lambda i: (i, 0)),
        out_shape=jax.ShapeDtypeStruct((t, IN_COLS_PAD), F32),
        compiler_params=_params(("parallel",)),
        name="inproj",
    )(x, gain, w)


def _rglru_kernel(reverse, *refs):
    if reverse:
        (xm_ref, xp_ref, xn_ref, cw_ref, cb_ref, wr_ref, br_ref, wi_ref, bi_ref, lam_ref,
         o_ref, a_s, b_s, p_s, ac_s, c_s, carry, h_s) = refs
    else:
        (xm_ref, xp_ref, xn_ref, cw_ref, cb_ref, wr_ref, br_ref, wi_ref, bi_ref, lam_ref, ya_ref, hb_ref,
         o_ref, a_s, b_s, p_s, ac_s, c_s, carry, h_s) = refs
    ts = xm_ref.shape[0]
    ng = ts // SUBLANES
    tb, first, last = _time_block(reverse)

    @pl.when(pl.program_id(1) == 0)
    def _():
        carry[...] = jnp.zeros_like(carry)

    x = _conv4(xm_ref[...], xp_ref[...], xn_ref[...], cw_ref, first, last) + cb_ref[...]
    xb = x.astype(BF16)
    r = _sigmoid(_dot(xb, wr_ref[...]) + br_ref[...])
    ig = _sigmoid(_dot(xb, wi_ref[...]) + bi_ref[...])
    log_a = (-LRU_C) * r * _softplus(-lam_ref[...])
    a = jnp.exp(log_a)
    th = jnp.tanh(-log_a)
    bx = jnp.sqrt(2.0 * th / (1.0 + th)) * (ig * x)
    nl = W_GROUP // LANES
    for hf in range(nl):
        a_s[hf] = a[:, hf * LANES:(hf + 1) * LANES]
        b_s[hf] = bx[:, hf * LANES:(hf + 1) * LANES]

    def slab(ref, k):
        return jnp.concatenate([ref[hf, pl.ds(k, ng, stride=SUBLANES), :] for hf in range(nl)], axis=-1)

    order = list(range(SUBLANES))[::-1] if reverse else list(range(SUBLANES))
    p = None
    ac = None
    for k in order:
        a_k = slab(a_s, k)
        b_k = slab(b_s, k)
        if p is None:
            p, ac = b_k, a_k
        else:
            p = a_k * p + b_k
            ac = a_k * ac
        p_s[k] = p
        ac_s[k] = ac

    k_last = order[-1]

    def body(g, c):
        gi = (ng - 1 - g) if reverse else g
        c_s[pl.ds(gi, 1), :] = c
        return ac_s[k_last, pl.ds(gi, 1), :] * c + p_s[k_last, pl.ds(gi, 1), :]

    carry[...] = lax.fori_loop(0, ng, body, carry[...])
    c_in = c_s[...]
    for k in order:
        h_k = p_s[k] + ac_s[k] * c_in
        for hf in range(nl):
            h_s[hf, pl.ds(k, ng, stride=SUBLANES), :] = h_k[:, hf * LANES:(hf + 1) * LANES]
    h = jnp.concatenate([h_s[hf] for hf in range(nl)], axis=-1)
    if reverse:
        o_ref[...] = h
    else:
        o_ref[...] = ((h + hb_ref[...]) * _gelu_tanh(ya_ref[...])).astype(o_ref.dtype)


def _rglru(u, bsz, seq, reverse, cw, cb, wr, br, wi, bi, lam, hb=None):
    ts = 512
    ns = seq // ts
    t = bsz * seq
    ng = ts // SUBLANES
    xm, xp, xn = _row_specs(ts, ns, t, CB_XA, reverse)
    in_specs = [xm, xp, xn, _const_spec((4, W_GROUP)), _const_spec((1, W_GROUP)),
                _const_spec((W_GROUP, W_GROUP)), _const_spec((1, W_GROUP)),
                _const_spec((W_GROUP, W_GROUP)), _const_spec((1, W_GROUP)), _const_spec((1, W_GROUP))]
    args = [u, u, u, cw, cb, wr, br, wi, bi, lam]
    halves = (W_GROUP // LANES, ts, LANES)
    scratch = [pltpu.VMEM(halves, F32), pltpu.VMEM(halves, F32),
               pltpu.VMEM((SUBLANES, ng, W_GROUP), F32), pltpu.VMEM((SUBLANES, ng, W_GROUP), F32),
               pltpu.VMEM((ng, W_GROUP), F32), pltpu.VMEM((1, W_GROUP), F32), pltpu.VMEM(halves, F32)]
    if reverse:
        out_dtype = F32
        out_spec = pl.BlockSpec((ts, W_GROUP), lambda b, i: (b * ns + ns - 1 - i, 0))
    else:
        ya = _row_specs(ts, ns, t, CB_YA, False)[0]
        in_specs += [ya, pl.BlockSpec((ts, W_GROUP), lambda b, i: (b * ns + i, 0))]
        args += [u, hb]
        out_dtype = BF16
        out_spec = pl.BlockSpec((ts, W_GROUP), lambda b, i: (b * ns + i, 0))
    return pl.pallas_call(
        functools.partial(_rglru_kernel, reverse),
        grid=(bsz, ns),
        in_specs=in_specs,
        out_specs=out_spec,
        out_shape=jax.ShapeDtypeStruct((t, W_GROUP), out_dtype),
        scratch_shapes=scratch,
        compiler_params=_params(("parallel", "arbitrary")),
        name="rglru_bwd" if reverse else "rglru_fwd",
    )(*args)


def _gdn_kernel(reverse, *refs):
    if reverse:
        (qm, qp, qn, km, kp, kn, vm, vp, vn, ab_ref, cwq, cwk, cwv, alog_ref, dtb_ref, eb_ref, eg_ref,
         o_ref, s_ref) = refs
    else:
        (qm, qp, qn, km, kp, kn, vm, vp, vn, ab_ref, cwq, cwk, cwv, alog_ref, dtb_ref, eb_ref, eg_ref,
         z_ref, ob_ref, gn_ref, o_ref, s_ref) = refs
    ts = qm.shape[0]
    nc = ts // CHUNK
    c = CHUNK
    tb, first, last = _time_block(reverse)

    @pl.when(pl.program_id(1) == 0)
    def _():
        s_ref[...] = jnp.zeros_like(s_ref)

    ones_h = _head_ones()
    q = _silu(_conv4(qm[...], qp[...], qn[...], cwq, first, last))
    k = _silu(_conv4(km[...], kp[...], kn[...], cwk, first, last))
    v = _silu(_conv4(vm[...], vp[...], vn[...], cwv, first, last))
    q = q * lax.rsqrt(_dot_sel(q * q, ones_h, 2) + EPS) * (HEAD_DIM ** -0.5)
    k = k * lax.rsqrt(_dot_sel(k * k, ones_h, 2) + EPS)
    ab = ab_ref[...]
    beta_b = _dot_sel(_sigmoid(ab), eb_ref[...], 2)
    g_b = _dot_sel(-jnp.exp(alog_ref[...]) * _softplus(ab + dtb_ref[...]), eg_ref[...], 3)

    ii = _iota((c, W_GROUP), 0)
    jj = _iota((c, W_GROUP), 1) % c
    if reverse:
        incl, strict = jj >= ii, jj > ii
        tri = (_iota((c, c), 1) >= _iota((c, c), 0)).astype(BF16)
        ucat = (ii < jj).astype(F32)
    else:
        incl, strict = jj <= ii, jj < ii
        tri = (_iota((c, c), 1) <= _iota((c, c), 0)).astype(BF16)
        ucat = (ii > jj).astype(F32)
    eye_cat = (ii == jj).astype(F32)
    blk = _iota((W_GROUP, W_GROUP), 0) // HEAD_DIM == _iota((W_GROUP, W_GROUP), 1) // HEAD_DIM

    s = s_ref[...]
    chunks = list(range(nc))[::-1] if reverse else list(range(nc))
    outs = [None] * nc
    for ci in chunks:
        sl = slice(ci * c, (ci + 1) * c)
        qc, kc, vc, bc, gc_ = q[sl], k[sl], v[sl], beta_b[sl], g_b[sl]
        gcum = _sel_dot(tri, gc_, 3)
        dmat = _sel_dot(tri, gc_ * ucat, 3)
        decay = jnp.where(incl, jnp.exp(jnp.where(incl, dmat, 0.0)), 0.0)
        kst = _stack4(kc.astype(BF16), blk)
        kq = _dot_nt(jnp.concatenate([kc, qc], axis=0).astype(BF16), kst)
        lmat = jnp.where(strict, bc * kq[:c] * decay, 0.0)
        attn = kq[c:] * decay
        pinv = eye_cat - lmat
        lpow = _dot(lmat.astype(BF16), _stack4(lmat.astype(BF16), blk))
        for lvl in range(5):
            z = _stack4(lpow.astype(BF16), blk)
            if lvl < 4:
                r2 = _dot(jnp.concatenate([pinv, lpow], axis=0).astype(BF16), z)
                pinv = pinv + r2[:c]
                lpow = r2[c:]
            else:
                pinv = pinv + _dot(pinv.astype(BF16), z)
        egc = jnp.exp(gcum)
        vb = (vc * bc).astype(BF16)
        kb = (kc * bc * egc).astype(BF16)
        pb = pinv.astype(BF16)
        u_val = _dot(pb, _stack4(vb, blk))
        w_key = _dot(pb, _stack4(kb, blk))
        q_g = qc * egc
        g_last = gcum[0:1, :] if reverse else gcum[c - 1:c, :]
        k_g = kc * jnp.exp(g_last - gcum)
        ws = _dot(jnp.concatenate([w_key, q_g], axis=0).astype(BF16), s.astype(BF16))
        v_new = u_val - ws[:c]
        vnb = v_new.astype(BF16)
        outs[ci] = ws[c:] + _dot(attn.astype(BF16), _stack4(vnb, blk))
        s = s * jnp.exp(g_last) + jnp.where(blk, _dot_tn(k_g.astype(BF16), vnb), 0.0)
    s_ref[...] = s
    o = jnp.concatenate(outs, axis=0)
    if reverse:
        o_ref[...] = o
    else:
        o = o + ob_ref[...]
        ms = _dot_sel(o * o, ones_h, 2) * (1.0 / HEAD_DIM)
        o_ref[...] = (o * lax.rsqrt(ms + EPS) * gn_ref[...] * _silu(z_ref[...])).astype(o_ref.dtype)


def _gdn(u, bsz, seq, reverse, cwq, cwk, cwv, alog, dtb, eb, eg, gn=None, ob=None):
    ts = 256
    ns = seq // ts
    t = bsz * seq
    specs = []
    for cb in (CB_QB, CB_KB, CB_VB):
        specs += list(_row_specs(ts, ns, t, cb, reverse))
    tbf = (lambda i: ns - 1 - i) if reverse else (lambda i: i)
    specs.append(pl.BlockSpec((ts, LANES), lambda b, i: (b * ns + tbf(i), CB128_AB)))
    specs += [_const_spec((4, W_GROUP))] * 3
    specs += [_const_spec((1, LANES))] * 2 + [_const_spec((LANES, W_GROUP))] * 2
    args = [u] * 10 + [cwq, cwk, cwv, alog, dtb, eb, eg]
    if reverse:
        out_dtype = F32
    else:
        specs += [pl.BlockSpec((ts, W_GROUP), lambda b, i: (b * ns + i, CB_ZB)),
                  pl.BlockSpec((ts, W_GROUP), lambda b, i: (b * ns + i, 0)),
                  _const_spec((1, W_GROUP))]
        args += [u, ob, gn]
        out_dtype = BF16
    return pl.pallas_call(
        functools.partial(_gdn_kernel, reverse),
        grid=(bsz, ns),
        in_specs=specs,
        out_specs=pl.BlockSpec((ts, W_GROUP), lambda b, i: (b * ns + tbf(i), 0)),
        out_shape=jax.ShapeDtypeStruct((t, W_GROUP), out_dtype),
        scratch_shapes=[pltpu.VMEM((W_GROUP, W_GROUP), F32)],
        compiler_params=_params(("parallel", "arbitrary")),
        name="gdn_bwd" if reverse else "gdn_fwd",
    )(*args)


RET_CHUNK = 128


def _ret_kernel(reverse, *refs):
    if reverse:
        (q_ref, k_ref, v_ref, cos_ref, sin_ref, lgq_ref, lgv_ref, o_ref, s_ref) = refs
    else:
        (q_ref, k_ref, v_ref, cos_ref, sin_ref, lgq_ref, lgv_ref, lgfc_ref, lgbc_ref, g_ref, ob_ref, rn_ref,
         o_ref, s_ref) = refs
    cr = q_ref.shape[0]

    @pl.when(pl.program_id(1) == 0)
    def _():
        s_ref[...] = jnp.zeros_like(s_ref)

    cos = cos_ref[...]
    sin = sin_ref[...]

    def rope(t):
        t1, t2 = t[:, :LANES], t[:, LANES:]
        return jnp.concatenate([t1 * cos - t2 * sin, t1 * sin + t2 * cos], axis=-1)

    q = rope(q_ref[...])
    k = rope(k_ref[...]) * (HEAD_DIM ** -0.5)
    v = v_ref[...]
    lgq = lgq_ref[...]
    lgv = lgv_ref[...]
    pos = _iota((cr, W_GROUP), 0).astype(F32)
    rows_qk = (_iota((W_GROUP, W_GROUP), 0) % LANES) // 32
    blk_state = rows_qk == _iota((W_GROUP, W_GROUP), 1) // HEAD_DIM
    s = s_ref[...]
    if reverse:
        q_dec = q * jnp.exp((cr - pos) * lgq)
        k_dec = k * jnp.exp(pos * lgq)
    else:
        q_dec = q * jnp.exp((pos + 1.0) * lgq)
        k_dec = k * jnp.exp((cr - 1.0 - pos) * lgq)
    cross = _dot(q_dec.astype(BF16), s.astype(BF16))
    s_ref[...] = s * jnp.exp(cr * lgv) + jnp.where(blk_state, _dot_tn(k_dec.astype(BF16), v.astype(BF16)), 0.0)
    if reverse:
        o_ref[...] = cross
        return
    nk = N_HEADS * cr
    stack_head = _iota((nk, W_GROUP), 0) // cr
    kst = _stack4(k.astype(BF16), stack_head == (_iota((nk, W_GROUP), 1) % LANES) // 32)
    vst = _stack4(v.astype(BF16), stack_head == _iota((nk, W_GROUP), 1) // HEAD_DIM)
    qk = _dot_nt(q.astype(BF16), kst)
    ii = _iota((cr, nk), 0)
    jj = _iota((cr, nk), 1) % cr
    dist = (ii - jj).astype(F32)
    lower, upper = jj <= ii, jj >= ii
    dm = (jnp.where(lower, jnp.exp(jnp.where(lower, dist * lgfc_ref[...], 0.0)), 0.0)
          + jnp.where(upper, jnp.exp(jnp.where(upper, -dist * lgbc_ref[...], 0.0)), 0.0))
    o = _dot((qk * dm).astype(BF16), vst) + cross + ob_ref[...]
    ones_h = _head_ones()
    oc = o - _dot_sel(o, ones_h, 3) * (1.0 / HEAD_DIM)
    var = _dot_sel(oc * oc, ones_h, 2) * (1.0 / HEAD_DIM)
    o_ref[...] = (oc * lax.rsqrt(var + EPS) * rn_ref[...] * _silu(g_ref[...])).astype(o_ref.dtype)


def _ret(u, bsz, seq, reverse, cos, sin, lgq, lgv, lgfc=None, lgbc=None, rn=None, ob=None):
    cr = RET_CHUNK
    ns = seq // cr
    t = bsz * seq
    tbf = (lambda i: ns - 1 - i) if reverse else (lambda i: i)

    def col(cb):
        return pl.BlockSpec((cr, W_GROUP), lambda b, i: (b * ns + tbf(i), cb))

    tab = pl.BlockSpec((cr, LANES), lambda b, i: (tbf(i), 0))
    specs = [col(CB_QC), col(CB_KC), col(CB_VC), tab, tab, _const_spec((1, W_GROUP)), _const_spec((1, W_GROUP))]
    args = [u, u, u, cos, sin, lgq, lgv]
    if reverse:
        out_dtype = F32
    else:
        specs += [_const_spec((1, N_HEADS * cr)), _const_spec((1, N_HEADS * cr)), col(CB_GC),
                  pl.BlockSpec((cr, W_GROUP), lambda b, i: (b * ns + i, 0)), _const_spec((1, W_GROUP))]
        args += [lgfc, lgbc, u, ob, rn]
        out_dtype = BF16
    return pl.pallas_call(
        functools.partial(_ret_kernel, reverse),
        grid=(bsz, ns),
        in_specs=specs,
        out_specs=pl.BlockSpec((cr, W_GROUP), lambda b, i: (b * ns + tbf(i), 0)),
        out_shape=jax.ShapeDtypeStruct((t, W_GROUP), out_dtype),
        scratch_shapes=[pltpu.VMEM((W_GROUP, W_GROUP), F32)],
        compiler_params=_params(("parallel", "arbitrary")),
        name="ret_bwd" if reverse else "ret_fwd",
    )(*args)


def _na_prep_kernel(q_ref, k_ref, v_ref, gq_ref, gk_ref, qo_ref, ko_ref, vo_ref):
    ones_h = _head_ones()
    inv = 1.0 / HEAD_DIM
    q = q_ref[...]
    k = k_ref[...]
    qo_ref[...] = (q * lax.rsqrt(_dot_sel(q * q, ones_h, 2) * inv + EPS) * gq_ref[...] * (HEAD_DIM ** -0.5)).astype(BF16)
    ko_ref[...] = (k * lax.rsqrt(_dot_sel(k * k, ones_h, 2) * inv + EPS) * gk_ref[...]).astype(BF16)
    vo_ref[...] = v_ref[...].astype(BF16)


def _na_prep(u, gq, gk):
    t = u.shape[0]
    tm = 512

    def col(cb):
        return pl.BlockSpec((tm, W_GROUP), lambda i: (i, cb))

    out = pl.BlockSpec((tm, W_GROUP), lambda i: (i, 0))
    shp = jax.ShapeDtypeStruct((t, W_GROUP), BF16)
    return pl.pallas_call(
        _na_prep_kernel,
        grid=(t // tm,),
        in_specs=[col(CB_QD), col(CB_KD), col(CB_VD), _const_spec((1, W_GROUP)), _const_spec((1, W_GROUP))],
        out_specs=[out, out, out],
        out_shape=[shp, shp, shp],
        compiler_params=_params(("parallel",)),
        name="na_prep",
    )(u, u, u, gq, gk)


NA_GROUP = 8
NA_WIN = NA_KH * GRID_W


def _na_kernel(rows, q_ref, kp_ref, kc_ref, kn_ref, vp_ref, vc_ref, vn_ref, tab_ref, o_ref, kw, vw):
    g = pl.program_id(1)
    blk = NA_GROUP * GRID_W
    kw[0:blk, :] = kp_ref[...]
    kw[blk:2 * blk, :] = kc_ref[...]
    kw[2 * blk:3 * blk, :] = kn_ref[...]
    vw[0:blk, :] = vp_ref[...]
    vw[blk:2 * blk, :] = vc_ref[...]
    vw[2 * blk:3 * blk, :] = vn_ref[...]
    for j in range(NA_GROUP):
        r = g * NA_GROUP + j
        r0 = jnp.clip(r - NA_KH // 2, 0, rows - NA_KH)
        off = pl.multiple_of((r0 - g * NA_GROUP + NA_GROUP) * GRID_W, GRID_W)
        didx = r - r0
        qr = q_ref[j * GRID_W:(j + 1) * GRID_W, :]
        kwin = kw[pl.ds(off, NA_WIN), :]
        vwin = vw[pl.ds(off, NA_WIN), :]
        heads = []
        for h in range(N_HEADS):
            hs = slice(h * HEAD_DIM, (h + 1) * HEAD_DIM)
            sc = _dot_nt(qr[:, hs], kwin[:, hs]) + tab_ref[didx, h]
            m = jnp.max(sc, axis=-1, keepdims=True)
            p = jnp.exp(sc - m)
            l = jnp.sum(p, axis=-1, keepdims=True)
            heads.append(_dot(p.astype(BF16), vwin[:, hs]) / l)
        o_ref[j * GRID_W:(j + 1) * GRID_W, :] = jnp.concatenate(heads, axis=-1).astype(o_ref.dtype)


def _na(qn, kn, vn, table, bsz, seq):
    rows = seq // GRID_W
    ng = rows // NA_GROUP
    blk = NA_GROUP * GRID_W
    t = bsz * seq
    cur = pl.BlockSpec((blk, W_GROUP), lambda b, g: (b * ng + g, 0))
    prev = pl.BlockSpec((blk, W_GROUP), lambda b, g: (b * ng + jnp.maximum(g - 1, 0), 0))
    nxt = pl.BlockSpec((blk, W_GROUP), lambda b, g: (b * ng + jnp.minimum(g + 1, ng - 1), 0))
    return pl.pallas_call(
        functools.partial(_na_kernel, rows),
        grid=(bsz, ng),
        in_specs=[cur, prev, cur, nxt, prev, cur, nxt, _const_spec((NA_KH, N_HEADS, GRID_W, NA_WIN))],
        out_specs=cur,
        out_shape=jax.ShapeDtypeStruct((t, W_GROUP), BF16),
        scratch_shapes=[pltpu.VMEM((3 * blk, W_GROUP), BF16), pltpu.VMEM((3 * blk, W_GROUP), BF16)],
        compiler_params=_params(("parallel", "parallel")),
        name="na",
    )(qn, kn, kn, kn, vn, vn, vn, table)


def _na_table(rpb):
    didx = np.arange(NA_KH)[:, None, None, None]
    kk = np.arange(NA_KH)[None, :, None, None]
    cc = np.arange(GRID_W)[None, None, :, None]
    kc = np.arange(GRID_W)[None, None, None, :]
    c0 = np.clip(cc - NA_KW // 2, 0, GRID_W - NA_KW)
    valid = np.broadcast_to((kc >= c0) & (kc < c0 + NA_KW), (NA_KH, NA_KH, GRID_W, GRID_W))
    dr = np.broadcast_to(kk + (NA_KH - 1) - didx, valid.shape)
    dc = np.broadcast_to(np.clip(kc - cc + (NA_KW - 1), 0, 2 * NA_KW - 2), valid.shape)
    bias = rpb[:, dr, dc]
    tab = jnp.where(valid[None], bias, NEG_BIG)
    return jnp.transpose(tab, (1, 0, 3, 2, 4)).reshape(NA_KH, N_HEADS, GRID_W, NA_WIN)


def _outproj_kernel(x_ref, a_ref, b_ref, c_ref, d_ref, w_ref, o_ref):
    acc = x_ref[...]
    for gi, m_ref in enumerate((a_ref, b_ref, c_ref, d_ref)):
        acc = acc + _dot(m_ref[...], w_ref[gi * W_GROUP:(gi + 1) * W_GROUP, :])
    o_ref[...] = acc


def _outproj(x, oa, ob, oc, od, w):
    t = x.shape[0]
    tm = 512
    xs = pl.BlockSpec((tm, D_MODEL), lambda i: (i, 0))
    ms = pl.BlockSpec((tm, W_GROUP), lambda i: (i, 0))
    return pl.pallas_call(
        _outproj_kernel,
        grid=(t // tm,),
        in_specs=[xs, ms, ms, ms, ms, _const_spec((D_MODEL, D_MODEL))],
        out_specs=xs,
        out_shape=jax.ShapeDtypeStruct((t, D_MODEL), F32),
        compiler_params=_params(("parallel",)),
        name="outproj",
    )(x, oa, ob, oc, od, w)


FF_CHUNK = 256


def _ffn_kernel(xm_ref, xp_ref, xn_ref, p_ref, n2_ref, wg_ref, wu_ref, cw_ref, cb_ref, wd_ref,
                n3_ref, pg_ref, pp_ref, o_ref):
    tm = xm_ref.shape[0]
    i = pl.program_id(1)
    first = i == 0
    last = i == pl.num_programs(1) - 1
    x = xm_ref[...]
    n2 = n2_ref[...]
    xp = jnp.where(first, 0.0, xp_ref[...])
    xn = jnp.where(last, 0.0, xn_ref[...])
    h2f = _rms(x, n2)
    h2 = h2f.astype(BF16)
    h2e = jnp.concatenate([_rms(xp, n2), h2f, _rms(xn, n2)], axis=0).astype(BF16)
    acc = x
    for f in range(D_FF // FF_CHUNK):
        fs = slice(f * FF_CHUNK, (f + 1) * FF_CHUNK)
        ge = _dot(h2e, wg_ref[:, fs])
        gate = (cw_ref[0:1, fs] * ge[7:7 + tm] + cw_ref[1:2, fs] * ge[8:8 + tm]
                + cw_ref[2:3, fs] * ge[9:9 + tm] + cb_ref[:, fs])
        up = _dot(h2, wu_ref[:, fs])
        acc = acc + _dot((_gelu_tanh(gate) * up).astype(BF16), wd_ref[fs, :])
    h3 = _rms(acc, n3_ref[...]).astype(BF16)
    gate3 = _sigmoid(_dot(h3, pg_ref[...]))
    o_ref[...] = acc + gate3 * _dot(p_ref[...].astype(BF16), pp_ref[...])


def _ffn(x, p, bsz, seq, n2, wg, wu, cw, cb, wd, n3, pg, pp):
    tm = 256
    ns = seq // tm
    t = bsz * seq
    xm, xp, xn = _row_specs(tm, ns, t, 0, False, width=D_MODEL)
    return pl.pallas_call(
        _ffn_kernel,
        grid=(bsz, ns),
        in_specs=[xm, xp, xn, pl.BlockSpec((tm, PLE_DIM), lambda b, i: (b * ns + i, 0)),
                  _const_spec((1, D_MODEL)), _const_spec((D_MODEL, D_FF)), _const_spec((D_MODEL, D_FF)),
                  _const_spec((3, D_FF)), _const_spec((1, D_FF)), _const_spec((D_FF, D_MODEL)),
                  _const_spec((1, D_MODEL)), _const_spec((D_MODEL, D_MODEL)), _const_spec((PLE_DIM, D_MODEL))],
        out_specs=pl.BlockSpec((tm, D_MODEL), lambda b, i: (b * ns + i, 0)),
        out_shape=jax.ShapeDtypeStruct((t, D_MODEL), F32),
        compiler_params=_params(("parallel", "parallel")),
        name="ffn_ple",
    )(x, x, x, p, n2, wg, wu, cw, cb, wd, n3, pg, pp)


def _block_diag(w):
    h, n, _ = w.shape
    eye = jnp.eye(h, dtype=w.dtype)
    return (eye[:, None, :, None] * w[:, :, None, :]).reshape(h * n, h * n)


def _row(v):
    return v.reshape(1, -1).astype(F32)


def _lane_pad(v8, offset):
    return jnp.zeros((1, LANES), F32).at[0, offset:offset + 2 * N_HEADS].set(v8.reshape(-1).astype(F32))


def _expander(offset):
    e = np.zeros((LANES, W_GROUP), np.float32)
    for h in range(N_HEADS):
        e[offset + h, h * HEAD_DIM:(h + 1) * HEAD_DIM] = 1.0
    return jnp.asarray(e, BF16)


def _layer(x, p, lw, bsz, seq):
    w_in = _reorder_in_cols(lw['w_in'].astype(BF16))
    u = _inproj(x, _row(lw['norm1']), w_in)

    cw_a, cb_a = lw['conv_a_w'].astype(F32), _row(lw['conv_a_b'])
    h_dir = None
    for d in (1, 0):
        h_dir = _rglru(u, bsz, seq, d == 1, cw_a, cb_a,
                       _block_diag(lw['lru_wr'][d]).astype(BF16), _row(lw['lru_br'][d]),
                       _block_diag(lw['lru_wi'][d]).astype(BF16), _row(lw['lru_bi'][d]),
                       _row(lw['lru_lambda'][d]), hb=h_dir)
    out_a = h_dir

    gconv = lw['gdn_conv'].astype(F32)
    cwq, cwk, cwv = gconv[:, :W_GROUP], gconv[:, W_GROUP:2 * W_GROUP], gconv[:, 2 * W_GROUP:]
    alog = _lane_pad(lw['gdn_a_log'], 2 * N_HEADS)
    dtb = _lane_pad(lw['gdn_dt_bias'], 2 * N_HEADS)
    gn = _row(jnp.tile(lw['gdn_norm'], N_HEADS))
    o_dir = None
    for d in (1, 0):
        o_dir = _gdn(u, bsz, seq, d == 1, cwq, cwk, cwv, alog, dtb,
                     _expander(d * N_HEADS), _expander(2 * N_HEADS + d * N_HEADS), gn=gn, ob=o_dir)
    out_b = o_dir

    half = HEAD_DIM // 2
    inv_freq = ROPE_BASE ** (-jnp.arange(half, dtype=F32) / half)
    ang = jnp.arange(seq, dtype=F32)[:, None] * inv_freq[None, :]
    cos = jnp.tile(jnp.cos(ang), (1, N_HEADS))
    sin = jnp.tile(jnp.sin(ang), (1, N_HEADS))
    log_gamma = jnp.log1p(-jnp.exp2(-lw['ret_decay'].astype(F32)))
    rn = _row(jnp.tile(lw['ret_norm'], N_HEADS))
    o_dir = None
    for d in (1, 0):
        lg = log_gamma[d]
        lgq = _row(jnp.tile(jnp.repeat(lg, half), 2))
        lgv = _row(jnp.repeat(lg, HEAD_DIM))
        if d == 1:
            o_dir = _ret(u, bsz, seq, True, cos, sin, lgq, lgv)
        else:
            o_dir = _ret(u, bsz, seq, False, cos, sin, lgq, lgv,
                         lgfc=_row(jnp.repeat(log_gamma[0], RET_CHUNK)), lgbc=_row(jnp.repeat(log_gamma[1], RET_CHUNK)),
                         rn=rn, ob=o_dir)
    out_c = o_dir

    qn, kn, vn = _na_prep(u, _row(jnp.tile(lw['na_qnorm'], N_HEADS)), _row(jnp.tile(lw['na_knorm'], N_HEADS)))
    out_d = _na(qn, kn, vn, _na_table(lw['na_rpb'].astype(F32)), bsz, seq)

    x = _outproj(x, out_a, out_b, out_c, out_d, lw['w_out'].astype(BF16))
    return _ffn(x, p, bsz, seq, _row(lw['norm2']), lw['ffn_wg'].astype(BF16), lw['ffn_wu'].astype(BF16),
                lw['ffn_conv_w'].astype(F32), _row(lw['ffn_conv_b']), lw['ffn_wd'].astype(BF16),
                _row(lw['norm3']), lw['ple_gate'].astype(BF16), lw['ple_proj'].astype(BF16))


_WEIGHT_NAMES = ('norm1', 'norm2', 'norm3', 'w_in', 'w_out', 'conv_a_w', 'conv_a_b', 'lru_wr', 'lru_br',
                 'lru_wi', 'lru_bi', 'lru_lambda', 'gdn_conv', 'gdn_a_log', 'gdn_dt_bias', 'gdn_norm',
                 'ret_decay', 'ret_norm', 'na_qnorm', 'na_knorm', 'na_rpb', 'ffn_wg', 'ffn_wu',
                 'ffn_conv_w', 'ffn_conv_b', 'ffn_wd', 'ple_proj', 'ple_gate')


def _trunk(x, p, weights):
    bsz, seq, _ = x.shape
    xf = x.reshape(bsz * seq, D_MODEL)
    for i in range(DEPTH):
        lw = {n: w[i] for n, w in weights.items()}
        xf = _layer(xf, p[i].reshape(bsz * seq, PLE_DIM), lw, bsz, seq)
    return xf.reshape(bsz, seq, D_MODEL)


def kernel(x_prompt, x_sample, p_prompt, p_sample, norm1, norm2, norm3, w_in, w_out, conv_a_w, conv_a_b, lru_wr, lru_br, lru_wi, lru_bi, lru_lambda, gdn_conv, gdn_a_log, gdn_dt_bias, gdn_norm, ret_decay, ret_norm, na_qnorm, na_knorm, na_rpb, ffn_wg, ffn_wu, ffn_conv_w, ffn_conv_b, ffn_wd, ple_proj, ple_gate):
    weights = dict(zip(_WEIGHT_NAMES, (norm1, norm2, norm3, w_in, w_out, conv_a_w, conv_a_b, lru_wr, lru_br,
                                       lru_wi, lru_bi, lru_lambda, gdn_conv, gdn_a_log, gdn_dt_bias, gdn_norm,
                                       ret_decay, ret_norm, na_qnorm, na_knorm, na_rpb, ffn_wg, ffn_wu,
                                       ffn_conv_w, ffn_conv_b, ffn_wd, ple_proj, ple_gate)))
    return (_trunk(x_prompt, p_prompt, weights), _trunk(x_sample, p_sample, weights))
```

```python
import functools

import jax
import jax.numpy as jnp
import numpy as np
from jax import lax
from jax.experimental import pallas as pl
from jax.experimental.pallas import tpu as pltpu

F32 = jnp.float32
BF16 = jnp.bfloat16

D_MODEL = 1024
DEPTH = 2
PLE_DIM = 256
GRID_W = 64
W_GROUP = 256
HEAD_DIM = 64
N_HEADS = 4
LRU_C = 8.0
CHUNK = 64
ROPE_BASE = 10000.0
NA_KH = 8
NA_KW = 16
D_FF = 2816
EPS = 1e-6
NEG_BIG = -1e30

SUBLANES = 8
LANES = 128
VMEM_LIMIT = 56 * 1024 * 1024

CB_XA, CB_YA, CB_QB, CB_KB, CB_VB, CB_ZB, CB_QC, CB_KC, CB_VC, CB_GC, CB_QD, CB_KD, CB_VD = range(13)
CB128_AB = 26
IN_COLS_PAD = 13 * W_GROUP + LANES
IN_COLS = 13 * W_GROUP + 4 * N_HEADS


def _reorder_in_cols(w):
    rows = w.shape[0]
    half = HEAD_DIM // 2

    def halves_first(cols):
        return cols.reshape(rows, N_HEADS, 2, half).transpose(0, 2, 1, 3).reshape(rows, W_GROUP)

    ab0 = 6 * W_GROUP
    qc0 = ab0 + 4 * N_HEADS
    parts = [w[:, :ab0], halves_first(w[:, qc0:qc0 + W_GROUP]), halves_first(w[:, qc0 + W_GROUP:qc0 + 2 * W_GROUP]),
             w[:, qc0 + 2 * W_GROUP:], w[:, ab0:qc0], jnp.zeros((rows, LANES - 4 * N_HEADS), w.dtype)]
    out = jnp.concatenate(parts, axis=1)
    assert out.shape[1] == IN_COLS_PAD
    return out


def _sigmoid(x):
    return 1.0 / (1.0 + jnp.exp(-x))


def _silu(x):
    return x * _sigmoid(x)


def _softplus(x):
    return jnp.maximum(x, 0.0) + jnp.log1p(jnp.exp(-jnp.abs(x)))


def _gelu_tanh(x):
    return x * (0.5 * (1.0 + jnp.tanh(0.7978845608028654 * (x + 0.044715 * (x * x * x)))))


def _rms(x, gain):
    ms = jnp.mean(x * x, axis=-1, keepdims=True)
    return x * lax.rsqrt(ms + EPS) * gain


def _dot(a, b):
    return jnp.dot(a, b, preferred_element_type=F32)


def _split_bf16(x, terms):
    parts = []
    for _ in range(terms):
        p = x.astype(BF16)
        parts.append(p)
        x = x - p.astype(F32)
    return parts


def _dot_sel(x, sel, terms):
    return sum(_dot(p, sel) for p in _split_bf16(x, terms))


def _sel_dot(sel, x, terms):
    return sum(_dot(sel, p) for p in _split_bf16(x, terms))


def _dot_nt(a, b):
    return lax.dot_general(a, b, (((1,), (1,)), ((), ())), preferred_element_type=F32)


def _dot_tn(a, b):
    return lax.dot_general(a, b, (((0,), (0,)), ((), ())), preferred_element_type=F32)


def _iota(shape, dim):
    return lax.broadcasted_iota(jnp.int32, shape, dim)


def _head_ones():
    return (_iota((W_GROUP, W_GROUP), 0) // HEAD_DIM == _iota((W_GROUP, W_GROUP), 1) // HEAD_DIM).astype(BF16)


def _stack4(x, mask):
    return jnp.where(mask, jnp.concatenate([x] * N_HEADS, axis=0), jnp.zeros((), x.dtype))


def _conv4(xm, xp, xn, w_ref, first, last):
    ts = xm.shape[0]
    xp = jnp.where(first, 0.0, xp)
    xn = jnp.where(last, 0.0, xn)
    xe = jnp.concatenate([xp, xm, xn], axis=0)
    out = w_ref[0:1, :] * xe[6:6 + ts]
    out = out + w_ref[1:2, :] * xe[7:7 + ts]
    out = out + w_ref[2:3, :] * xe[8:8 + ts]
    out = out + w_ref[3:4, :] * xe[9:9 + ts]
    return out


def _time_block(reverse):
    i = pl.program_id(1)
    n = pl.num_programs(1)
    tb = (n - 1 - i) if reverse else i
    return tb, tb == 0, tb == n - 1


def _row_specs(ts, ns, total_rows, col_block, reverse, width=W_GROUP):
    r8 = ts // SUBLANES
    last8 = total_rows // SUBLANES - 1

    def tb(i):
        return (ns - 1 - i) if reverse else i

    main = pl.BlockSpec((ts, width), lambda b, i: (b * ns + tb(i), col_block))
    prev = pl.BlockSpec((SUBLANES, width), lambda b, i: (jnp.maximum((b * ns + tb(i)) * r8 - 1, 0), col_block))
    nxt = pl.BlockSpec((SUBLANES, width), lambda b, i: (jnp.minimum((b * ns + tb(i) + 1) * r8, last8), col_block))
    return main, prev, nxt


def _const_spec(shape):
    nd = len(shape)
    return pl.BlockSpec(shape, lambda *_: (0,) * nd, pipeline_mode=pl.Buffered(1))


def _params(sem):
    return pltpu.CompilerParams(dimension_semantics=sem, vmem_limit_bytes=VMEM_LIMIT)


def _inproj_kernel(x_ref, g_ref, w_ref, o_ref):
    h = _rms(x_ref[...], g_ref[...]).astype(BF16)
    o_ref[...] = _dot(h, w_ref[...])


def _inproj(x, gain, w):
    t = x.shape[0]
    tm = 256
    return pl.pallas_call(
        _inproj_kernel,
        grid=(t // tm,),
        in_specs=[pl.BlockSpec((tm, D_MODEL), lambda i: (i, 0)),
                  _const_spec((1, D_MODEL)),
                  _const_spec((D_MODEL, IN_COLS_PAD))],
        out_specs=pl.BlockSpec((tm, IN_COLS_PAD), lambda i: (i, 0)),
        out_shape=jax.ShapeDtypeStruct((t, IN_COLS_PAD), F32),
        compiler_params=_params(("parallel",)),
        name="inproj",
    )(x, gain, w)


def _rglru_kernel(reverse, *refs):
    if reverse:
        (xm_ref, xp_ref, xn_ref, cw_ref, cb_ref, wr_ref, br_ref, wi_ref, bi_ref, lam_ref,
         o_ref, a_s, b_s, p_s, ac_s, c_s, carry, h_s) = refs
    else:
        (xm_ref, xp_ref, xn_ref, cw_ref, cb_ref, wr_ref, br_ref, wi_ref, bi_ref, lam_ref, ya_ref, hb_ref,
         o_ref, a_s, b_s, p_s, ac_s, c_s, carry, h_s) = refs
    ts = xm_ref.shape[0]
    ng = ts // SUBLANES
    tb, first, last = _time_block(reverse)

    @pl.when(pl.program_id(1) == 0)
    def _():
        carry[...] = jnp.zeros_like(carry)

    x = _conv4(xm_ref[...], xp_ref[...], xn_ref[...], cw_ref, first, last) + cb_ref[...]
    xb = x.astype(BF16)
    r = _sigmoid(_dot(xb, wr_ref[...]) + br_ref[...])
    ig = _sigmoid(_dot(xb, wi_ref[...]) + bi_ref[...])
    log_a = (-LRU_C) * r * _softplus(-lam_ref[...])
    a = jnp.exp(log_a)
    th = jnp.tanh(-log_a)
    bx = jnp.sqrt(2.0 * th / (1.0 + th)) * (ig * x)
    nl = W_GROUP // LANES
    for hf in range(nl):
        a_s[hf] = a[:, hf * LANES:(hf + 1) * LANES]
        b_s[hf] = bx[:, hf * LANES:(hf + 1) * LANES]

    def slab(ref, k):
        return jnp.concatenate([ref[hf, pl.ds(k, ng, stride=SUBLANES), :] for hf in range(nl)], axis=-1)

    order = list(range(SUBLANES))[::-1] if reverse else list(range(SUBLANES))
    p = None
    ac = None
    for k in order:
        a_k = slab(a_s, k)
        b_k = slab(b_s, k)
        if p is None:
            p, ac = b_k, a_k
        else:
            p = a_k * p + b_k
            ac = a_k * ac
        p_s[k] = p
        ac_s[k] = ac

    k_last = order[-1]

    def body(g, c):
        gi = (ng - 1 - g) if reverse else g
        c_s[pl.ds(gi, 1), :] = c
        return ac_s[k_last, pl.ds(gi, 1), :] * c + p_s[k_last, pl.ds(gi, 1), :]

    carry[...] = lax.fori_loop(0, ng, body, carry[...])
    c_in = c_s[...]
    for k in order:
        h_k = p_s[k] + ac_s[k] * c_in
        for hf in range(nl):
            h_s[hf, pl.ds(k, ng, stride=SUBLANES), :] = h_k[:, hf * LANES:(hf + 1) * LANES]
    h = jnp.concatenate([h_s[hf] for hf in range(nl)], axis=-1)
    if reverse:
        o_ref[...] = h
    else:
        o_ref[...] = ((h + hb_ref[...]) * _gelu_tanh(ya_ref[...])).astype(o_ref.dtype)


def _rglru(u, bsz, seq, reverse, cw, cb, wr, br, wi, bi, lam, hb=None):
    ts = 512
    ns = seq // ts
    t = bsz * seq
    ng = ts // SUBLANES
    xm, xp, xn = _row_specs(ts, ns, t, CB_XA, reverse)
    in_specs = [xm, xp, xn, _const_spec((4, W_GROUP)), _const_spec((1, W_GROUP)),
                _const_spec((W_GROUP, W_GROUP)), _const_spec((1, W_GROUP)),
                _const_spec((W_GROUP, W_GROUP)), _const_spec((1, W_GROUP)), _const_spec((1, W_GROUP))]
    args = [u, u, u, cw, cb, wr, br, wi, bi, lam]
    halves = (W_GROUP // LANES, ts, LANES)
    scratch = [pltpu.VMEM(halves, F32), pltpu.VMEM(halves, F32),
               pltpu.VMEM((SUBLANES, ng, W_GROUP), F32), pltpu.VMEM((SUBLANES, ng, W_GROUP), F32),
               pltpu.VMEM((ng, W_GROUP), F32), pltpu.VMEM((1, W_GROUP), F32), pltpu.VMEM(halves, F32)]
    if reverse:
        out_dtype = F32
        out_spec = pl.BlockSpec((ts, W_GROUP), lambda b, i: (b * ns + ns - 1 - i, 0))
    else:
        ya = _row_specs(ts, ns, t, CB_YA, False)[0]
        in_specs += [ya, pl.BlockSpec((ts, W_GROUP), lambda b, i: (b * ns + i, 0))]
        args += [u, hb]
        out_dtype = BF16
        out_spec = pl.BlockSpec((ts, W_GROUP), lambda b, i: (b * ns + i, 0))
    return pl.pallas_call(
        functools.partial(_rglru_kernel, reverse),
        grid=(bsz, ns),
        in_specs=in_specs,
        out_specs=out_spec,
        out_shape=jax.ShapeDtypeStruct((t, W_GROUP), out_dtype),
        scratch_shapes=scratch,
        compiler_params=_params(("parallel", "arbitrary")),
        name="rglru_bwd" if reverse else "rglru_fwd",
    )(*args)


GDN_BLOCK = 256


def _gdn_prep_kernel(qm, qp, qn, km, kp, kn, vm, vp, vn, ab_ref, cwq, cwk, cwv, alog_ref, dtb_ref, eb_ref, eg_ref,
                     *outs):
    ts = qm.shape[0]
    nc = ts // CHUNK
    c = CHUNK
    i = pl.program_id(1)
    first = i == 0
    last = i == pl.num_programs(1) - 1
    out_refs = (outs[0:5], outs[5:10])

    ones_h = _head_ones()
    q = _silu(_conv4(qm[...], qp[...], qn[...], cwq, first, last))
    k = _silu(_conv4(km[...], kp[...], kn[...], cwk, first, last))
    v = _silu(_conv4(vm[...], vp[...], vn[...], cwv, first, last))
    q = q * lax.rsqrt(_dot_sel(q * q, ones_h, 2) + EPS) * (HEAD_DIM ** -0.5)
    k = k * lax.rsqrt(_dot_sel(k * k, ones_h, 2) + EPS)
    ab = ab_ref[...]
    sig_parts = _split_bf16(_sigmoid(ab), 2)
    g_parts = _split_bf16(-jnp.exp(alog_ref[...]) * _softplus(ab + dtb_ref[...]), 3)
    beta = [sum(_dot(p, eb_ref[d]) for p in sig_parts) for d in range(2)]
    gate = [sum(_dot(p, eg_ref[d]) for p in g_parts) for d in range(2)]

    ii = _iota((c, W_GROUP), 0)
    jj = _iota((c, W_GROUP), 1) % c
    r2 = _iota((c, c), 0)
    c2 = _iota((c, c), 1)
    incl = (jj <= ii, jj >= ii)
    strict = (jj < ii, jj > ii)
    tri = ((c2 <= r2).astype(BF16), (c2 >= r2).astype(BF16))
    ucat = ((ii > jj).astype(F32), (ii < jj).astype(F32))
    eye_cat = (ii == jj).astype(F32)
    blk = _iota((W_GROUP, W_GROUP), 0) // HEAD_DIM == _iota((W_GROUP, W_GROUP), 1) // HEAD_DIM

    sls = [slice(ci * c, (ci + 1) * c) for ci in range(nc)]
    kq = [_dot_nt(jnp.concatenate([k[sl], q[sl]], axis=0).astype(BF16), _stack4(k[sl].astype(BF16), blk))
          for sl in sls]
    units = [(d, ci) for ci in range(nc) for d in range(2)]
    gd = [_sel_dot(tri[d], jnp.concatenate([gate[d][sls[ci]], gate[d][sls[ci]] * ucat[d]], axis=1), 3)
          for d, ci in units]
    gcum = [x[:, :W_GROUP] for x in gd]
    decay = [jnp.where(incl[d], jnp.exp(jnp.where(incl[d], x[:, W_GROUP:], 0.0)), 0.0)
             for (d, ci), x in zip(units, gd)]
    lmat = [jnp.where(strict[d], beta[d][sls[ci]] * kq[ci][:c] * dec, 0.0) for (d, ci), dec in zip(units, decay)]
    attn = [kq[ci][c:] * dec for (d, ci), dec in zip(units, decay)]
    pinv = [eye_cat - l for l in lmat]
    lb = [l.astype(BF16) for l in lmat]
    lpow = [_dot(x, _stack4(x, blk)) for x in lb]
    for lvl in range(5):
        z = [_stack4(x.astype(BF16), blk) for x in lpow]
        if lvl < 4:
            r = [_dot(jnp.concatenate([p, lp], axis=0).astype(BF16), zz) for p, lp, zz in zip(pinv, lpow, z)]
            pinv = [p + x[:c] for p, x in zip(pinv, r)]
            lpow = [x[c:] for x in r]
        else:
            pinv = [p + _dot(p.astype(BF16), zz) for p, zz in zip(pinv, z)]
    egc = [jnp.exp(x) for x in gcum]
    pb = [p.astype(BF16) for p in pinv]
    for n, (d, ci) in enumerate(units):
        sl = sls[ci]
        u_ref, wq_ref, at_ref, kg_ref, ds_ref = out_refs[d]
        bc = beta[d][sl]
        u_ref[sl, :] = _dot(pb[n], _stack4((v[sl] * bc).astype(BF16), blk))
        w_key = _dot(pb[n], _stack4((k[sl] * bc * egc[n]).astype(BF16), blk))
        g_last = gcum[n][0:1, :] if d else gcum[n][c - 1:c, :]
        wq_ref[2 * ci * c:(2 * ci + 1) * c, :] = w_key.astype(BF16)
        wq_ref[(2 * ci + 1) * c:(2 * ci + 2) * c, :] = (q[sl] * egc[n]).astype(BF16)
        at_ref[sl, :] = attn[n].astype(BF16)
        kg_ref[sl, :] = (k[sl] * jnp.exp(g_last - gcum[n])).astype(BF16)
        ds_ref[ci] = jnp.exp(g_last)


def _gdn_prep(u, bsz, seq, cwq, cwk, cwv, alog, dtb, eb, eg):
    ts = GDN_BLOCK
    ns = seq // ts
    t = bsz * seq
    nc = ts // CHUNK
    specs = []
    for cb in (CB_QB, CB_KB, CB_VB):
        specs += list(_row_specs(ts, ns, t, cb, False))
    specs.append(pl.BlockSpec((ts, LANES), lambda b, i: (b * ns + i, CB128_AB)))
    specs += [_const_spec((4, W_GROUP))] * 3
    specs += [_const_spec((1, LANES))] * 2 + [_const_spec((2, LANES, W_GROUP))] * 2

    def row(b, i):
        return (b * ns + i, 0)

    one_dir_specs = [pl.BlockSpec((ts, W_GROUP), row), pl.BlockSpec((2 * ts, W_GROUP), row),
                     pl.BlockSpec((ts, W_GROUP), row), pl.BlockSpec((ts, W_GROUP), row),
                     pl.BlockSpec((nc, 1, W_GROUP), lambda b, i: (b * ns + i, 0, 0))]
    one_dir_shapes = [jax.ShapeDtypeStruct((t, W_GROUP), F32), jax.ShapeDtypeStruct((2 * t, W_GROUP), BF16),
                      jax.ShapeDtypeStruct((t, W_GROUP), BF16), jax.ShapeDtypeStruct((t, W_GROUP), BF16),
                      jax.ShapeDtypeStruct((t // CHUNK, 1, W_GROUP), F32)]
    return pl.pallas_call(
        _gdn_prep_kernel,
        grid=(bsz, ns),
        in_specs=specs,
        out_specs=one_dir_specs * 2,
        out_shape=one_dir_shapes * 2,
        compiler_params=_params(("parallel", "parallel")),
        name="gdn_prep",
    )(*([u] * 10 + [cwq, cwk, cwv, alog, dtb, eb, eg]))


def _gdn_scan_kernel(uf, wqf, atf, kgf, dsf, ub, wqb, atb, kgb, dsb, of_ref, ob_ref, s_ref):
    bsz, ts, _ = uf.shape
    nc = ts // CHUNK
    c = CHUNK

    @pl.when(pl.program_id(0) == 0)
    def _():
        s_ref[...] = jnp.zeros_like(s_ref)

    blk = _iota((W_GROUP, W_GROUP), 0) // HEAD_DIM == _iota((W_GROUP, W_GROUP), 1) // HEAD_DIM
    ins = ((uf, wqf, atf, kgf, dsf, of_ref), (ub, wqb, atb, kgb, dsb, ob_ref))
    chains = [(d, b) for b in range(bsz) for d in range(2)]
    state = [s_ref[d, b] for d, b in chains]
    for step in range(nc):
        pos = (step, nc - 1 - step)
        ws = [_dot(ins[d][1][b, 2 * pos[d] * c:(2 * pos[d] + 2) * c, :], s.astype(BF16))
              for (d, b), s in zip(chains, state)]
        vnb = [(ins[d][0][b, pos[d] * c:(pos[d] + 1) * c, :] - x[:c]).astype(BF16) for (d, b), x in zip(chains, ws)]
        for (d, b), x, vn in zip(chains, ws, vnb):
            sl = slice(pos[d] * c, (pos[d] + 1) * c)
            ins[d][5][b, sl, :] = x[c:] + _dot(ins[d][2][b, sl, :], _stack4(vn, blk))
        state = [s * ins[d][4][b, pos[d]] + jnp.where(blk, _dot_tn(ins[d][3][b, pos[d] * c:(pos[d] + 1) * c, :], vn), 0.0)
                 for (d, b), s, vn in zip(chains, state, vnb)]
    for (d, b), s in zip(chains, state):
        s_ref[d, b] = s


def _gdn_scan(prep, bsz, seq):
    ts = GDN_BLOCK
    ns = seq // ts
    nc = ts // CHUNK
    specs, args = [], []
    for d in range(2):
        tb = (lambda i: ns - 1 - i) if d else (lambda i: i)
        u_, wq, at, kg, ds = prep[5 * d:5 * d + 5]
        args += [u_.reshape(bsz, seq, W_GROUP), wq.reshape(bsz, 2 * seq, W_GROUP), at.reshape(bsz, seq, W_GROUP),
                 kg.reshape(bsz, seq, W_GROUP), ds.reshape(bsz, seq // CHUNK, 1, W_GROUP)]

        def blk3(rows, tb=tb):
            return pl.BlockSpec((bsz, rows, W_GROUP), lambda i: (0, tb(i), 0))

        specs += [blk3(ts), blk3(2 * ts), blk3(ts), blk3(ts),
                  pl.BlockSpec((bsz, nc, 1, W_GROUP), lambda i, tb=tb: (0, tb(i), 0, 0))]
    out_shape = jax.ShapeDtypeStruct((bsz, seq, W_GROUP), F32)
    o_fwd, o_bwd = pl.pallas_call(
        _gdn_scan_kernel,
        grid=(ns,),
        in_specs=specs,
        out_specs=[pl.BlockSpec((bsz, ts, W_GROUP), lambda i: (0, i, 0)),
                   pl.BlockSpec((bsz, ts, W_GROUP), lambda i: (0, ns - 1 - i, 0))],
        out_shape=[out_shape, out_shape],
        scratch_shapes=[pltpu.VMEM((2, bsz, W_GROUP, W_GROUP), F32)],
        compiler_params=_params(("arbitrary",)),
        name="gdn_scan",
    )(*args)
    return o_fwd.reshape(bsz * seq, W_GROUP), o_bwd.reshape(bsz * seq, W_GROUP)


RET_CHUNK = 128


def _ret_kernel(reverse, *refs):
    if reverse:
        (q_ref, k_ref, v_ref, cos_ref, sin_ref, lgq_ref, lgv_ref, o_ref, s_ref) = refs
    else:
        (q_ref, k_ref, v_ref, cos_ref, sin_ref, lgq_ref, lgv_ref, lgfc_ref, lgbc_ref, g_ref, ob_ref, rn_ref,
         o_ref, s_ref) = refs
    cr = q_ref.shape[0]

    @pl.when(pl.program_id(1) == 0)
    def _():
        s_ref[...] = jnp.zeros_like(s_ref)

    cos = cos_ref[...]
    sin = sin_ref[...]

    def rope(t):
        t1, t2 = t[:, :LANES], t[:, LANES:]
        return jnp.concatenate([t1 * cos - t2 * sin, t1 * sin + t2 * cos], axis=-1)

    q = rope(q_ref[...])
    k = rope(k_ref[...]) * (HEAD_DIM ** -0.5)
    v = v_ref[...]
    lgq = lgq_ref[...]
    lgv = lgv_ref[...]
    pos = _iota((cr, W_GROUP), 0).astype(F32)
    rows_qk = (_iota((W_GROUP, W_GROUP), 0) % LANES) // 32
    blk_state = rows_qk == _iota((W_GROUP, W_GROUP), 1) // HEAD_DIM
    s = s_ref[...]
    if reverse:
        q_dec = q * jnp.exp((cr - pos) * lgq)
        k_dec = k * jnp.exp(pos * lgq)
    else:
        q_dec = q * jnp.exp((pos + 1.0) * lgq)
        k_dec = k * jnp.exp((cr - 1.0 - pos) * lgq)
    cross = _dot(q_dec.astype(BF16), s.astype(BF16))
    s_ref[...] = s * jnp.exp(cr * lgv) + jnp.where(blk_state, _dot_tn(k_dec.astype(BF16), v.astype(BF16)), 0.0)
    if reverse:
        o_ref[...] = cross
        return
    nk = N_HEADS * cr
    stack_head = _iota((nk, W_GROUP), 0) // cr
    kst = _stack4(k.astype(BF16), stack_head == (_iota((nk, W_GROUP), 1) % LANES) // 32)
    vst = _stack4(v.astype(BF16), stack_head == _iota((nk, W_GROUP), 1) // HEAD_DIM)
    qk = _dot_nt(q.astype(BF16), kst)
    ii = _iota((cr, nk), 0)
    jj = _iota((cr, nk), 1) % cr
    dist = (ii - jj).astype(F32)
    lower, upper = jj <= ii, jj >= ii
    dm = (jnp.where(lower, jnp.exp(jnp.where(lower, dist * lgfc_ref[...], 0.0)), 0.0)
          + jnp.where(upper, jnp.exp(jnp.where(upper, -dist * lgbc_ref[...], 0.0)), 0.0))
    o = _dot((qk * dm).astype(BF16), vst) + cross + ob_ref[...]
    ones_h = _head_ones()
    oc = o - _dot_sel(o, ones_h, 3) * (1.0 / HEAD_DIM)
    var = _dot_sel(oc * oc, ones_h, 2) * (1.0 / HEAD_DIM)
    o_ref[...] = (oc * lax.rsqrt(var + EPS) * rn_ref[...] * _silu(g_ref[...])).astype(o_ref.dtype)


def _ret(u, bsz, seq, reverse, cos, sin, lgq, lgv, lgfc=None, lgbc=None, rn=None, ob=None):
    cr = RET_CHUNK
    ns = seq // cr
    t = bsz * seq
    tbf = (lambda i: ns - 1 - i) if reverse else (lambda i: i)

    def col(cb):
        return pl.BlockSpec((cr, W_GROUP), lambda b, i: (b * ns + tbf(i), cb))

    tab = pl.BlockSpec((cr, LANES), lambda b, i: (tbf(i), 0))
    specs = [col(CB_QC), col(CB_KC), col(CB_VC), tab, tab, _const_spec((1, W_GROUP)), _const_spec((1, W_GROUP))]
    args = [u, u, u, cos, sin, lgq, lgv]
    if reverse:
        out_dtype = F32
    else:
        specs += [_const_spec((1, N_HEADS * cr)), _const_spec((1, N_HEADS * cr)), col(CB_GC),
                  pl.BlockSpec((cr, W_GROUP), lambda b, i: (b * ns + i, 0)), _const_spec((1, W_GROUP))]
        args += [lgfc, lgbc, u, ob, rn]
        out_dtype = BF16
    return pl.pallas_call(
        functools.partial(_ret_kernel, reverse),
        grid=(bsz, ns),
        in_specs=specs,
        out_specs=pl.BlockSpec((cr, W_GROUP), lambda b, i: (b * ns + tbf(i), 0)),
        out_shape=jax.ShapeDtypeStruct((t, W_GROUP), out_dtype),
        scratch_shapes=[pltpu.VMEM((W_GROUP, W_GROUP), F32)],
        compiler_params=_params(("parallel", "arbitrary")),
        name="ret_bwd" if reverse else "ret_fwd",
    )(*args)


def _na_prep_kernel(q_ref, k_ref, v_ref, gq_ref, gk_ref, qo_ref, ko_ref, vo_ref):
    ones_h = _head_ones()
    inv = 1.0 / HEAD_DIM
    q = q_ref[...]
    k = k_ref[...]
    qo_ref[...] = (q * lax.rsqrt(_dot_sel(q * q, ones_h, 2) * inv + EPS) * gq_ref[...] * (HEAD_DIM ** -0.5)).astype(BF16)
    ko_ref[...] = (k * lax.rsqrt(_dot_sel(k * k, ones_h, 2) * inv + EPS) * gk_ref[...]).astype(BF16)
    vo_ref[...] = v_ref[...].astype(BF16)


def _na_prep(u, gq, gk):
    t = u.shape[0]
    tm = 512

    def col(cb):
        return pl.BlockSpec((tm, W_GROUP), lambda i: (i, cb))

    out = pl.BlockSpec((tm, W_GROUP), lambda i: (i, 0))
    shp = jax.ShapeDtypeStruct((t, W_GROUP), BF16)
    return pl.pallas_call(
        _na_prep_kernel,
        grid=(t // tm,),
        in_specs=[col(CB_QD), col(CB_KD), col(CB_VD), _const_spec((1, W_GROUP)), _const_spec((1, W_GROUP))],
        out_specs=[out, out, out],
        out_shape=[shp, shp, shp],
        compiler_params=_params(("parallel",)),
        name="na_prep",
    )(u, u, u, gq, gk)


NA_GROUP = 8
NA_WIN = NA_KH * GRID_W


def _na_kernel(rows, q_ref, kp_ref, kc_ref, kn_ref, vp_ref, vc_ref, vn_ref, tab_ref, o_ref, kw, vw):
    g = pl.program_id(1)
    blk = NA_GROUP * GRID_W
    kw[0:blk, :] = kp_ref[...]
    kw[blk:2 * blk, :] = kc_ref[...]
    kw[2 * blk:3 * blk, :] = kn_ref[...]
    vw[0:blk, :] = vp_ref[...]
    vw[blk:2 * blk, :] = vc_ref[...]
    vw[2 * blk:3 * blk, :] = vn_ref[...]
    for j in range(NA_GROUP):
        r = g * NA_GROUP + j
        r0 = jnp.clip(r - NA_KH // 2, 0, rows - NA_KH)
        off = pl.multiple_of((r0 - g * NA_GROUP + NA_GROUP) * GRID_W, GRID_W)
        didx = r - r0
        qr = q_ref[j * GRID_W:(j + 1) * GRID_W, :]
        kwin = kw[pl.ds(off, NA_WIN), :]
        vwin = vw[pl.ds(off, NA_WIN), :]
        heads = []
        for h in range(N_HEADS):
            hs = slice(h * HEAD_DIM, (h + 1) * HEAD_DIM)
            sc = _dot_nt(qr[:, hs], kwin[:, hs]) + tab_ref[didx, h]
            m = jnp.max(sc, axis=-1, keepdims=True)
            p = jnp.exp(sc - m)
            l = jnp.sum(p, axis=-1, keepdims=True)
            heads.append(_dot(p.astype(BF16), vwin[:, hs]) / l)
        o_ref[j * GRID_W:(j + 1) * GRID_W, :] = jnp.concatenate(heads, axis=-1).astype(o_ref.dtype)


def _na(qn, kn, vn, table, bsz, seq):
    rows = seq // GRID_W
    ng = rows // NA_GROUP
    blk = NA_GROUP * GRID_W
    t = bsz * seq
    cur = pl.BlockSpec((blk, W_GROUP), lambda b, g: (b * ng + g, 0))
    prev = pl.BlockSpec((blk, W_GROUP), lambda b, g: (b * ng + jnp.maximum(g - 1, 0), 0))
    nxt = pl.BlockSpec((blk, W_GROUP), lambda b, g: (b * ng + jnp.minimum(g + 1, ng - 1), 0))
    return pl.pallas_call(
        functools.partial(_na_kernel, rows),
        grid=(bsz, ng),
        in_specs=[cur, prev, cur, nxt, prev, cur, nxt, _const_spec((NA_KH, N_HEADS, GRID_W, NA_WIN))],
        out_specs=cur,
        out_shape=jax.ShapeDtypeStruct((t, W_GROUP), BF16),
        scratch_shapes=[pltpu.VMEM((3 * blk, W_GROUP), BF16), pltpu.VMEM((3 * blk, W_GROUP), BF16)],
        compiler_params=_params(("parallel", "parallel")),
        name="na",
    )(qn, kn, kn, kn, vn, vn, vn, table)


def _na_table(rpb):
    didx = np.arange(NA_KH)[:, None]
    kk = np.arange(NA_KH)[None, :]
    dr = (kk + (NA_KH - 1) - didx).reshape(-1)
    cc = np.arange(GRID_W)[:, None]
    kc = np.arange(GRID_W)[None, :]
    c0 = np.clip(cc - NA_KW // 2, 0, GRID_W - NA_KW)
    valid = (kc >= c0) & (kc < c0 + NA_KW)
    dc = np.clip(kc - cc + (NA_KW - 1), 0, 2 * NA_KW - 2).reshape(-1)
    row_sel = np.zeros((dr.size, 2 * NA_KH - 1), np.float32)
    row_sel[np.arange(dr.size), dr] = 1.0
    col_sel = np.zeros((2 * NA_KW - 1, dc.size), np.float32)
    col_sel[dc, np.arange(dc.size)] = 1.0
    bias = jnp.einsum('ar,hrs,sb->hab', row_sel, rpb, col_sel, precision=lax.Precision.HIGHEST)
    bias = bias.reshape(N_HEADS, NA_KH, NA_KH, GRID_W, GRID_W)
    tab = jnp.where(valid[None, None, None], bias, NEG_BIG)
    return jnp.transpose(tab, (1, 0, 3, 2, 4)).reshape(NA_KH, N_HEADS, GRID_W, NA_WIN)


def _outproj_kernel(x_ref, a_ref, bf_ref, bb_ref, z_ref, gn_ref, c_ref, d_ref, w_ref, o_ref):
    ob = bf_ref[...] + bb_ref[...]
    ms = _dot_sel(ob * ob, _head_ones(), 2) * (1.0 / HEAD_DIM)
    out_b = (ob * lax.rsqrt(ms + EPS) * gn_ref[...] * _silu(z_ref[...])).astype(BF16)
    acc = x_ref[...]
    for gi, m in enumerate((a_ref[...], out_b, c_ref[...], d_ref[...])):
        acc = acc + _dot(m, w_ref[gi * W_GROUP:(gi + 1) * W_GROUP, :])
    o_ref[...] = acc


def _outproj(x, u, oa, obf, obb, gn, oc, od, w):
    t = x.shape[0]
    tm = 512
    xs = pl.BlockSpec((tm, D_MODEL), lambda i: (i, 0))
    ms = pl.BlockSpec((tm, W_GROUP), lambda i: (i, 0))
    zs = pl.BlockSpec((tm, W_GROUP), lambda i: (i, CB_ZB))
    return pl.pallas_call(
        _outproj_kernel,
        grid=(t // tm,),
        in_specs=[xs, ms, ms, ms, zs, _const_spec((1, W_GROUP)), ms, ms, _const_spec((D_MODEL, D_MODEL))],
        out_specs=xs,
        out_shape=jax.ShapeDtypeStruct((t, D_MODEL), F32),
        compiler_params=_params(("parallel",)),
        name="outproj",
    )(x, oa, obf, obb, u, gn, oc, od, w)


FF_CHUNK = 256


def _ffn_kernel(xm_ref, xp_ref, xn_ref, p_ref, n2_ref, wg_ref, wu_ref, cw_ref, cb_ref, wd_ref,
                n3_ref, pg_ref, pp_ref, o_ref):
    tm = xm_ref.shape[0]
    i = pl.program_id(1)
    first = i == 0
    last = i == pl.num_programs(1) - 1
    x = xm_ref[...]
    n2 = n2_ref[...]
    xp = jnp.where(first, 0.0, xp_ref[...])
    xn = jnp.where(last, 0.0, xn_ref[...])
    h2f = _rms(x, n2)
    h2 = h2f.astype(BF16)
    h2e = jnp.concatenate([_rms(xp, n2), h2f, _rms(xn, n2)], axis=0).astype(BF16)
    acc = x
    for f in range(D_FF // FF_CHUNK):
        fs = slice(f * FF_CHUNK, (f + 1) * FF_CHUNK)
        ge = _dot(h2e, wg_ref[:, fs])
        gate = (cw_ref[0:1, fs] * ge[7:7 + tm] + cw_ref[1:2, fs] * ge[8:8 + tm]
                + cw_ref[2:3, fs] * ge[9:9 + tm] + cb_ref[:, fs])
        up = _dot(h2, wu_ref[:, fs])
        acc = acc + _dot((_gelu_tanh(gate) * up).astype(BF16), wd_ref[fs, :])
    h3 = _rms(acc, n3_ref[...]).astype(BF16)
    gate3 = _sigmoid(_dot(h3, pg_ref[...]))
    o_ref[...] = acc + gate3 * _dot(p_ref[...].astype(BF16), pp_ref[...])


def _ffn(x, p, bsz, seq, n2, wg, wu, cw, cb, wd, n3, pg, pp):
    tm = 512
    ns = seq // tm
    t = bsz * seq
    xm, xp, xn = _row_specs(tm, ns, t, 0, False, width=D_MODEL)
    return pl.pallas_call(
        _ffn_kernel,
        grid=(bsz, ns),
        in_specs=[xm, xp, xn, pl.BlockSpec((tm, PLE_DIM), lambda b, i: (b * ns + i, 0)),
                  _const_spec((1, D_MODEL)), _const_spec((D_MODEL, D_FF)), _const_spec((D_MODEL, D_FF)),
                  _const_spec((3, D_FF)), _const_spec((1, D_FF)), _const_spec((D_FF, D_MODEL)),
                  _const_spec((1, D_MODEL)), _const_spec((D_MODEL, D_MODEL)), _const_spec((PLE_DIM, D_MODEL))],
        out_specs=pl.BlockSpec((tm, D_MODEL), lambda b, i: (b * ns + i, 0)),
        out_shape=jax.ShapeDtypeStruct((t, D_MODEL), F32),
        compiler_params=_params(("parallel", "parallel")),
        name="ffn_ple",
    )(x, x, x, p, n2, wg, wu, cw, cb, wd, n3, pg, pp)


def _block_diag(w):
    h, n, _ = w.shape
    eye = jnp.eye(h, dtype=w.dtype)
    return (eye[:, None, :, None] * w[:, :, None, :]).reshape(h * n, h * n)


def _row(v):
    return v.reshape(1, -1).astype(F32)


def _lane_pad(v8, offset):
    return jnp.zeros((1, LANES), F32).at[0, offset:offset + 2 * N_HEADS].set(v8.reshape(-1).astype(F32))


def _expander(offset):
    e = np.zeros((LANES, W_GROUP), np.float32)
    for h in range(N_HEADS):
        e[offset + h, h * HEAD_DIM:(h + 1) * HEAD_DIM] = 1.0
    return jnp.asarray(e, BF16)


def _layer(x, p, lw, bsz, seq):
    w_in = _reorder_in_cols(lw['w_in'].astype(BF16))
    u = _inproj(x, _row(lw['norm1']), w_in)

    cw_a, cb_a = lw['conv_a_w'].astype(F32), _row(lw['conv_a_b'])
    h_dir = None
    for d in (1, 0):
        h_dir = _rglru(u, bsz, seq, d == 1, cw_a, cb_a,
                       _block_diag(lw['lru_wr'][d]).astype(BF16), _row(lw['lru_br'][d]),
                       _block_diag(lw['lru_wi'][d]).astype(BF16), _row(lw['lru_bi'][d]),
                       _row(lw['lru_lambda'][d]), hb=h_dir)
    out_a = h_dir

    gconv = lw['gdn_conv'].astype(F32)
    cwq, cwk, cwv = gconv[:, :W_GROUP], gconv[:, W_GROUP:2 * W_GROUP], gconv[:, 2 * W_GROUP:]
    alog = _lane_pad(lw['gdn_a_log'], 2 * N_HEADS)
    dtb = _lane_pad(lw['gdn_dt_bias'], 2 * N_HEADS)
    gn = _row(jnp.tile(lw['gdn_norm'], N_HEADS))
    eb = jnp.stack([_expander(d * N_HEADS) for d in range(2)])
    eg = jnp.stack([_expander(2 * N_HEADS + d * N_HEADS) for d in range(2)])
    o_b_fwd, o_b_bwd = _gdn_scan(_gdn_prep(u, bsz, seq, cwq, cwk, cwv, alog, dtb, eb, eg), bsz, seq)

    half = HEAD_DIM // 2
    inv_freq = ROPE_BASE ** (-jnp.arange(half, dtype=F32) / half)
    ang = jnp.arange(seq, dtype=F32)[:, None] * inv_freq[None, :]
    cos = jnp.tile(jnp.cos(ang), (1, N_HEADS))
    sin = jnp.tile(jnp.sin(ang), (1, N_HEADS))
    log_gamma = jnp.log1p(-jnp.exp2(-lw['ret_decay'].astype(F32)))
    rn = _row(jnp.tile(lw['ret_norm'], N_HEADS))
    o_dir = None
    for d in (1, 0):
        lg = log_gamma[d]
        lgq = _row(jnp.tile(jnp.repeat(lg, half), 2))
        lgv = _row(jnp.repeat(lg, HEAD_DIM))
        if d == 1:
            o_dir = _ret(u, bsz, seq, True, cos, sin, lgq, lgv)
        else:
            o_dir = _ret(u, bsz, seq, False, cos, sin, lgq, lgv,
                         lgfc=_row(jnp.repeat(log_gamma[0], RET_CHUNK)), lgbc=_row(jnp.repeat(log_gamma[1], RET_CHUNK)),
                         rn=rn, ob=o_dir)
    out_c = o_dir

    qn, kn, vn = _na_prep(u, _row(jnp.tile(lw['na_qnorm'], N_HEADS)), _row(jnp.tile(lw['na_knorm'], N_HEADS)))
    out_d = _na(qn, kn, vn, _na_table(lw['na_rpb'].astype(F32)), bsz, seq)

    x = _outproj(x, u, out_a, o_b_fwd, o_b_bwd, gn, out_c, out_d, lw['w_out'].astype(BF16))
    return _ffn(x, p, bsz, seq, _row(lw['norm2']), lw['ffn_wg'].astype(BF16), lw['ffn_wu'].astype(BF16),
                lw['ffn_conv_w'].astype(F32), _row(lw['ffn_conv_b']), lw['ffn_wd'].astype(BF16),
                _row(lw['norm3']), lw['ple_gate'].astype(BF16), lw['ple_proj'].astype(BF16))


_WEIGHT_NAMES = ('norm1', 'norm2', 'norm3', 'w_in', 'w_out', 'conv_a_w', 'conv_a_b', 'lru_wr', 'lru_br',
                 'lru_wi', 'lru_bi', 'lru_lambda', 'gdn_conv', 'gdn_a_log', 'gdn_dt_bias', 'gdn_norm',
                 'ret_decay', 'ret_norm', 'na_qnorm', 'na_knorm', 'na_rpb', 'ffn_wg', 'ffn_wu',
                 'ffn_conv_w', 'ffn_conv_b', 'ffn_wd', 'ple_proj', 'ple_gate')


def _trunk(x, p, weights):
    bsz, seq, _ = x.shape
    xf = x.reshape(bsz * seq, D_MODEL)
    for i in range(DEPTH):
        lw = {n: w[i] for n, w in weights.items()}
        xf = _layer(xf, p[i].reshape(bsz * seq, PLE_DIM), lw, bsz, seq)
    return xf.reshape(bsz, seq, D_MODEL)


def kernel(x_prompt, x_sample, p_prompt, p_sample, norm1, norm2, norm3, w_in, w_out, conv_a_w, conv_a_b, lru_wr, lru_br, lru_wi, lru_bi, lru_lambda, gdn_conv, gdn_a_log, gdn_dt_bias, gdn_norm, ret_decay, ret_norm, na_qnorm, na_knorm, na_rpb, ffn_wg, ffn_wu, ffn_conv_w, ffn_conv_b, ffn_wd, ple_proj, ple_gate):
    weights = dict(zip(_WEIGHT_NAMES, (norm1, norm2, norm3, w_in, w_out, conv_a_w, conv_a_b, lru_wr, lru_br,
                                       lru_wi, lru_bi, lru_lambda, gdn_conv, gdn_a_log, gdn_dt_bias, gdn_norm,
                                       ret_decay, ret_norm, na_qnorm, na_knorm, na_rpb, ffn_wg, ffn_wu,
                                       ffn_conv_w, ffn_conv_b, ffn_wd, ple_proj, ple_gate)))
    return (_trunk(x_prompt, p_prompt, weights), _trunk(x_sample, p_sample, weights))
```

```python
import functools

import jax
import jax.numpy as jnp
import numpy as np
from jax import lax
from jax.experimental import pallas as pl
from jax.experimental.pallas import tpu as pltpu

F32 = jnp.float32
BF16 = jnp.bfloat16

D_MODEL = 1024
DEPTH = 2
PLE_DIM = 256
GRID_W = 64
W_GROUP = 256
HEAD_DIM = 64
N_HEADS = 4
LRU_C = 8.0
CHUNK = 64
ROPE_BASE = 10000.0
NA_KH = 8
NA_KW = 16
D_FF = 2816
EPS = 1e-6
NEG_BIG = -1e30

SUBLANES = 8
LANES = 128
VMEM_LIMIT = 56 * 1024 * 1024

CB_XA, CB_YA, CB_QB, CB_KB, CB_VB, CB_ZB, CB_QC, CB_KC, CB_VC, CB_GC, CB_QD, CB_KD, CB_VD = range(13)
CB128_AB = 26
IN_COLS_PAD = 13 * W_GROUP + LANES
IN_COLS = 13 * W_GROUP + 4 * N_HEADS


def _reorder_in_cols(w):
    rows = w.shape[0]
    half = HEAD_DIM // 2

    def halves_first(cols):
        return cols.reshape(rows, N_HEADS, 2, half).transpose(0, 2, 1, 3).reshape(rows, W_GROUP)

    ab0 = 6 * W_GROUP
    qc0 = ab0 + 4 * N_HEADS
    parts = [w[:, :ab0], halves_first(w[:, qc0:qc0 + W_GROUP]), halves_first(w[:, qc0 + W_GROUP:qc0 + 2 * W_GROUP]),
             w[:, qc0 + 2 * W_GROUP:], w[:, ab0:qc0], jnp.zeros((rows, LANES - 4 * N_HEADS), w.dtype)]
    out = jnp.concatenate(parts, axis=1)
    assert out.shape[1] == IN_COLS_PAD
    return out


def _sigmoid(x):
    return 1.0 / (1.0 + jnp.exp(-x))


def _silu(x):
    return x * _sigmoid(x)


def _softplus(x):
    return jnp.maximum(x, 0.0) + jnp.log1p(jnp.exp(-jnp.abs(x)))


def _gelu_tanh(x):
    return x * (0.5 * (1.0 + jnp.tanh(0.7978845608028654 * (x + 0.044715 * (x * x * x)))))


def _rms(x, gain):
    ms = jnp.mean(x * x, axis=-1, keepdims=True)
    return x * lax.rsqrt(ms + EPS) * gain


def _dot(a, b):
    return jnp.dot(a, b, preferred_element_type=F32)


def _split_bf16(x, terms):
    parts = []
    for _ in range(terms):
        p = x.astype(BF16)
        parts.append(p)
        x = x - p.astype(F32)
    return parts


def _dot_sel(x, sel, terms):
    return sum(_dot(p, sel) for p in _split_bf16(x, terms))


def _sel_dot(sel, x, terms):
    return sum(_dot(sel, p) for p in _split_bf16(x, terms))


def _dot_nt(a, b):
    return lax.dot_general(a, b, (((1,), (1,)), ((), ())), preferred_element_type=F32)


def _dot_tn(a, b):
    return lax.dot_general(a, b, (((0,), (0,)), ((), ())), preferred_element_type=F32)


def _iota(shape, dim):
    return lax.broadcasted_iota(jnp.int32, shape, dim)


def _head_ones():
    return (_iota((W_GROUP, W_GROUP), 0) // HEAD_DIM == _iota((W_GROUP, W_GROUP), 1) // HEAD_DIM).astype(BF16)


def _stack4(x, mask):
    return jnp.where(mask, jnp.concatenate([x] * N_HEADS, axis=0), jnp.zeros((), x.dtype))


def _conv4(xm, xp, xn, w_ref, first, last):
    ts = xm.shape[0]
    xp = jnp.where(first, 0.0, xp)
    xn = jnp.where(last, 0.0, xn)
    xe = jnp.concatenate([xp, xm, xn], axis=0)
    out = w_ref[0:1, :] * xe[6:6 + ts]
    out = out + w_ref[1:2, :] * xe[7:7 + ts]
    out = out + w_ref[2:3, :] * xe[8:8 + ts]
    out = out + w_ref[3:4, :] * xe[9:9 + ts]
    return out


def _time_block(reverse):
    i = pl.program_id(1)
    n = pl.num_programs(1)
    tb = (n - 1 - i) if reverse else i
    return tb, tb == 0, tb == n - 1


def _row_specs(ts, ns, total_rows, col_block, reverse, width=W_GROUP):
    r8 = ts // SUBLANES
    last8 = total_rows // SUBLANES - 1

    def tb(i):
        return (ns - 1 - i) if reverse else i

    main = pl.BlockSpec((ts, width), lambda b, i: (b * ns + tb(i), col_block))
    prev = pl.BlockSpec((SUBLANES, width), lambda b, i: (jnp.maximum((b * ns + tb(i)) * r8 - 1, 0), col_block))
    nxt = pl.BlockSpec((SUBLANES, width), lambda b, i: (jnp.minimum((b * ns + tb(i) + 1) * r8, last8), col_block))
    return main, prev, nxt


def _const_spec(shape):
    nd = len(shape)
    return pl.BlockSpec(shape, lambda *_: (0,) * nd, pipeline_mode=pl.Buffered(1))


def _params(sem):
    return pltpu.CompilerParams(dimension_semantics=sem, vmem_limit_bytes=VMEM_LIMIT)


def _inproj_kernel(x_ref, g_ref, w_ref, o_ref):
    h = _rms(x_ref[...], g_ref[...]).astype(BF16)
    o_ref[...] = _dot(h, w_ref[...])


def _inproj(x, gain, w):
    t = x.shape[0]
    tm = 512
    return pl.pallas_call(
        _inproj_kernel,
        grid=(t // tm,),
        in_specs=[pl.BlockSpec((tm, D_MODEL), lambda i: (i, 0)),
                  _const_spec((1, D_MODEL)),
                  _const_spec((D_MODEL, IN_COLS_PAD))],
        out_specs=pl.BlockSpec((tm, IN_COLS_PAD), lambda i: (i, 0)),
        out_shape=jax.ShapeDtypeStruct((t, IN_COLS_PAD), F32),
        compiler_params=_params(("parallel",)),
        name="inproj",
    )(x, gain, w)


def _rglru_kernel(reverse, *refs):
    if reverse:
        (xm_ref, xp_ref, xn_ref, cw_ref, cb_ref, wr_ref, br_ref, wi_ref, bi_ref, lam_ref,
         o_ref, a_s, b_s, p_s, ac_s, c_s, carry, h_s) = refs
    else:
        (xm_ref, xp_ref, xn_ref, cw_ref, cb_ref, wr_ref, br_ref, wi_ref, bi_ref, lam_ref, ya_ref, hb_ref,
         o_ref, a_s, b_s, p_s, ac_s, c_s, carry, h_s) = refs
    ts = xm_ref.shape[0]
    ng = ts // SUBLANES
    tb, first, last = _time_block(reverse)

    @pl.when(pl.program_id(1) == 0)
    def _():
        carry[...] = jnp.zeros_like(carry)

    x = _conv4(xm_ref[...], xp_ref[...], xn_ref[...], cw_ref, first, last) + cb_ref[...]
    xb = x.astype(BF16)
    r = _sigmoid(_dot(xb, wr_ref[...]) + br_ref[...])
    ig = _sigmoid(_dot(xb, wi_ref[...]) + bi_ref[...])
    log_a = (-LRU_C) * r * _softplus(-lam_ref[...])
    a = jnp.exp(log_a)
    th = jnp.tanh(-log_a)
    bx = jnp.sqrt(2.0 * th / (1.0 + th)) * (ig * x)
    nl = W_GROUP // LANES
    for hf in range(nl):
        a_s[hf] = a[:, hf * LANES:(hf + 1) * LANES]
        b_s[hf] = bx[:, hf * LANES:(hf + 1) * LANES]

    def slab(ref, k):
        return jnp.concatenate([ref[hf, pl.ds(k, ng, stride=SUBLANES), :] for hf in range(nl)], axis=-1)

    order = list(range(SUBLANES))[::-1] if reverse else list(range(SUBLANES))
    p = None
    ac = None
    for k in order:
        a_k = slab(a_s, k)
        b_k = slab(b_s, k)
        if p is None:
            p, ac = b_k, a_k
        else:
            p = a_k * p + b_k
            ac = a_k * ac
        p_s[k] = p
        ac_s[k] = ac

    k_last = order[-1]

    def body(g, c):
        gi = (ng - 1 - g) if reverse else g
        c_s[pl.ds(gi, 1), :] = c
        return ac_s[k_last, pl.ds(gi, 1), :] * c + p_s[k_last, pl.ds(gi, 1), :]

    carry[...] = lax.fori_loop(0, ng, body, carry[...])
    c_in = c_s[...]
    for k in order:
        h_k = p_s[k] + ac_s[k] * c_in
        for hf in range(nl):
            h_s[hf, pl.ds(k, ng, stride=SUBLANES), :] = h_k[:, hf * LANES:(hf + 1) * LANES]
    h = jnp.concatenate([h_s[hf] for hf in range(nl)], axis=-1)
    if reverse:
        o_ref[...] = h
    else:
        o_ref[...] = ((h + hb_ref[...]) * _gelu_tanh(ya_ref[...])).astype(o_ref.dtype)


def _rglru(u, bsz, seq, reverse, cw, cb, wr, br, wi, bi, lam, hb=None):
    ts = 512
    ns = seq // ts
    t = bsz * seq
    ng = ts // SUBLANES
    xm, xp, xn = _row_specs(ts, ns, t, CB_XA, reverse)
    in_specs = [xm, xp, xn, _const_spec((4, W_GROUP)), _const_spec((1, W_GROUP)),
                _const_spec((W_GROUP, W_GROUP)), _const_spec((1, W_GROUP)),
                _const_spec((W_GROUP, W_GROUP)), _const_spec((1, W_GROUP)), _const_spec((1, W_GROUP))]
    args = [u, u, u, cw, cb, wr, br, wi, bi, lam]
    halves = (W_GROUP // LANES, ts, LANES)
    scratch = [pltpu.VMEM(halves, F32), pltpu.VMEM(halves, F32),
               pltpu.VMEM((SUBLANES, ng, W_GROUP), F32), pltpu.VMEM((SUBLANES, ng, W_GROUP), F32),
               pltpu.VMEM((ng, W_GROUP), F32), pltpu.VMEM((1, W_GROUP), F32), pltpu.VMEM(halves, F32)]
    if reverse:
        out_dtype = F32
        out_spec = pl.BlockSpec((ts, W_GROUP), lambda b, i: (b * ns + ns - 1 - i, 0))
    else:
        ya = _row_specs(ts, ns, t, CB_YA, False)[0]
        in_specs += [ya, pl.BlockSpec((ts, W_GROUP), lambda b, i: (b * ns + i, 0))]
        args += [u, hb]
        out_dtype = BF16
        out_spec = pl.BlockSpec((ts, W_GROUP), lambda b, i: (b * ns + i, 0))
    return pl.pallas_call(
        functools.partial(_rglru_kernel, reverse),
        grid=(bsz, ns),
        in_specs=in_specs,
        out_specs=out_spec,
        out_shape=jax.ShapeDtypeStruct((t, W_GROUP), out_dtype),
        scratch_shapes=scratch,
        compiler_params=_params(("parallel", "arbitrary")),
        name="rglru_bwd" if reverse else "rglru_fwd",
    )(*args)


GDN_BLOCK = 256


def _gdn_prep_kernel(qm, qp, qn, km, kp, kn, vm, vp, vn, ab_ref, cwq, cwk, cwv, alog_ref, dtb_ref, eb_ref, eg_ref,
                     *outs):
    ts = qm.shape[0]
    nc = ts // CHUNK
    c = CHUNK
    i = pl.program_id(1)
    first = i == 0
    last = i == pl.num_programs(1) - 1
    out_refs = (outs[0:5], outs[5:10])

    ones_h = _head_ones()
    q = _silu(_conv4(qm[...], qp[...], qn[...], cwq, first, last))
    k = _silu(_conv4(km[...], kp[...], kn[...], cwk, first, last))
    v = _silu(_conv4(vm[...], vp[...], vn[...], cwv, first, last))
    q = q * lax.rsqrt(_dot_sel(q * q, ones_h, 2) + EPS) * (HEAD_DIM ** -0.5)
    k = k * lax.rsqrt(_dot_sel(k * k, ones_h, 2) + EPS)
    ab = ab_ref[...]
    sig_parts = _split_bf16(_sigmoid(ab), 2)
    g_parts = _split_bf16(-jnp.exp(alog_ref[...]) * _softplus(ab + dtb_ref[...]), 3)
    beta = [sum(_dot(p, eb_ref[d]) for p in sig_parts) for d in range(2)]
    gate = [sum(_dot(p, eg_ref[d]) for p in g_parts) for d in range(2)]

    ii = _iota((c, W_GROUP), 0)
    jj = _iota((c, W_GROUP), 1) % c
    r2 = _iota((c, c), 0)
    c2 = _iota((c, c), 1)
    incl = (jj <= ii, jj >= ii)
    strict = (jj < ii, jj > ii)
    tri = ((c2 <= r2).astype(BF16), (c2 >= r2).astype(BF16))
    ucat = ((ii > jj).astype(F32), (ii < jj).astype(F32))
    eye_cat = (ii == jj).astype(F32)
    blk = _iota((W_GROUP, W_GROUP), 0) // HEAD_DIM == _iota((W_GROUP, W_GROUP), 1) // HEAD_DIM

    sls = [slice(ci * c, (ci + 1) * c) for ci in range(nc)]
    kq = [_dot_nt(jnp.concatenate([k[sl], q[sl]], axis=0).astype(BF16), _stack4(k[sl].astype(BF16), blk))
          for sl in sls]
    units = [(d, ci) for ci in range(nc) for d in range(2)]
    gd = [_sel_dot(tri[d], jnp.concatenate([gate[d][sls[ci]], gate[d][sls[ci]] * ucat[d]], axis=1), 3)
          for d, ci in units]
    gcum = [x[:, :W_GROUP] for x in gd]
    decay = [jnp.where(incl[d], jnp.exp(jnp.where(incl[d], x[:, W_GROUP:], 0.0)), 0.0)
             for (d, ci), x in zip(units, gd)]
    lmat = [jnp.where(strict[d], beta[d][sls[ci]] * kq[ci][:c] * dec, 0.0) for (d, ci), dec in zip(units, decay)]
    attn = [kq[ci][c:] * dec for (d, ci), dec in zip(units, decay)]
    pinv = [eye_cat - l for l in lmat]
    lb = [l.astype(BF16) for l in lmat]
    lpow = [_dot(x, _stack4(x, blk)) for x in lb]
    for lvl in range(5):
        z = [_stack4(x.astype(BF16), blk) for x in lpow]
        if lvl < 4:
            r = [_dot(jnp.concatenate([p, lp], axis=0).astype(BF16), zz) for p, lp, zz in zip(pinv, lpow, z)]
            pinv = [p + x[:c] for p, x in zip(pinv, r)]
            lpow = [x[c:] for x in r]
        else:
            pinv = [p + _dot(p.astype(BF16), zz) for p, zz in zip(pinv, z)]
    egc = [jnp.exp(x) for x in gcum]
    pb = [p.astype(BF16) for p in pinv]
    for n, (d, ci) in enumerate(units):
        sl = sls[ci]
        u_ref, wq_ref, at_ref, kg_ref, ds_ref = out_refs[d]
        bc = beta[d][sl]
        u_ref[sl, :] = _dot(pb[n], _stack4((v[sl] * bc).astype(BF16), blk))
        w_key = _dot(pb[n], _stack4((k[sl] * bc * egc[n]).astype(BF16), blk))
        g_last = gcum[n][0:1, :] if d else gcum[n][c - 1:c, :]
        wq_ref[2 * ci * c:(2 * ci + 1) * c, :] = w_key.astype(BF16)
        wq_ref[(2 * ci + 1) * c:(2 * ci + 2) * c, :] = (q[sl] * egc[n]).astype(BF16)
        at_ref[sl, :] = attn[n].astype(BF16)
        kg_ref[sl, :] = (k[sl] * jnp.exp(g_last - gcum[n])).astype(BF16)
        ds_ref[ci] = jnp.exp(g_last)


def _gdn_prep(u, bsz, seq, cwq, cwk, cwv, alog, dtb, eb, eg):
    ts = GDN_BLOCK
    ns = seq // ts
    t = bsz * seq
    nc = ts // CHUNK
    specs = []
    for cb in (CB_QB, CB_KB, CB_VB):
        specs += list(_row_specs(ts, ns, t, cb, False))
    specs.append(pl.BlockSpec((ts, LANES), lambda b, i: (b * ns + i, CB128_AB)))
    specs += [_const_spec((4, W_GROUP))] * 3
    specs += [_const_spec((1, LANES))] * 2 + [_const_spec((2, LANES, W_GROUP))] * 2

    def row(b, i):
        return (b * ns + i, 0)

    one_dir_specs = [pl.BlockSpec((ts, W_GROUP), row), pl.BlockSpec((2 * ts, W_GROUP), row),
                     pl.BlockSpec((ts, W_GROUP), row), pl.BlockSpec((ts, W_GROUP), row),
                     pl.BlockSpec((nc, 1, W_GROUP), lambda b, i: (b * ns + i, 0, 0))]
    one_dir_shapes = [jax.ShapeDtypeStruct((t, W_GROUP), F32), jax.ShapeDtypeStruct((2 * t, W_GROUP), BF16),
                      jax.ShapeDtypeStruct((t, W_GROUP), BF16), jax.ShapeDtypeStruct((t, W_GROUP), BF16),
                      jax.ShapeDtypeStruct((t // CHUNK, 1, W_GROUP), F32)]
    return pl.pallas_call(
        _gdn_prep_kernel,
        grid=(bsz, ns),
        in_specs=specs,
        out_specs=one_dir_specs * 2,
        out_shape=one_dir_shapes * 2,
        compiler_params=_params(("parallel", "parallel")),
        name="gdn_prep",
    )(*([u] * 10 + [cwq, cwk, cwv, alog, dtb, eb, eg]))


def _gdn_scan_kernel(uf, wqf, atf, kgf, dsf, ub, wqb, atb, kgb, dsb, of_ref, ob_ref, s_ref):
    bsz, ts, _ = uf.shape
    nc = ts // CHUNK
    c = CHUNK

    @pl.when(pl.program_id(0) == 0)
    def _():
        s_ref[...] = jnp.zeros_like(s_ref)

    blk = _iota((W_GROUP, W_GROUP), 0) // HEAD_DIM == _iota((W_GROUP, W_GROUP), 1) // HEAD_DIM
    ins = ((uf, wqf, atf, kgf, dsf, of_ref), (ub, wqb, atb, kgb, dsb, ob_ref))
    chains = [(d, b) for b in range(bsz) for d in range(2)]
    state = [s_ref[d, b] for d, b in chains]
    for step in range(nc):
        pos = (step, nc - 1 - step)
        ws = [_dot(ins[d][1][b, 2 * pos[d] * c:(2 * pos[d] + 2) * c, :], s.astype(BF16))
              for (d, b), s in zip(chains, state)]
        vnb = [(ins[d][0][b, pos[d] * c:(pos[d] + 1) * c, :] - x[:c]).astype(BF16) for (d, b), x in zip(chains, ws)]
        for (d, b), x, vn in zip(chains, ws, vnb):
            sl = slice(pos[d] * c, (pos[d] + 1) * c)
            ins[d][5][b, sl, :] = x[c:] + _dot(ins[d][2][b, sl, :], _stack4(vn, blk))
        state = [s * ins[d][4][b, pos[d]] + jnp.where(blk, _dot_tn(ins[d][3][b, pos[d] * c:(pos[d] + 1) * c, :], vn), 0.0)
                 for (d, b), s, vn in zip(chains, state, vnb)]
    for (d, b), s in zip(chains, state):
        s_ref[d, b] = s


def _gdn_scan(prep, bsz, seq):
    ts = GDN_BLOCK
    ns = seq // ts
    nc = ts // CHUNK
    specs, args = [], []
    for d in range(2):
        tb = (lambda i: ns - 1 - i) if d else (lambda i: i)
        u_, wq, at, kg, ds = prep[5 * d:5 * d + 5]
        args += [u_.reshape(bsz, seq, W_GROUP), wq.reshape(bsz, 2 * seq, W_GROUP), at.reshape(bsz, seq, W_GROUP),
                 kg.reshape(bsz, seq, W_GROUP), ds.reshape(bsz, seq // CHUNK, 1, W_GROUP)]

        def blk3(rows, tb=tb):
            return pl.BlockSpec((bsz, rows, W_GROUP), lambda i: (0, tb(i), 0))

        specs += [blk3(ts), blk3(2 * ts), blk3(ts), blk3(ts),
                  pl.BlockSpec((bsz, nc, 1, W_GROUP), lambda i, tb=tb: (0, tb(i), 0, 0))]
    out_shape = jax.ShapeDtypeStruct((bsz, seq, W_GROUP), F32)
    o_fwd, o_bwd = pl.pallas_call(
        _gdn_scan_kernel,
        grid=(ns,),
        in_specs=specs,
        out_specs=[pl.BlockSpec((bsz, ts, W_GROUP), lambda i: (0, i, 0)),
                   pl.BlockSpec((bsz, ts, W_GROUP), lambda i: (0, ns - 1 - i, 0))],
        out_shape=[out_shape, out_shape],
        scratch_shapes=[pltpu.VMEM((2, bsz, W_GROUP, W_GROUP), F32)],
        compiler_params=_params(("arbitrary",)),
        name="gdn_scan",
    )(*args)
    return o_fwd.reshape(bsz * seq, W_GROUP), o_bwd.reshape(bsz * seq, W_GROUP)


RET_CHUNK = 128
RET_BLOCK = 512


def _ret_kernel(reverse, *refs):
    if reverse:
        (q_ref, k_ref, v_ref, cos_ref, sin_ref, lgq_ref, lgv_ref, o_ref, s_ref) = refs
    else:
        (q_ref, k_ref, v_ref, cos_ref, sin_ref, lgq_ref, lgv_ref, lgfc_ref, lgbc_ref, g_ref, ob_ref, rn_ref,
         o_ref, s_ref, dm_ref) = refs
    ts = q_ref.shape[0]
    cr = RET_CHUNK
    nc = ts // cr
    nk = N_HEADS * cr

    @pl.when(pl.program_id(1) == 0)
    def _():
        s_ref[...] = jnp.zeros_like(s_ref)
        if not reverse:
            ii = _iota((cr, nk), 0)
            jj = _iota((cr, nk), 1) % cr
            dist = (ii - jj).astype(F32)
            lower, upper = jj <= ii, jj >= ii
            dm_ref[...] = (jnp.where(lower, jnp.exp(jnp.where(lower, dist * lgfc_ref[...], 0.0)), 0.0)
                           + jnp.where(upper, jnp.exp(jnp.where(upper, -dist * lgbc_ref[...], 0.0)), 0.0))

    cos = cos_ref[...]
    sin = sin_ref[...]

    def rope(t):
        t1, t2 = t[:, :LANES], t[:, LANES:]
        return jnp.concatenate([t1 * cos - t2 * sin, t1 * sin + t2 * cos], axis=-1)

    q = rope(q_ref[...])
    k = rope(k_ref[...]) * (HEAD_DIM ** -0.5)
    v = v_ref[...]
    lgq = lgq_ref[...]
    lgv = lgv_ref[...]
    pos = _iota((cr, W_GROUP), 0).astype(F32)
    rows_qk = (_iota((W_GROUP, W_GROUP), 0) % LANES) // 32
    blk_state = rows_qk == _iota((W_GROUP, W_GROUP), 1) // HEAD_DIM
    if reverse:
        q_fac = jnp.exp((cr - pos) * lgq)
        k_fac = jnp.exp(pos * lgq)
    else:
        q_fac = jnp.exp((pos + 1.0) * lgq)
        k_fac = jnp.exp((cr - 1.0 - pos) * lgq)
    s_fac = jnp.exp(cr * lgv)
    sls = [slice(ci * cr, (ci + 1) * cr) for ci in range(nc)]
    kv = [jnp.where(blk_state, _dot_tn((k[sl] * k_fac).astype(BF16), v[sl].astype(BF16)), 0.0) for sl in sls]
    s = s_ref[...]
    cross = [None] * nc
    for ci in (range(nc - 1, -1, -1) if reverse else range(nc)):
        cross[ci] = _dot((q[sls[ci]] * q_fac).astype(BF16), s.astype(BF16))
        s = s * s_fac + kv[ci]
    s_ref[...] = s
    if reverse:
        o_ref[...] = jnp.concatenate(cross, axis=0)
        return
    stack_head = _iota((nk, W_GROUP), 0) // cr
    k_mask = stack_head == (_iota((nk, W_GROUP), 1) % LANES) // 32
    v_mask = stack_head == _iota((nk, W_GROUP), 1) // HEAD_DIM
    qk = [_dot_nt(q[sl].astype(BF16), _stack4(k[sl].astype(BF16), k_mask)) for sl in sls]
    dm = dm_ref[...]
    intra = [_dot((x * dm).astype(BF16), _stack4(v[sl].astype(BF16), v_mask)) for x, sl in zip(qk, sls)]
    o = jnp.concatenate([a + b for a, b in zip(intra, cross)], axis=0) + ob_ref[...]
    ones_h = _head_ones()
    oc = o - _dot_sel(o, ones_h, 3) * (1.0 / HEAD_DIM)
    var = _dot_sel(oc * oc, ones_h, 2) * (1.0 / HEAD_DIM)
    o_ref[...] = (oc * lax.rsqrt(var + EPS) * rn_ref[...] * _silu(g_ref[...])).astype(o_ref.dtype)


def _ret(u, bsz, seq, reverse, cos, sin, lgq, lgv, lgfc=None, lgbc=None, rn=None, ob=None):
    cr = RET_CHUNK
    ts = RET_BLOCK
    ns = seq // ts
    t = bsz * seq
    tbf = (lambda i: ns - 1 - i) if reverse else (lambda i: i)

    def col(cb):
        return pl.BlockSpec((ts, W_GROUP), lambda b, i: (b * ns + tbf(i), cb))

    tab = pl.BlockSpec((ts, LANES), lambda b, i: (tbf(i), 0))
    specs = [col(CB_QC), col(CB_KC), col(CB_VC), tab, tab, _const_spec((1, W_GROUP)), _const_spec((1, W_GROUP))]
    args = [u, u, u, cos, sin, lgq, lgv]
    scratch = [pltpu.VMEM((W_GROUP, W_GROUP), F32)]
    if reverse:
        out_dtype = F32
    else:
        specs += [_const_spec((1, N_HEADS * cr)), _const_spec((1, N_HEADS * cr)), col(CB_GC),
                  pl.BlockSpec((ts, W_GROUP), lambda b, i: (b * ns + i, 0)), _const_spec((1, W_GROUP))]
        args += [lgfc, lgbc, u, ob, rn]
        scratch.append(pltpu.VMEM((cr, N_HEADS * cr), F32))
        out_dtype = BF16
    return pl.pallas_call(
        functools.partial(_ret_kernel, reverse),
        grid=(bsz, ns),
        in_specs=specs,
        out_specs=pl.BlockSpec((ts, W_GROUP), lambda b, i: (b * ns + tbf(i), 0)),
        out_shape=jax.ShapeDtypeStruct((t, W_GROUP), out_dtype),
        scratch_shapes=scratch,
        compiler_params=_params(("parallel", "arbitrary")),
        name="ret_bwd" if reverse else "ret_fwd",
    )(*args)


def _na_prep_kernel(q_ref, k_ref, v_ref, gq_ref, gk_ref, qo_ref, ko_ref, vo_ref):
    ones_h = _head_ones()
    inv = 1.0 / HEAD_DIM
    q = q_ref[...]
    k = k_ref[...]
    qo_ref[...] = (q * lax.rsqrt(_dot_sel(q * q, ones_h, 2) * inv + EPS) * gq_ref[...] * (HEAD_DIM ** -0.5)).astype(BF16)
    ko_ref[...] = (k * lax.rsqrt(_dot_sel(k * k, ones_h, 2) * inv + EPS) * gk_ref[...]).astype(BF16)
    vo_ref[...] = v_ref[...].astype(BF16)


def _na_prep(u, gq, gk):
    t = u.shape[0]
    tm = 512

    def col(cb):
        return pl.BlockSpec((tm, W_GROUP), lambda i: (i, cb))

    out = pl.BlockSpec((tm, W_GROUP), lambda i: (i, 0))
    shp = jax.ShapeDtypeStruct((t, W_GROUP), BF16)
    return pl.pallas_call(
        _na_prep_kernel,
        grid=(t // tm,),
        in_specs=[col(CB_QD), col(CB_KD), col(CB_VD), _const_spec((1, W_GROUP)), _const_spec((1, W_GROUP))],
        out_specs=[out, out, out],
        out_shape=[shp, shp, shp],
        compiler_params=_params(("parallel",)),
        name="na_prep",
    )(u, u, u, gq, gk)


NA_GROUP = 8
NA_WIN = NA_KH * GRID_W
NA_ROWS_PER_PASS = 4


def _na_kernel(rows, q_ref, kp_ref, kc_ref, kn_ref, vp_ref, vc_ref, vn_ref, tab_ref, o_ref, kw, vw):
    g = pl.program_id(1)
    blk = NA_GROUP * GRID_W
    kw[0:blk, :] = kp_ref[...]
    kw[blk:2 * blk, :] = kc_ref[...]
    kw[2 * blk:3 * blk, :] = kn_ref[...]
    vw[0:blk, :] = vp_ref[...]
    vw[blk:2 * blk, :] = vc_ref[...]
    vw[2 * blk:3 * blk, :] = vn_ref[...]
    row_head = _iota((N_HEADS * GRID_W, W_GROUP), 0) // GRID_W
    lane_head = _iota((N_HEADS * GRID_W, W_GROUP), 1) // HEAD_DIM
    blk = row_head == lane_head
    out_head = _iota((GRID_W, W_GROUP), 1) // HEAD_DIM
    for j0 in range(0, NA_GROUP, NA_ROWS_PER_PASS):
        js = list(range(j0, j0 + NA_ROWS_PER_PASS))
        vwins, sc = [], []
        for j in js:
            r = g * NA_GROUP + j
            r0 = jnp.clip(r - NA_KH // 2, 0, rows - NA_KH)
            off = pl.multiple_of((r0 - g * NA_GROUP + NA_GROUP) * GRID_W, GRID_W)
            qst = _stack4(q_ref[j * GRID_W:(j + 1) * GRID_W, :], blk)
            vwins.append(vw[pl.ds(off, NA_WIN), :])
            sc.append(_dot_nt(qst, kw[pl.ds(off, NA_WIN), :]) + tab_ref[r - r0])
        mx = [jnp.max(s, axis=-1, keepdims=True) for s in sc]
        pr = [jnp.exp(s - m) for s, m in zip(sc, mx)]
        den = [jnp.sum(p, axis=-1, keepdims=True) for p in pr]
        pv = [_dot(p.astype(BF16), vw_) / l for p, vw_, l in zip(pr, vwins, den)]
        for j, o in zip(js, pv):
            res = o[0:GRID_W]
            for h in range(1, N_HEADS):
                res = jnp.where(out_head == h, o[h * GRID_W:(h + 1) * GRID_W], res)
            o_ref[j * GRID_W:(j + 1) * GRID_W, :] = res.astype(o_ref.dtype)


def _na(qn, kn, vn, table, bsz, seq):
    rows = seq // GRID_W
    ng = rows // NA_GROUP
    blk = NA_GROUP * GRID_W
    t = bsz * seq
    cur = pl.BlockSpec((blk, W_GROUP), lambda b, g: (b * ng + g, 0))
    prev = pl.BlockSpec((blk, W_GROUP), lambda b, g: (b * ng + jnp.maximum(g - 1, 0), 0))
    nxt = pl.BlockSpec((blk, W_GROUP), lambda b, g: (b * ng + jnp.minimum(g + 1, ng - 1), 0))
    return pl.pallas_call(
        functools.partial(_na_kernel, rows),
        grid=(bsz, ng),
        in_specs=[cur, prev, cur, nxt, prev, cur, nxt, _const_spec((NA_KH, N_HEADS * GRID_W, NA_WIN))],
        out_specs=cur,
        out_shape=jax.ShapeDtypeStruct((t, W_GROUP), BF16),
        scratch_shapes=[pltpu.VMEM((3 * blk, W_GROUP), BF16), pltpu.VMEM((3 * blk, W_GROUP), BF16)],
        compiler_params=_params(("parallel", "parallel")),
        name="na",
    )(qn, kn, kn, kn, vn, vn, vn, table)


def _na_table(rpb):
    didx = np.arange(NA_KH)[:, None]
    kk = np.arange(NA_KH)[None, :]
    dr = (kk + (NA_KH - 1) - didx).reshape(-1)
    cc = np.arange(GRID_W)[:, None]
    kc = np.arange(GRID_W)[None, :]
    c0 = np.clip(cc - NA_KW // 2, 0, GRID_W - NA_KW)
    valid = (kc >= c0) & (kc < c0 + NA_KW)
    dc = np.clip(kc - cc + (NA_KW - 1), 0, 2 * NA_KW - 2).reshape(-1)
    row_sel = np.zeros((dr.size, 2 * NA_KH - 1), np.float32)
    row_sel[np.arange(dr.size), dr] = 1.0
    col_sel = np.zeros((2 * NA_KW - 1, dc.size), np.float32)
    col_sel[dc, np.arange(dc.size)] = 1.0
    bias = jnp.einsum('ar,hrs,sb->hab', row_sel, rpb, col_sel, precision=lax.Precision.HIGHEST)
    bias = bias.reshape(N_HEADS, NA_KH, NA_KH, GRID_W, GRID_W)
    tab = jnp.where(valid[None, None, None], bias, NEG_BIG)
    return jnp.transpose(tab, (1, 0, 3, 2, 4)).reshape(NA_KH, N_HEADS * GRID_W, NA_WIN)


def _outproj_kernel(x_ref, a_ref, bf_ref, bb_ref, z_ref, gn_ref, c_ref, d_ref, w_ref, o_ref):
    ob = bf_ref[...] + bb_ref[...]
    ms = _dot_sel(ob * ob, _head_ones(), 2) * (1.0 / HEAD_DIM)
    out_b = (ob * lax.rsqrt(ms + EPS) * gn_ref[...] * _silu(z_ref[...])).astype(BF16)
    acc = x_ref[...]
    for gi, m in enumerate((a_ref[...], out_b, c_ref[...], d_ref[...])):
        acc = acc + _dot(m, w_ref[gi * W_GROUP:(gi + 1) * W_GROUP, :])
    o_ref[...] = acc


def _outproj(x, u, oa, obf, obb, gn, oc, od, w):
    t = x.shape[0]
    tm = 512
    xs = pl.BlockSpec((tm, D_MODEL), lambda i: (i, 0))
    ms = pl.BlockSpec((tm, W_GROUP), lambda i: (i, 0))
    zs = pl.BlockSpec((tm, W_GROUP), lambda i: (i, CB_ZB))
    return pl.pallas_call(
        _outproj_kernel,
        grid=(t // tm,),
        in_specs=[xs, ms, ms, ms, zs, _const_spec((1, W_GROUP)), ms, ms, _const_spec((D_MODEL, D_MODEL))],
        out_specs=xs,
        out_shape=jax.ShapeDtypeStruct((t, D_MODEL), F32),
        compiler_params=_params(("parallel",)),
        name="outproj",
    )(x, oa, obf, obb, u, gn, oc, od, w)


FF_CHUNK = 256


def _ffn_kernel(xm_ref, xp_ref, xn_ref, p_ref, n2_ref, wg_ref, wu_ref, cw_ref, cb_ref, wd_ref,
                n3_ref, pg_ref, pp_ref, o_ref):
    tm = xm_ref.shape[0]
    i = pl.program_id(1)
    first = i == 0
    last = i == pl.num_programs(1) - 1
    x = xm_ref[...]
    n2 = n2_ref[...]
    xp = jnp.where(first, 0.0, xp_ref[...])
    xn = jnp.where(last, 0.0, xn_ref[...])
    h2f = _rms(x, n2)
    h2 = h2f.astype(BF16)
    h2e = jnp.concatenate([_rms(xp, n2), h2f, _rms(xn, n2)], axis=0).astype(BF16)
    acc = x
    for f in range(D_FF // FF_CHUNK):
        fs = slice(f * FF_CHUNK, (f + 1) * FF_CHUNK)
        ge = _dot(h2e, wg_ref[:, fs])
        gate = (cw_ref[0:1, fs] * ge[7:7 + tm] + cw_ref[1:2, fs] * ge[8:8 + tm]
                + cw_ref[2:3, fs] * ge[9:9 + tm] + cb_ref[:, fs])
        up = _dot(h2, wu_ref[:, fs])
        acc = acc + _dot((_gelu_tanh(gate) * up).astype(BF16), wd_ref[fs, :])
    h3 = _rms(acc, n3_ref[...]).astype(BF16)
    gate3 = _sigmoid(_dot(h3, pg_ref[...]))
    o_ref[...] = acc + gate3 * _dot(p_ref[...].astype(BF16), pp_ref[...])


def _ffn(x, p, bsz, seq, n2, wg, wu, cw, cb, wd, n3, pg, pp):
    tm = 512
    ns = seq // tm
    t = bsz * seq
    xm, xp, xn = _row_specs(tm, ns, t, 0, False, width=D_MODEL)
    return pl.pallas_call(
        _ffn_kernel,
        grid=(bsz, ns),
        in_specs=[xm, xp, xn, pl.BlockSpec((tm, PLE_DIM), lambda b, i: (b * ns + i, 0)),
                  _const_spec((1, D_MODEL)), _const_spec((D_MODEL, D_FF)), _const_spec((D_MODEL, D_FF)),
                  _const_spec((3, D_FF)), _const_spec((1, D_FF)), _const_spec((D_FF, D_MODEL)),
                  _const_spec((1, D_MODEL)), _const_spec((D_MODEL, D_MODEL)), _const_spec((PLE_DIM, D_MODEL))],
        out_specs=pl.BlockSpec((tm, D_MODEL), lambda b, i: (b * ns + i, 0)),
        out_shape=jax.ShapeDtypeStruct((t, D_MODEL), F32),
        compiler_params=_params(("parallel", "parallel")),
        name="ffn_ple",
    )(x, x, x, p, n2, wg, wu, cw, cb, wd, n3, pg, pp)


def _block_diag(w):
    h, n, _ = w.shape
    eye = jnp.eye(h, dtype=w.dtype)
    return (eye[:, None, :, None] * w[:, :, None, :]).reshape(h * n, h * n)


def _row(v):
    return v.reshape(1, -1).astype(F32)


def _lane_pad(v8, offset):
    return jnp.zeros((1, LANES), F32).at[0, offset:offset + 2 * N_HEADS].set(v8.reshape(-1).astype(F32))


def _expander(offset):
    e = np.zeros((LANES, W_GROUP), np.float32)
    for h in range(N_HEADS):
        e[offset + h, h * HEAD_DIM:(h + 1) * HEAD_DIM] = 1.0
    return jnp.asarray(e, BF16)


def _layer(x, p, lw, bsz, seq):
    w_in = _reorder_in_cols(lw['w_in'].astype(BF16))
    u = _inproj(x, _row(lw['norm1']), w_in)

    cw_a, cb_a = lw['conv_a_w'].astype(F32), _row(lw['conv_a_b'])
    h_dir = None
    for d in (1, 0):
        h_dir = _rglru(u, bsz, seq, d == 1, cw_a, cb_a,
                       _block_diag(lw['lru_wr'][d]).astype(BF16), _row(lw['lru_br'][d]),
                       _block_diag(lw['lru_wi'][d]).astype(BF16), _row(lw['lru_bi'][d]),
                       _row(lw['lru_lambda'][d]), hb=h_dir)
    out_a = h_dir

    gconv = lw['gdn_conv'].astype(F32)
    cwq, cwk, cwv = gconv[:, :W_GROUP], gconv[:, W_GROUP:2 * W_GROUP], gconv[:, 2 * W_GROUP:]
    alog = _lane_pad(lw['gdn_a_log'], 2 * N_HEADS)
    dtb = _lane_pad(lw['gdn_dt_bias'], 2 * N_HEADS)
    gn = _row(jnp.tile(lw['gdn_norm'], N_HEADS))
    eb = jnp.stack([_expander(d * N_HEADS) for d in range(2)])
    eg = jnp.stack([_expander(2 * N_HEADS + d * N_HEADS) for d in range(2)])
    o_b_fwd, o_b_bwd = _gdn_scan(_gdn_prep(u, bsz, seq, cwq, cwk, cwv, alog, dtb, eb, eg), bsz, seq)

    half = HEAD_DIM // 2
    inv_freq = ROPE_BASE ** (-jnp.arange(half, dtype=F32) / half)
    ang = jnp.arange(seq, dtype=F32)[:, None] * inv_freq[None, :]
    cos = jnp.tile(jnp.cos(ang), (1, N_HEADS))
    sin = jnp.tile(jnp.sin(ang), (1, N_HEADS))
    log_gamma = jnp.log1p(-jnp.exp2(-lw['ret_decay'].astype(F32)))
    rn = _row(jnp.tile(lw['ret_norm'], N_HEADS))
    o_dir = None
    for d in (1, 0):
        lg = log_gamma[d]
        lgq = _row(jnp.tile(jnp.repeat(lg, half), 2))
        lgv = _row(jnp.repeat(lg, HEAD_DIM))
        if d == 1:
            o_dir = _ret(u, bsz, seq, True, cos, sin, lgq, lgv)
        else:
            o_dir = _ret(u, bsz, seq, False, cos, sin, lgq, lgv,
                         lgfc=_row(jnp.repeat(log_gamma[0], RET_CHUNK)), lgbc=_row(jnp.repeat(log_gamma[1], RET_CHUNK)),
                         rn=rn, ob=o_dir)
    out_c = o_dir

    qn, kn, vn = _na_prep(u, _row(jnp.tile(lw['na_qnorm'], N_HEADS)), _row(jnp.tile(lw['na_knorm'], N_HEADS)))
    out_d = _na(qn, kn, vn, _na_table(lw['na_rpb'].astype(F32)), bsz, seq)

    x = _outproj(x, u, out_a, o_b_fwd, o_b_bwd, gn, out_c, out_d, lw['w_out'].astype(BF16))
    return _ffn(x, p, bsz, seq, _row(lw['norm2']), lw['ffn_wg'].astype(BF16), lw['ffn_wu'].astype(BF16),
                lw['ffn_conv_w'].astype(F32), _row(lw['ffn_conv_b']), lw['ffn_wd'].astype(BF16),
                _row(lw['norm3']), lw['ple_gate'].astype(BF16), lw['ple_proj'].astype(BF16))


_WEIGHT_NAMES = ('norm1', 'norm2', 'norm3', 'w_in', 'w_out', 'conv_a_w', 'conv_a_b', 'lru_wr', 'lru_br',
                 'lru_wi', 'lru_bi', 'lru_lambda', 'gdn_conv', 'gdn_a_log', 'gdn_dt_bias', 'gdn_norm',
                 'ret_decay', 'ret_norm', 'na_qnorm', 'na_knorm', 'na_rpb', 'ffn_wg', 'ffn_wu',
                 'ffn_conv_w', 'ffn_conv_b', 'ffn_wd', 'ple_proj', 'ple_gate')


def _trunk(x, p, weights):
    bsz, seq, _ = x.shape
    xf = x.reshape(bsz * seq, D_MODEL)
    for i in range(DEPTH):
        lw = {n: w[i] for n, w in weights.items()}
        xf = _layer(xf, p[i].reshape(bsz * seq, PLE_DIM), lw, bsz, seq)
    return xf.reshape(bsz, seq, D_MODEL)


def kernel(x_prompt, x_sample, p_prompt, p_sample, norm1, norm2, norm3, w_in, w_out, conv_a_w, conv_a_b, lru_wr, lru_br, lru_wi, lru_bi, lru_lambda, gdn_conv, gdn_a_log, gdn_dt_bias, gdn_norm, ret_decay, ret_norm, na_qnorm, na_knorm, na_rpb, ffn_wg, ffn_wu, ffn_conv_w, ffn_conv_b, ffn_wd, ple_proj, ple_gate):
    weights = dict(zip(_WEIGHT_NAMES, (norm1, norm2, norm3, w_in, w_out, conv_a_w, conv_a_b, lru_wr, lru_br,
                                       lru_wi, lru_bi, lru_lambda, gdn_conv, gdn_a_log, gdn_dt_bias, gdn_norm,
                                       ret_decay, ret_norm, na_qnorm, na_knorm, na_rpb, ffn_wg, ffn_wu,
                                       ffn_conv_w, ffn_conv_b, ffn_wd, ple_proj, ple_gate)))
    return (_trunk(x_prompt, p_prompt, weights), _trunk(x_sample, p_sample, weights))
```

```python
import functools

import jax
import jax.numpy as jnp
import numpy as np
from jax import lax
from jax.experimental import pallas as pl
from jax.experimental.pallas import tpu as pltpu

F32 = jnp.float32
BF16 = jnp.bfloat16

D_MODEL = 1024
DEPTH = 2
PLE_DIM = 256
GRID_W = 64
W_GROUP = 256
HEAD_DIM = 64
N_HEADS = 4
LRU_C = 8.0
CHUNK = 64
ROPE_BASE = 10000.0
NA_KH = 8
NA_KW = 16
D_FF = 2816
EPS = 1e-6
NEG_BIG = -1e30

SUBLANES = 8
LANES = 128
VMEM_LIMIT = 56 * 1024 * 1024

CB_XA, CB_YA, CB_QB, CB_KB, CB_VB, CB_ZB, CB_QC, CB_KC, CB_VC, CB_GC = range(10)
CB128_AB = 20
U_COLS = 10 * W_GROUP + LANES
IN_COLS_PAD = U_COLS + 3 * W_GROUP
IN_COLS = 13 * W_GROUP + 4 * N_HEADS


def _reorder_in_cols(w):
    rows = w.shape[0]
    half = HEAD_DIM // 2

    def halves_first(cols):
        return cols.reshape(rows, N_HEADS, 2, half).transpose(0, 2, 1, 3).reshape(rows, W_GROUP)

    ab0 = 6 * W_GROUP
    qc0 = ab0 + 4 * N_HEADS
    qd0 = qc0 + 4 * W_GROUP
    parts = [w[:, :ab0], halves_first(w[:, qc0:qc0 + W_GROUP]), halves_first(w[:, qc0 + W_GROUP:qc0 + 2 * W_GROUP]),
             w[:, qc0 + 2 * W_GROUP:qd0], w[:, ab0:qc0], jnp.zeros((rows, LANES - 4 * N_HEADS), w.dtype), w[:, qd0:]]
    out = jnp.concatenate(parts, axis=1)
    assert out.shape[1] == IN_COLS_PAD
    return out


def _sigmoid(x):
    return 1.0 / (1.0 + jnp.exp(-x))


def _silu(x):
    return x * _sigmoid(x)


def _softplus(x):
    return jnp.maximum(x, 0.0) + jnp.log1p(jnp.exp(-jnp.abs(x)))


def _gelu_tanh(x):
    return x * (0.5 * (1.0 + jnp.tanh(0.7978845608028654 * (x + 0.044715 * (x * x * x)))))


def _rms(x, gain):
    ms = jnp.mean(x * x, axis=-1, keepdims=True)
    return x * lax.rsqrt(ms + EPS) * gain


def _dot(a, b):
    return jnp.dot(a, b, preferred_element_type=F32)


def _split_bf16(x, terms):
    parts = []
    for _ in range(terms):
        p = x.astype(BF16)
        parts.append(p)
        x = x - p.astype(F32)
    return parts


def _dot_sel(x, sel, terms):
    return sum(_dot(p, sel) for p in _split_bf16(x, terms))


def _sel_dot(sel, x, terms):
    return sum(_dot(sel, p) for p in _split_bf16(x, terms))


def _dot_nt(a, b):
    return lax.dot_general(a, b, (((1,), (1,)), ((), ())), preferred_element_type=F32)


def _dot_tn(a, b):
    return lax.dot_general(a, b, (((0,), (0,)), ((), ())), preferred_element_type=F32)


def _iota(shape, dim):
    return lax.broadcasted_iota(jnp.int32, shape, dim)


def _head_ones():
    return (_iota((W_GROUP, W_GROUP), 0) // HEAD_DIM == _iota((W_GROUP, W_GROUP), 1) // HEAD_DIM).astype(BF16)


def _stack4(x, mask):
    return jnp.where(mask, jnp.concatenate([x] * N_HEADS, axis=0), jnp.zeros((), x.dtype))


def _conv4(xm, xp, xn, w_ref, first, last):
    ts = xm.shape[0]
    xp = jnp.where(first, 0.0, xp)
    xn = jnp.where(last, 0.0, xn)
    xe = jnp.concatenate([xp, xm, xn], axis=0)
    out = w_ref[0:1, :] * xe[6:6 + ts]
    out = out + w_ref[1:2, :] * xe[7:7 + ts]
    out = out + w_ref[2:3, :] * xe[8:8 + ts]
    out = out + w_ref[3:4, :] * xe[9:9 + ts]
    return out


def _time_block(reverse):
    i = pl.program_id(1)
    n = pl.num_programs(1)
    tb = (n - 1 - i) if reverse else i
    return tb, tb == 0, tb == n - 1


def _row_specs(ts, ns, total_rows, col_block, reverse, width=W_GROUP):
    r8 = ts // SUBLANES
    last8 = total_rows // SUBLANES - 1

    def tb(i):
        return (ns - 1 - i) if reverse else i

    main = pl.BlockSpec((ts, width), lambda b, i: (b * ns + tb(i), col_block))
    prev = pl.BlockSpec((SUBLANES, width), lambda b, i: (jnp.maximum((b * ns + tb(i)) * r8 - 1, 0), col_block))
    nxt = pl.BlockSpec((SUBLANES, width), lambda b, i: (jnp.minimum((b * ns + tb(i) + 1) * r8, last8), col_block))
    return main, prev, nxt


def _const_spec(shape):
    nd = len(shape)
    return pl.BlockSpec(shape, lambda *_: (0,) * nd, pipeline_mode=pl.Buffered(1))


def _params(sem):
    return pltpu.CompilerParams(dimension_semantics=sem, vmem_limit_bytes=VMEM_LIMIT)


def _inproj_kernel(x_ref, g_ref, w_ref, gq_ref, gk_ref, u_ref, qd_ref, kd_ref, vd_ref):
    h = _rms(x_ref[...], g_ref[...]).astype(BF16)
    u_ref[...] = _dot(h, w_ref[:, :U_COLS])
    qkv = _dot(h, w_ref[:, U_COLS:])
    q, k = qkv[:, :W_GROUP], qkv[:, W_GROUP:2 * W_GROUP]
    ones_h = _head_ones()
    inv = 1.0 / HEAD_DIM
    qd_ref[...] = (q * lax.rsqrt(_dot_sel(q * q, ones_h, 2) * inv + EPS) * gq_ref[...] * (HEAD_DIM ** -0.5)).astype(BF16)
    kd_ref[...] = (k * lax.rsqrt(_dot_sel(k * k, ones_h, 2) * inv + EPS) * gk_ref[...]).astype(BF16)
    vd_ref[...] = qkv[:, 2 * W_GROUP:].astype(BF16)


def _inproj(x, gain, w, gq, gk):
    t = x.shape[0]
    tm = 512
    grp = pl.BlockSpec((tm, W_GROUP), lambda i: (i, 0))
    grp_shape = jax.ShapeDtypeStruct((t, W_GROUP), BF16)
    return pl.pallas_call(
        _inproj_kernel,
        grid=(t // tm,),
        in_specs=[pl.BlockSpec((tm, D_MODEL), lambda i: (i, 0)),
                  _const_spec((1, D_MODEL)),
                  _const_spec((D_MODEL, IN_COLS_PAD)),
                  _const_spec((1, W_GROUP)), _const_spec((1, W_GROUP))],
        out_specs=[pl.BlockSpec((tm, U_COLS), lambda i: (i, 0)), grp, grp, grp],
        out_shape=[jax.ShapeDtypeStruct((t, U_COLS), F32), grp_shape, grp_shape, grp_shape],
        compiler_params=_params(("parallel",)),
        name="inproj",
    )(x, gain, w, gq, gk)


def _rglru_kernel(reverse, *refs):
    if reverse:
        (xm_ref, xp_ref, xn_ref, cw_ref, cb_ref, wr_ref, br_ref, wi_ref, bi_ref, lam_ref,
         o_ref, a_s, b_s, p_s, ac_s, c_s, carry, h_s) = refs
    else:
        (xm_ref, xp_ref, xn_ref, cw_ref, cb_ref, wr_ref, br_ref, wi_ref, bi_ref, lam_ref, ya_ref, hb_ref,
         o_ref, a_s, b_s, p_s, ac_s, c_s, carry, h_s) = refs
    ts = xm_ref.shape[0]
    ng = ts // SUBLANES
    tb, first, last = _time_block(reverse)

    @pl.when(pl.program_id(1) == 0)
    def _():
        carry[...] = jnp.zeros_like(carry)

    x = _conv4(xm_ref[...], xp_ref[...], xn_ref[...], cw_ref, first, last) + cb_ref[...]
    xb = x.astype(BF16)
    r = _sigmoid(_dot(xb, wr_ref[...]) + br_ref[...])
    ig = _sigmoid(_dot(xb, wi_ref[...]) + bi_ref[...])
    log_a = (-LRU_C) * r * _softplus(-lam_ref[...])
    a = jnp.exp(log_a)
    th = jnp.tanh(-log_a)
    bx = jnp.sqrt(2.0 * th / (1.0 + th)) * (ig * x)
    nl = W_GROUP // LANES
    for hf in range(nl):
        a_s[hf] = a[:, hf * LANES:(hf + 1) * LANES]
        b_s[hf] = bx[:, hf * LANES:(hf + 1) * LANES]

    def slab(ref, k):
        return jnp.concatenate([ref[hf, pl.ds(k, ng, stride=SUBLANES), :] for hf in range(nl)], axis=-1)

    order = list(range(SUBLANES))[::-1] if reverse else list(range(SUBLANES))
    p = None
    ac = None
    for k in order:
        a_k = slab(a_s, k)
        b_k = slab(b_s, k)
        if p is None:
            p, ac = b_k, a_k
        else:
            p = a_k * p + b_k
            ac = a_k * ac
        p_s[k] = p
        ac_s[k] = ac

    k_last = order[-1]

    def body(g, c):
        gi = (ng - 1 - g) if reverse else g
        c_s[pl.ds(gi, 1), :] = c
        return ac_s[k_last, pl.ds(gi, 1), :] * c + p_s[k_last, pl.ds(gi, 1), :]

    carry[...] = lax.fori_loop(0, ng, body, carry[...])
    c_in = c_s[...]
    for k in order:
        h_k = p_s[k] + ac_s[k] * c_in
        for hf in range(nl):
            h_s[hf, pl.ds(k, ng, stride=SUBLANES), :] = h_k[:, hf * LANES:(hf + 1) * LANES]
    h = jnp.concatenate([h_s[hf] for hf in range(nl)], axis=-1)
    if reverse:
        o_ref[...] = h
    else:
        o_ref[...] = ((h + hb_ref[...]) * _gelu_tanh(ya_ref[...])).astype(o_ref.dtype)


def _rglru(u, bsz, seq, reverse, cw, cb, wr, br, wi, bi, lam, hb=None):
    ts = 512
    ns = seq // ts
    t = bsz * seq
    ng = ts // SUBLANES
    xm, xp, xn = _row_specs(ts, ns, t, CB_XA, reverse)
    in_specs = [xm, xp, xn, _const_spec((4, W_GROUP)), _const_spec((1, W_GROUP)),
                _const_spec((W_GROUP, W_GROUP)), _const_spec((1, W_GROUP)),
                _const_spec((W_GROUP, W_GROUP)), _const_spec((1, W_GROUP)), _const_spec((1, W_GROUP))]
    args = [u, u, u, cw, cb, wr, br, wi, bi, lam]
    halves = (W_GROUP // LANES, ts, LANES)
    scratch = [pltpu.VMEM(halves, F32), pltpu.VMEM(halves, F32),
               pltpu.VMEM((SUBLANES, ng, W_GROUP), F32), pltpu.VMEM((SUBLANES, ng, W_GROUP), F32),
               pltpu.VMEM((ng, W_GROUP), F32), pltpu.VMEM((1, W_GROUP), F32), pltpu.VMEM(halves, F32)]
    if reverse:
        out_dtype = F32
        out_spec = pl.BlockSpec((ts, W_GROUP), lambda b, i: (b * ns + ns - 1 - i, 0))
    else:
        ya = _row_specs(ts, ns, t, CB_YA, False)[0]
        in_specs += [ya, pl.BlockSpec((ts, W_GROUP), lambda b, i: (b * ns + i, 0))]
        args += [u, hb]
        out_dtype = BF16
        out_spec = pl.BlockSpec((ts, W_GROUP), lambda b, i: (b * ns + i, 0))
    return pl.pallas_call(
        functools.partial(_rglru_kernel, reverse),
        grid=(bsz, ns),
        in_specs=in_specs,
        out_specs=out_spec,
        out_shape=jax.ShapeDtypeStruct((t, W_GROUP), out_dtype),
        scratch_shapes=scratch,
        compiler_params=_params(("parallel", "arbitrary")),
        name="rglru_bwd" if reverse else "rglru_fwd",
    )(*args)


GDN_BLOCK = 256


def _gdn_prep_kernel(qm, qp, qn, km, kp, kn, vm, vp, vn, ab_ref, cwq, cwk, cwv, alog_ref, dtb_ref, eb_ref, eg_ref,
                     *outs):
    ts = qm.shape[0]
    nc = ts // CHUNK
    c = CHUNK
    i = pl.program_id(1)
    first = i == 0
    last = i == pl.num_programs(1) - 1
    out_refs = (outs[0:5], outs[5:10])

    ones_h = _head_ones()
    q = _silu(_conv4(qm[...], qp[...], qn[...], cwq, first, last))
    k = _silu(_conv4(km[...], kp[...], kn[...], cwk, first, last))
    v = _silu(_conv4(vm[...], vp[...], vn[...], cwv, first, last))
    q = q * lax.rsqrt(_dot_sel(q * q, ones_h, 2) + EPS) * (HEAD_DIM ** -0.5)
    k = k * lax.rsqrt(_dot_sel(k * k, ones_h, 2) + EPS)
    ab = ab_ref[...]
    sig_parts = _split_bf16(_sigmoid(ab), 2)
    g_parts = _split_bf16(-jnp.exp(alog_ref[...]) * _softplus(ab + dtb_ref[...]), 3)
    beta = [sum(_dot(p, eb_ref[d]) for p in sig_parts) for d in range(2)]
    gate = [sum(_dot(p, eg_ref[d]) for p in g_parts) for d in range(2)]

    ii = _iota((c, W_GROUP), 0)
    jj = _iota((c, W_GROUP), 1) % c
    r2 = _iota((c, c), 0)
    c2 = _iota((c, c), 1)
    incl = (jj <= ii, jj >= ii)
    strict = (jj < ii, jj > ii)
    tri = ((c2 <= r2).astype(BF16), (c2 >= r2).astype(BF16))
    ucat = ((ii > jj).astype(F32), (ii < jj).astype(F32))
    eye_cat = (ii == jj).astype(F32)
    blk = _iota((W_GROUP, W_GROUP), 0) // HEAD_DIM == _iota((W_GROUP, W_GROUP), 1) // HEAD_DIM

    sls = [slice(ci * c, (ci + 1) * c) for ci in range(nc)]
    kq = [_dot_nt(jnp.concatenate([k[sl], q[sl]], axis=0).astype(BF16), _stack4(k[sl].astype(BF16), blk))
          for sl in sls]
    units = [(d, ci) for ci in range(nc) for d in range(2)]
    gd = [_sel_dot(tri[d], jnp.concatenate([gate[d][sls[ci]], gate[d][sls[ci]] * ucat[d]], axis=1), 3)
          for d, ci in units]
    gcum = [x[:, :W_GROUP] for x in gd]
    decay = [jnp.where(incl[d], jnp.exp(jnp.where(incl[d], x[:, W_GROUP:], 0.0)), 0.0)
             for (d, ci), x in zip(units, gd)]
    lmat = [jnp.where(strict[d], beta[d][sls[ci]] * kq[ci][:c] * dec, 0.0) for (d, ci), dec in zip(units, decay)]
    attn = [kq[ci][c:] * dec for (d, ci), dec in zip(units, decay)]
    pinv = [eye_cat - l for l in lmat]
    lb = [l.astype(BF16) for l in lmat]
    lpow = [_dot(x, _stack4(x, blk)) for x in lb]
    for lvl in range(5):
        z = [_stack4(x.astype(BF16), blk) for x in lpow]
        if lvl < 4:
            r = [_dot(jnp.concatenate([p, lp], axis=0).astype(BF16), zz) for p, lp, zz in zip(pinv, lpow, z)]
            pinv = [p + x[:c] for p, x in zip(pinv, r)]
            lpow = [x[c:] for x in r]
        else:
            pinv = [p + _dot(p.astype(BF16), zz) for p, zz in zip(pinv, z)]
    egc = [jnp.exp(x) for x in gcum]
    pb = [p.astype(BF16) for p in pinv]
    for n, (d, ci) in enumerate(units):
        sl = sls[ci]
        u_ref, wq_ref, at_ref, kg_ref, ds_ref = out_refs[d]
        bc = beta[d][sl]
        u_ref[sl, :] = _dot(pb[n], _stack4((v[sl] * bc).astype(BF16), blk))
        w_key = _dot(pb[n], _stack4((k[sl] * bc * egc[n]).astype(BF16), blk))
        g_last = gcum[n][0:1, :] if d else gcum[n][c - 1:c, :]
        wq_ref[2 * ci * c:(2 * ci + 1) * c, :] = w_key.astype(BF16)
        wq_ref[(2 * ci + 1) * c:(2 * ci + 2) * c, :] = (q[sl] * egc[n]).astype(BF16)
        at_ref[sl, :] = attn[n].astype(BF16)
        kg_ref[sl, :] = (k[sl] * jnp.exp(g_last - gcum[n])).astype(BF16)
        ds_ref[ci] = jnp.exp(g_last)


def _gdn_prep(u, bsz, seq, cwq, cwk, cwv, alog, dtb, eb, eg):
    ts = GDN_BLOCK
    ns = seq // ts
    t = bsz * seq
    nc = ts // CHUNK
    specs = []
    for cb in (CB_QB, CB_KB, CB_VB):
        specs += list(_row_specs(ts, ns, t, cb, False))
    specs.append(pl.BlockSpec((ts, LANES), lambda b, i: (b * ns + i, CB128_AB)))
    specs += [_const_spec((4, W_GROUP))] * 3
    specs += [_const_spec((1, LANES))] * 2 + [_const_spec((2, LANES, W_GROUP))] * 2

    def row(b, i):
        return (b * ns + i, 0)

    one_dir_specs = [pl.BlockSpec((ts, W_GROUP), row), pl.BlockSpec((2 * ts, W_GROUP), row),
                     pl.BlockSpec((ts, W_GROUP), row), pl.BlockSpec((ts, W_GROUP), row),
                     pl.BlockSpec((nc, 1, W_GROUP), lambda b, i: (b * ns + i, 0, 0))]
    one_dir_shapes = [jax.ShapeDtypeStruct((t, W_GROUP), F32), jax.ShapeDtypeStruct((2 * t, W_GROUP), BF16),
                      jax.ShapeDtypeStruct((t, W_GROUP), BF16), jax.ShapeDtypeStruct((t, W_GROUP), BF16),
                      jax.ShapeDtypeStruct((t // CHUNK, 1, W_GROUP), F32)]
    return pl.pallas_call(
        _gdn_prep_kernel,
        grid=(bsz, ns),
        in_specs=specs,
        out_specs=one_dir_specs * 2,
        out_shape=one_dir_shapes * 2,
        compiler_params=_params(("parallel", "parallel")),
        name="gdn_prep",
    )(*([u] * 10 + [cwq, cwk, cwv, alog, dtb, eb, eg]))


def _gdn_scan_kernel(uf, wqf, atf, kgf, dsf, ub, wqb, atb, kgb, dsb, of_ref, ob_ref, s_ref):
    bsz, ts, _ = uf.shape
    nc = ts // CHUNK
    c = CHUNK

    @pl.when(pl.program_id(0) == 0)
    def _():
        s_ref[...] = jnp.zeros_like(s_ref)

    blk = _iota((W_GROUP, W_GROUP), 0) // HEAD_DIM == _iota((W_GROUP, W_GROUP), 1) // HEAD_DIM
    ins = ((uf, wqf, atf, kgf, dsf, of_ref), (ub, wqb, atb, kgb, dsb, ob_ref))
    chains = [(d, b) for b in range(bsz) for d in range(2)]
    state = [s_ref[d, b] for d, b in chains]
    for step in range(nc):
        pos = (step, nc - 1 - step)
        ws = [_dot(ins[d][1][b, 2 * pos[d] * c:(2 * pos[d] + 2) * c, :], s.astype(BF16))
              for (d, b), s in zip(chains, state)]
        vnb = [(ins[d][0][b, pos[d] * c:(pos[d] + 1) * c, :] - x[:c]).astype(BF16) for (d, b), x in zip(chains, ws)]
        for (d, b), x, vn in zip(chains, ws, vnb):
            sl = slice(pos[d] * c, (pos[d] + 1) * c)
            ins[d][5][b, sl, :] = x[c:] + _dot(ins[d][2][b, sl, :], _stack4(vn, blk))
        state = [s * ins[d][4][b, pos[d]] + jnp.where(blk, _dot_tn(ins[d][3][b, pos[d] * c:(pos[d] + 1) * c, :], vn), 0.0)
                 for (d, b), s, vn in zip(chains, state, vnb)]
    for (d, b), s in zip(chains, state):
        s_ref[d, b] = s


def _gdn_scan(prep, bsz, seq):
    ts = GDN_BLOCK
    ns = seq // ts
    nc = ts // CHUNK
    specs, args = [], []
    for d in range(2):
        tb = (lambda i: ns - 1 - i) if d else (lambda i: i)
        u_, wq, at, kg, ds = prep[5 * d:5 * d + 5]
        args += [u_.reshape(bsz, seq, W_GROUP), wq.reshape(bsz, 2 * seq, W_GROUP), at.reshape(bsz, seq, W_GROUP),
                 kg.reshape(bsz, seq, W_GROUP), ds.reshape(bsz, seq // CHUNK, 1, W_GROUP)]

        def blk3(rows, tb=tb):
            return pl.BlockSpec((bsz, rows, W_GROUP), lambda i: (0, tb(i), 0))

        specs += [blk3(ts), blk3(2 * ts), blk3(ts), blk3(ts),
                  pl.BlockSpec((bsz, nc, 1, W_GROUP), lambda i, tb=tb: (0, tb(i), 0, 0))]
    out_shape = jax.ShapeDtypeStruct((bsz, seq, W_GROUP), F32)
    o_fwd, o_bwd = pl.pallas_call(
        _gdn_scan_kernel,
        grid=(ns,),
        in_specs=specs,
        out_specs=[pl.BlockSpec((bsz, ts, W_GROUP), lambda i: (0, i, 0)),
                   pl.BlockSpec((bsz, ts, W_GROUP), lambda i: (0, ns - 1 - i, 0))],
        out_shape=[out_shape, out_shape],
        scratch_shapes=[pltpu.VMEM((2, bsz, W_GROUP, W_GROUP), F32)],
        compiler_params=_params(("arbitrary",)),
        name="gdn_scan",
    )(*args)
    return o_fwd.reshape(bsz * seq, W_GROUP), o_bwd.reshape(bsz * seq, W_GROUP)


RET_CHUNK = 128
RET_BLOCK = 512


def _ret_kernel(reverse, *refs):
    if reverse:
        (q_ref, k_ref, v_ref, cos_ref, sin_ref, lgq_ref, lgv_ref, o_ref, s_ref) = refs
    else:
        (q_ref, k_ref, v_ref, cos_ref, sin_ref, lgq_ref, lgv_ref, lgfc_ref, lgbc_ref, g_ref, ob_ref, rn_ref,
         o_ref, s_ref, dm_ref) = refs
    ts = q_ref.shape[0]
    cr = RET_CHUNK
    nc = ts // cr
    nk = N_HEADS * cr

    @pl.when(pl.program_id(1) == 0)
    def _():
        s_ref[...] = jnp.zeros_like(s_ref)
        if not reverse:
            ii = _iota((cr, nk), 0)
            jj = _iota((cr, nk), 1) % cr
            dist = (ii - jj).astype(F32)
            lower, upper = jj <= ii, jj >= ii
            dm_ref[...] = (jnp.where(lower, jnp.exp(jnp.where(lower, dist * lgfc_ref[...], 0.0)), 0.0)
                           + jnp.where(upper, jnp.exp(jnp.where(upper, -dist * lgbc_ref[...], 0.0)), 0.0))

    cos = cos_ref[...]
    sin = sin_ref[...]

    def rope(t):
        t1, t2 = t[:, :LANES], t[:, LANES:]
        return jnp.concatenate([t1 * cos - t2 * sin, t1 * sin + t2 * cos], axis=-1)

    q = rope(q_ref[...])
    k = rope(k_ref[...]) * (HEAD_DIM ** -0.5)
    v = v_ref[...]
    lgq = lgq_ref[...]
    lgv = lgv_ref[...]
    pos = _iota((cr, W_GROUP), 0).astype(F32)
    rows_qk = (_iota((W_GROUP, W_GROUP), 0) % LANES) // 32
    blk_state = rows_qk == _iota((W_GROUP, W_GROUP), 1) // HEAD_DIM
    if reverse:
        q_fac = jnp.exp((cr - pos) * lgq)
        k_fac = jnp.exp(pos * lgq)
    else:
        q_fac = jnp.exp((pos + 1.0) * lgq)
        k_fac = jnp.exp((cr - 1.0 - pos) * lgq)
    s_fac = jnp.exp(cr * lgv)
    sls = [slice(ci * cr, (ci + 1) * cr) for ci in range(nc)]
    kv = [jnp.where(blk_state, _dot_tn((k[sl] * k_fac).astype(BF16), v[sl].astype(BF16)), 0.0) for sl in sls]
    s = s_ref[...]
    cross = [None] * nc
    for ci in (range(nc - 1, -1, -1) if reverse else range(nc)):
        cross[ci] = _dot((q[sls[ci]] * q_fac).astype(BF16), s.astype(BF16))
        s = s * s_fac + kv[ci]
    s_ref[...] = s
    if reverse:
        o_ref[...] = jnp.concatenate(cross, axis=0)
        return
    stack_head = _iota((nk, W_GROUP), 0) // cr
    k_mask = stack_head == (_iota((nk, W_GROUP), 1) % LANES) // 32
    v_mask = stack_head == _iota((nk, W_GROUP), 1) // HEAD_DIM
    qk = [_dot_nt(q[sl].astype(BF16), _stack4(k[sl].astype(BF16), k_mask)) for sl in sls]
    dm = dm_ref[...]
    intra = [_dot((x * dm).astype(BF16), _stack4(v[sl].astype(BF16), v_mask)) for x, sl in zip(qk, sls)]
    o = jnp.concatenate([a + b for a, b in zip(intra, cross)], axis=0) + ob_ref[...]
    ones_h = _head_ones()
    oc = o - _dot_sel(o, ones_h, 3) * (1.0 / HEAD_DIM)
    var = _dot_sel(oc * oc, ones_h, 2) * (1.0 / HEAD_DIM)
    o_ref[...] = (oc * lax.rsqrt(var + EPS) * rn_ref[...] * _silu(g_ref[...])).astype(o_ref.dtype)


def _ret(u, bsz, seq, reverse, cos, sin, lgq, lgv, lgfc=None, lgbc=None, rn=None, ob=None):
    cr = RET_CHUNK
    ts = RET_BLOCK
    ns = seq // ts
    t = bsz * seq
    tbf = (lambda i: ns - 1 - i) if reverse else (lambda i: i)

    def col(cb):
        return pl.BlockSpec((ts, W_GROUP), lambda b, i: (b * ns + tbf(i), cb))

    tab = pl.BlockSpec((ts, LANES), lambda b, i: (tbf(i), 0))
    specs = [col(CB_QC), col(CB_KC), col(CB_VC), tab, tab, _const_spec((1, W_GROUP)), _const_spec((1, W_GROUP))]
    args = [u, u, u, cos, sin, lgq, lgv]
    scratch = [pltpu.VMEM((W_GROUP, W_GROUP), F32)]
    if reverse:
        out_dtype = F32
    else:
        specs += [_const_spec((1, N_HEADS * cr)), _const_spec((1, N_HEADS * cr)), col(CB_GC),
                  pl.BlockSpec((ts, W_GROUP), lambda b, i: (b * ns + i, 0)), _const_spec((1, W_GROUP))]
        args += [lgfc, lgbc, u, ob, rn]
        scratch.append(pltpu.VMEM((cr, N_HEADS * cr), F32))
        out_dtype = BF16
    return pl.pallas_call(
        functools.partial(_ret_kernel, reverse),
        grid=(bsz, ns),
        in_specs=specs,
        out_specs=pl.BlockSpec((ts, W_GROUP), lambda b, i: (b * ns + tbf(i), 0)),
        out_shape=jax.ShapeDtypeStruct((t, W_GROUP), out_dtype),
        scratch_shapes=scratch,
        compiler_params=_params(("parallel", "arbitrary")),
        name="ret_bwd" if reverse else "ret_fwd",
    )(*args)


NA_GROUP = 8
NA_WIN = NA_KH * GRID_W
NA_ROWS_PER_PASS = 4


def _na_kernel(rows, q_ref, kp_ref, kc_ref, kn_ref, vp_ref, vc_ref, vn_ref, tab_ref, o_ref, kw, vw):
    g = pl.program_id(1)
    blk = NA_GROUP * GRID_W
    kw[0:blk, :] = kp_ref[...]
    kw[blk:2 * blk, :] = kc_ref[...]
    kw[2 * blk:3 * blk, :] = kn_ref[...]
    vw[0:blk, :] = vp_ref[...]
    vw[blk:2 * blk, :] = vc_ref[...]
    vw[2 * blk:3 * blk, :] = vn_ref[...]
    row_head = _iota((N_HEADS * GRID_W, W_GROUP), 0) // GRID_W
    lane_head = _iota((N_HEADS * GRID_W, W_GROUP), 1) // HEAD_DIM
    blk = row_head == lane_head
    out_head = _iota((GRID_W, W_GROUP), 1) // HEAD_DIM
    for j0 in range(0, NA_GROUP, NA_ROWS_PER_PASS):
        js = list(range(j0, j0 + NA_ROWS_PER_PASS))
        vwins, sc = [], []
        for j in js:
            r = g * NA_GROUP + j
            r0 = jnp.clip(r - NA_KH // 2, 0, rows - NA_KH)
            off = pl.multiple_of((r0 - g * NA_GROUP + NA_GROUP) * GRID_W, GRID_W)
            qst = _stack4(q_ref[j * GRID_W:(j + 1) * GRID_W, :], blk)
            vwins.append(vw[pl.ds(off, NA_WIN), :])
            sc.append(_dot_nt(qst, kw[pl.ds(off, NA_WIN), :]) + tab_ref[r - r0])
        mx = [jnp.max(s, axis=-1, keepdims=True) for s in sc]
        pr = [jnp.exp(s - m) for s, m in zip(sc, mx)]
        den = [jnp.sum(p, axis=-1, keepdims=True) for p in pr]
        pv = [_dot(p.astype(BF16), vw_) / l for p, vw_, l in zip(pr, vwins, den)]
        for j, o in zip(js, pv):
            res = o[0:GRID_W]
            for h in range(1, N_HEADS):
                res = jnp.where(out_head == h, o[h * GRID_W:(h + 1) * GRID_W], res)
            o_ref[j * GRID_W:(j + 1) * GRID_W, :] = res.astype(o_ref.dtype)


def _na(qn, kn, vn, table, bsz, seq):
    rows = seq // GRID_W
    ng = rows // NA_GROUP
    blk = NA_GROUP * GRID_W
    t = bsz * seq
    cur = pl.BlockSpec((blk, W_GROUP), lambda b, g: (b * ng + g, 0))
    prev = pl.BlockSpec((blk, W_GROUP), lambda b, g: (b * ng + jnp.maximum(g - 1, 0), 0))
    nxt = pl.BlockSpec((blk, W_GROUP), lambda b, g: (b * ng + jnp.minimum(g + 1, ng - 1), 0))
    return pl.pallas_call(
        functools.partial(_na_kernel, rows),
        grid=(bsz, ng),
        in_specs=[cur, prev, cur, nxt, prev, cur, nxt, _const_spec((NA_KH, N_HEADS * GRID_W, NA_WIN))],
        out_specs=cur,
        out_shape=jax.ShapeDtypeStruct((t, W_GROUP), BF16),
        scratch_shapes=[pltpu.VMEM((3 * blk, W_GROUP), BF16), pltpu.VMEM((3 * blk, W_GROUP), BF16)],
        compiler_params=_params(("parallel", "parallel")),
        name="na",
    )(qn, kn, kn, kn, vn, vn, vn, table)


def _na_table(rpb):
    didx = np.arange(NA_KH)[:, None]
    kk = np.arange(NA_KH)[None, :]
    dr = (kk + (NA_KH - 1) - didx).reshape(-1)
    cc = np.arange(GRID_W)[:, None]
    kc = np.arange(GRID_W)[None, :]
    c0 = np.clip(cc - NA_KW // 2, 0, GRID_W - NA_KW)
    valid = (kc >= c0) & (kc < c0 + NA_KW)
    dc = np.clip(kc - cc + (NA_KW - 1), 0, 2 * NA_KW - 2).reshape(-1)
    row_sel = np.zeros((dr.size, 2 * NA_KH - 1), np.float32)
    row_sel[np.arange(dr.size), dr] = 1.0
    col_sel = np.zeros((2 * NA_KW - 1, dc.size), np.float32)
    col_sel[dc, np.arange(dc.size)] = 1.0
    bias = jnp.einsum('ar,hrs,sb->hab', row_sel, rpb, col_sel, precision=lax.Precision.HIGHEST)
    bias = bias.reshape(N_HEADS, NA_KH, NA_KH, GRID_W, GRID_W)
    tab = jnp.where(valid[None, None, None], bias, NEG_BIG)
    return jnp.transpose(tab, (1, 0, 3, 2, 4)).reshape(NA_KH, N_HEADS * GRID_W, NA_WIN)


def _outproj_kernel(x_ref, a_ref, bf_ref, bb_ref, z_ref, gn_ref, c_ref, d_ref, w_ref, o_ref):
    ob = bf_ref[...] + bb_ref[...]
    ms = _dot_sel(ob * ob, _head_ones(), 2) * (1.0 / HEAD_DIM)
    out_b = (ob * lax.rsqrt(ms + EPS) * gn_ref[...] * _silu(z_ref[...])).astype(BF16)
    acc = x_ref[...]
    for gi, m in enumerate((a_ref[...], out_b, c_ref[...], d_ref[...])):
        acc = acc + _dot(m, w_ref[gi * W_GROUP:(gi + 1) * W_GROUP, :])
    o_ref[...] = acc


def _outproj(x, u, oa, obf, obb, gn, oc, od, w):
    t = x.shape[0]
    tm = 512
    xs = pl.BlockSpec((tm, D_MODEL), lambda i: (i, 0))
    ms = pl.BlockSpec((tm, W_GROUP), lambda i: (i, 0))
    zs = pl.BlockSpec((tm, W_GROUP), lambda i: (i, CB_ZB))
    return pl.pallas_call(
        _outproj_kernel,
        grid=(t // tm,),
        in_specs=[xs, ms, ms, ms, zs, _const_spec((1, W_GROUP)), ms, ms, _const_spec((D_MODEL, D_MODEL))],
        out_specs=xs,
        out_shape=jax.ShapeDtypeStruct((t, D_MODEL), F32),
        compiler_params=_params(("parallel",)),
        name="outproj",
    )(x, oa, obf, obb, u, gn, oc, od, w)


FF_CHUNK = 256


def _ffn_kernel(xm_ref, xp_ref, xn_ref, p_ref, n2_ref, wg_ref, wu_ref, cw_ref, cb_ref, wd_ref,
                n3_ref, pg_ref, pp_ref, o_ref, act_s):
    tm = xm_ref.shape[0]
    i = pl.program_id(1)
    first = i == 0
    last = i == pl.num_programs(1) - 1
    x = xm_ref[...]
    n2 = n2_ref[...]
    xp = jnp.where(first, 0.0, xp_ref[...])
    xn = jnp.where(last, 0.0, xn_ref[...])
    h2f = _rms(x, n2)
    h2 = h2f.astype(BF16)
    h2e = jnp.concatenate([_rms(xp, n2), h2f, _rms(xn, n2)], axis=0).astype(BF16)
    for f in range(D_FF // FF_CHUNK):
        fs = slice(f * FF_CHUNK, (f + 1) * FF_CHUNK)
        ge = _dot(h2e, wg_ref[:, fs])
        gate = (cw_ref[0:1, fs] * ge[7:7 + tm] + cw_ref[1:2, fs] * ge[8:8 + tm]
                + cw_ref[2:3, fs] * ge[9:9 + tm] + cb_ref[:, fs])
        up = _dot(h2, wu_ref[:, fs])
        act_s[:, fs] = (_gelu_tanh(gate) * up).astype(BF16)
    acc = x + _dot(act_s[...], wd_ref[...])
    h3 = _rms(acc, n3_ref[...]).astype(BF16)
    gate3 = _sigmoid(_dot(h3, pg_ref[...]))
    o_ref[...] = acc + gate3 * _dot(p_ref[...].astype(BF16), pp_ref[...])


def _ffn(x, p, bsz, seq, n2, wg, wu, cw, cb, wd, n3, pg, pp):
    tm = 512
    ns = seq // tm
    t = bsz * seq
    xm, xp, xn = _row_specs(tm, ns, t, 0, False, width=D_MODEL)
    return pl.pallas_call(
        _ffn_kernel,
        grid=(bsz, ns),
        in_specs=[xm, xp, xn, pl.BlockSpec((tm, PLE_DIM), lambda b, i: (b * ns + i, 0)),
                  _const_spec((1, D_MODEL)), _const_spec((D_MODEL, D_FF)), _const_spec((D_MODEL, D_FF)),
                  _const_spec((3, D_FF)), _const_spec((1, D_FF)), _const_spec((D_FF, D_MODEL)),
                  _const_spec((1, D_MODEL)), _const_spec((D_MODEL, D_MODEL)), _const_spec((PLE_DIM, D_MODEL))],
        out_specs=pl.BlockSpec((tm, D_MODEL), lambda b, i: (b * ns + i, 0)),
        out_shape=jax.ShapeDtypeStruct((t, D_MODEL), F32),
        compiler_params=_params(("parallel", "parallel")),
        scratch_shapes=[pltpu.VMEM((tm, D_FF), BF16)],
        name="ffn_ple",
    )(x, x, x, p, n2, wg, wu, cw, cb, wd, n3, pg, pp)


def _block_diag(w):
    h, n, _ = w.shape
    eye = jnp.eye(h, dtype=w.dtype)
    return (eye[:, None, :, None] * w[:, :, None, :]).reshape(h * n, h * n)


def _row(v):
    return v.reshape(1, -1).astype(F32)


def _lane_pad(v8, offset):
    return jnp.zeros((1, LANES), F32).at[0, offset:offset + 2 * N_HEADS].set(v8.reshape(-1).astype(F32))


def _expander(offset):
    e = np.zeros((LANES, W_GROUP), np.float32)
    for h in range(N_HEADS):
        e[offset + h, h * HEAD_DIM:(h + 1) * HEAD_DIM] = 1.0
    return jnp.asarray(e, BF16)


def _layer(x, p, lw, bsz, seq):
    w_in = _reorder_in_cols(lw['w_in'].astype(BF16))
    u, qn, kn, vn = _inproj(x, _row(lw['norm1']), w_in,
                            _row(jnp.tile(lw['na_qnorm'], N_HEADS)), _row(jnp.tile(lw['na_knorm'], N_HEADS)))

    cw_a, cb_a = lw['conv_a_w'].astype(F32), _row(lw['conv_a_b'])
    h_dir = None
    for d in (1, 0):
        h_dir = _rglru(u, bsz, seq, d == 1, cw_a, cb_a,
                       _block_diag(lw['lru_wr'][d]).astype(BF16), _row(lw['lru_br'][d]),
                       _block_diag(lw['lru_wi'][d]).astype(BF16), _row(lw['lru_bi'][d]),
                       _row(lw['lru_lambda'][d]), hb=h_dir)
    out_a = h_dir

    gconv = lw['gdn_conv'].astype(F32)
    cwq, cwk, cwv = gconv[:, :W_GROUP], gconv[:, W_GROUP:2 * W_GROUP], gconv[:, 2 * W_GROUP:]
    alog = _lane_pad(lw['gdn_a_log'], 2 * N_HEADS)
    dtb = _lane_pad(lw['gdn_dt_bias'], 2 * N_HEADS)
    gn = _row(jnp.tile(lw['gdn_norm'], N_HEADS))
    eb = jnp.stack([_expander(d * N_HEADS) for d in range(2)])
    eg = jnp.stack([_expander(2 * N_HEADS + d * N_HEADS) for d in range(2)])
    o_b_fwd, o_b_bwd = _gdn_scan(_gdn_prep(u, bsz, seq, cwq, cwk, cwv, alog, dtb, eb, eg), bsz, seq)

    half = HEAD_DIM // 2
    inv_freq = ROPE_BASE ** (-jnp.arange(half, dtype=F32) / half)
    ang = jnp.arange(seq, dtype=F32)[:, None] * inv_freq[None, :]
    cos = jnp.tile(jnp.cos(ang), (1, N_HEADS))
    sin = jnp.tile(jnp.sin(ang), (1, N_HEADS))
    log_gamma = jnp.log1p(-jnp.exp2(-lw['ret_decay'].astype(F32)))
    rn = _row(jnp.tile(lw['ret_norm'], N_HEADS))
    o_dir = None
    for d in (1, 0):
        lg = log_gamma[d]
        lgq = _row(jnp.tile(jnp.repeat(lg, half), 2))
        lgv = _row(jnp.repeat(lg, HEAD_DIM))
        if d == 1:
            o_dir = _ret(u, bsz, seq, True, cos, sin, lgq, lgv)
        else:
            o_dir = _ret(u, bsz, seq, False, cos, sin, lgq, lgv,
                         lgfc=_row(jnp.repeat(log_gamma[0], RET_CHUNK)), lgbc=_row(jnp.repeat(log_gamma[1], RET_CHUNK)),
                         rn=rn, ob=o_dir)
    out_c = o_dir

    out_d = _na(qn, kn, vn, _na_table(lw['na_rpb'].astype(F32)), bsz, seq)

    x = _outproj(x, u, out_a, o_b_fwd, o_b_bwd, gn, out_c, out_d, lw['w_out'].astype(BF16))
    return _ffn(x, p, bsz, seq, _row(lw['norm2']), lw['ffn_wg'].astype(BF16), lw['ffn_wu'].astype(BF16),
                lw['ffn_conv_w'].astype(F32), _row(lw['ffn_conv_b']), lw['ffn_wd'].astype(BF16),
                _row(lw['norm3']), lw['ple_gate'].astype(BF16), lw['ple_proj'].astype(BF16))


_WEIGHT_NAMES = ('norm1', 'norm2', 'norm3', 'w_in', 'w_out', 'conv_a_w', 'conv_a_b', 'lru_wr', 'lru_br',
                 'lru_wi', 'lru_bi', 'lru_lambda', 'gdn_conv', 'gdn_a_log', 'gdn_dt_bias', 'gdn_norm',
                 'ret_decay', 'ret_norm', 'na_qnorm', 'na_knorm', 'na_rpb', 'ffn_wg', 'ffn_wu',
                 'ffn_conv_w', 'ffn_conv_b', 'ffn_wd', 'ple_proj', 'ple_gate')


def _trunk(x, p, weights):
    bsz, seq, _ = x.shape
    xf = x.reshape(bsz * seq, D_MODEL)
    for i in range(DEPTH):
        lw = {n: w[i] for n, w in weights.items()}
        xf = _layer(xf, p[i].reshape(bsz * seq, PLE_DIM), lw, bsz, seq)
    return xf.reshape(bsz, seq, D_MODEL)


def kernel(x_prompt, x_sample, p_prompt, p_sample, norm1, norm2, norm3, w_in, w_out, conv_a_w, conv_a_b, lru_wr, lru_br, lru_wi, lru_bi, lru_lambda, gdn_conv, gdn_a_log, gdn_dt_bias, gdn_norm, ret_decay, ret_norm, na_qnorm, na_knorm, na_rpb, ffn_wg, ffn_wu, ffn_conv_w, ffn_conv_b, ffn_wd, ple_proj, ple_gate):
    weights = dict(zip(_WEIGHT_NAMES, (norm1, norm2, norm3, w_in, w_out, conv_a_w, conv_a_b, lru_wr, lru_br,
                                       lru_wi, lru_bi, lru_lambda, gdn_conv, gdn_a_log, gdn_dt_bias, gdn_norm,
                                       ret_decay, ret_norm, na_qnorm, na_knorm, na_rpb, ffn_wg, ffn_wu,
                                       ffn_conv_w, ffn_conv_b, ffn_wd, ple_proj, ple_gate)))
    return (_trunk(x_prompt, p_prompt, weights), _trunk(x_sample, p_sample, weights))
```

```python
import functools

import jax
import jax.numpy as jnp
import numpy as np
from jax import lax
from jax.experimental import pallas as pl
from jax.experimental.pallas import tpu as pltpu

F32 = jnp.float32
BF16 = jnp.bfloat16

D_MODEL = 1024
DEPTH = 2
PLE_DIM = 256
GRID_W = 64
W_GROUP = 256
HEAD_DIM = 64
N_HEADS = 4
LRU_C = 8.0
CHUNK = 64
ROPE_BASE = 10000.0
NA_KH = 8
NA_KW = 16
D_FF = 2816
EPS = 1e-6
NEG_BIG = -1e30

SUBLANES = 8
LANES = 128
VMEM_LIMIT = 56 * 1024 * 1024

CB_XA, CB_YA, CB_QB, CB_KB, CB_VB, CB_ZB, CB_QC, CB_KC, CB_VC, CB_GC = range(10)
CB128_AB = 20
U_COLS = 10 * W_GROUP + LANES
IN_COLS_PAD = U_COLS + 3 * W_GROUP
IN_COLS = 13 * W_GROUP + 4 * N_HEADS


def _reorder_in_cols(w):
    rows = w.shape[0]
    half = HEAD_DIM // 2

    def halves_first(cols):
        return cols.reshape(rows, N_HEADS, 2, half).transpose(0, 2, 1, 3).reshape(rows, W_GROUP)

    ab0 = 6 * W_GROUP
    qc0 = ab0 + 4 * N_HEADS
    qd0 = qc0 + 4 * W_GROUP
    parts = [w[:, :ab0], halves_first(w[:, qc0:qc0 + W_GROUP]), halves_first(w[:, qc0 + W_GROUP:qc0 + 2 * W_GROUP]),
             w[:, qc0 + 2 * W_GROUP:qd0], w[:, ab0:qc0], jnp.zeros((rows, LANES - 4 * N_HEADS), w.dtype), w[:, qd0:]]
    out = jnp.concatenate(parts, axis=1)
    assert out.shape[1] == IN_COLS_PAD
    return out


def _sigmoid(x):
    return 1.0 / (1.0 + jnp.exp(-x))


def _silu(x):
    return x * _sigmoid(x)


def _softplus(x):
    return jnp.maximum(x, 0.0) + jnp.log1p(jnp.exp(-jnp.abs(x)))


def _gelu_tanh(x):
    return x * (0.5 * (1.0 + jnp.tanh(0.7978845608028654 * (x + 0.044715 * (x * x * x)))))


def _rms(x, gain):
    ms = jnp.mean(x * x, axis=-1, keepdims=True)
    return x * lax.rsqrt(ms + EPS) * gain


def _dot(a, b):
    return jnp.dot(a, b, preferred_element_type=F32)


def _split_bf16(x, terms):
    parts = []
    for _ in range(terms):
        p = x.astype(BF16)
        parts.append(p)
        x = x - p.astype(F32)
    return parts


def _dot_sel(x, sel, terms):
    return sum(_dot(p, sel) for p in _split_bf16(x, terms))


def _sel_dot(sel, x, terms):
    return sum(_dot(sel, p) for p in _split_bf16(x, terms))


def _dot_nt(a, b):
    return lax.dot_general(a, b, (((1,), (1,)), ((), ())), preferred_element_type=F32)


def _dot_tn(a, b):
    return lax.dot_general(a, b, (((0,), (0,)), ((), ())), preferred_element_type=F32)


def _iota(shape, dim):
    return lax.broadcasted_iota(jnp.int32, shape, dim)


def _head_ones():
    return (_iota((W_GROUP, W_GROUP), 0) // HEAD_DIM == _iota((W_GROUP, W_GROUP), 1) // HEAD_DIM).astype(BF16)


def _stack4(x, mask):
    return jnp.where(mask, jnp.concatenate([x] * N_HEADS, axis=0), jnp.zeros((), x.dtype))


def _conv4(xm, xp, xn, w_ref, first, last):
    ts = xm.shape[0]
    xp = jnp.where(first, 0.0, xp)
    xn = jnp.where(last, 0.0, xn)
    xe = jnp.concatenate([xp, xm, xn], axis=0)
    n = ts + 2 * SUBLANES

    def tap(offset):
        rolled = xe if offset == 0 else pltpu.roll(xe, (-offset) % n, 0)
        return rolled[SUBLANES:SUBLANES + ts]

    out = w_ref[0:1, :] * tap(-2)
    out = out + w_ref[1:2, :] * tap(-1)
    out = out + w_ref[2:3, :] * tap(0)
    out = out + w_ref[3:4, :] * tap(1)
    return out


def _time_block(reverse):
    i = pl.program_id(1)
    n = pl.num_programs(1)
    tb = (n - 1 - i) if reverse else i
    return tb, tb == 0, tb == n - 1


def _row_specs(ts, ns, total_rows, col_block, reverse, width=W_GROUP, halo=SUBLANES):
    per_block = ts // halo
    last = total_rows // halo - 1

    def tb(i):
        return (ns - 1 - i) if reverse else i

    main = pl.BlockSpec((ts, width), lambda b, i: (b * ns + tb(i), col_block))
    prev = pl.BlockSpec((halo, width), lambda b, i: (jnp.maximum((b * ns + tb(i)) * per_block - 1, 0), col_block))
    nxt = pl.BlockSpec((halo, width), lambda b, i: (jnp.minimum((b * ns + tb(i) + 1) * per_block, last), col_block))
    return main, prev, nxt


def _const_spec(shape):
    nd = len(shape)
    return pl.BlockSpec(shape, lambda *_: (0,) * nd, pipeline_mode=pl.Buffered(1))


def _params(sem):
    return pltpu.CompilerParams(dimension_semantics=sem, vmem_limit_bytes=VMEM_LIMIT)


def _inproj_kernel(x_ref, g_ref, w_ref, gq_ref, gk_ref, u_ref, qd_ref, kd_ref, vd_ref):
    h = _rms(x_ref[...], g_ref[...]).astype(BF16)
    u_ref[...] = _dot(h, w_ref[:, :U_COLS])
    qkv = _dot(h, w_ref[:, U_COLS:])
    q, k = qkv[:, :W_GROUP], qkv[:, W_GROUP:2 * W_GROUP]
    ones_h = _head_ones()
    inv = 1.0 / HEAD_DIM
    qd_ref[...] = (q * lax.rsqrt(_dot_sel(q * q, ones_h, 2) * inv + EPS) * gq_ref[...] * (HEAD_DIM ** -0.5)).astype(BF16)
    kd_ref[...] = (k * lax.rsqrt(_dot_sel(k * k, ones_h, 2) * inv + EPS) * gk_ref[...]).astype(BF16)
    vd_ref[...] = qkv[:, 2 * W_GROUP:].astype(BF16)


def _inproj(x, gain, w, gq, gk):
    t = x.shape[0]
    tm = 512
    grp = pl.BlockSpec((tm, W_GROUP), lambda i: (i, 0))
    grp_shape = jax.ShapeDtypeStruct((t, W_GROUP), BF16)
    return pl.pallas_call(
        _inproj_kernel,
        grid=(t // tm,),
        in_specs=[pl.BlockSpec((tm, D_MODEL), lambda i: (i, 0)),
                  _const_spec((1, D_MODEL)),
                  _const_spec((D_MODEL, IN_COLS_PAD)),
                  _const_spec((1, W_GROUP)), _const_spec((1, W_GROUP))],
        out_specs=[pl.BlockSpec((tm, U_COLS), lambda i: (i, 0)), grp, grp, grp],
        out_shape=[jax.ShapeDtypeStruct((t, U_COLS), F32), grp_shape, grp_shape, grp_shape],
        compiler_params=_params(("parallel",)),
        name="inproj",
    )(x, gain, w, gq, gk)


def _rglru_kernel(reverse, *refs):
    if reverse:
        (xm_ref, xp_ref, xn_ref, cw_ref, cb_ref, wr_ref, br_ref, wi_ref, bi_ref, lam_ref,
         o_ref, a_s, b_s, p_s, ac_s, c_s, carry, h_s) = refs
    else:
        (xm_ref, xp_ref, xn_ref, cw_ref, cb_ref, wr_ref, br_ref, wi_ref, bi_ref, lam_ref, ya_ref, hb_ref,
         o_ref, a_s, b_s, p_s, ac_s, c_s, carry, h_s) = refs
    ts = xm_ref.shape[0]
    ng = ts // SUBLANES
    tb, first, last = _time_block(reverse)

    @pl.when(pl.program_id(1) == 0)
    def _():
        carry[...] = jnp.zeros_like(carry)

    x = _conv4(xm_ref[...], xp_ref[...], xn_ref[...], cw_ref, first, last) + cb_ref[...]
    xb = x.astype(BF16)
    r = _sigmoid(_dot(xb, wr_ref[...]) + br_ref[...])
    ig = _sigmoid(_dot(xb, wi_ref[...]) + bi_ref[...])
    log_a = (-LRU_C) * r * _softplus(-lam_ref[...])
    a = jnp.exp(log_a)
    th = jnp.tanh(-log_a)
    bx = jnp.sqrt(2.0 * th / (1.0 + th)) * (ig * x)
    nl = W_GROUP // LANES
    for hf in range(nl):
        a_s[hf] = a[:, hf * LANES:(hf + 1) * LANES]
        b_s[hf] = bx[:, hf * LANES:(hf + 1) * LANES]

    def slab(ref, k):
        return jnp.concatenate([ref[hf, pl.ds(k, ng, stride=SUBLANES), :] for hf in range(nl)], axis=-1)

    order = list(range(SUBLANES))[::-1] if reverse else list(range(SUBLANES))
    p = None
    ac = None
    for k in order:
        a_k = slab(a_s, k)
        b_k = slab(b_s, k)
        if p is None:
            p, ac = b_k, a_k
        else:
            p = a_k * p + b_k
            ac = a_k * ac
        p_s[k] = p
        ac_s[k] = ac

    k_last = order[-1]

    def body(g, c):
        gi = (ng - 1 - g) if reverse else g
        c_s[pl.ds(gi, 1), :] = c
        return ac_s[k_last, pl.ds(gi, 1), :] * c + p_s[k_last, pl.ds(gi, 1), :]

    carry[...] = lax.fori_loop(0, ng, body, carry[...])
    c_in = c_s[...]
    for k in order:
        h_k = p_s[k] + ac_s[k] * c_in
        for hf in range(nl):
            h_s[hf, pl.ds(k, ng, stride=SUBLANES), :] = h_k[:, hf * LANES:(hf + 1) * LANES]
    h = jnp.concatenate([h_s[hf] for hf in range(nl)], axis=-1)
    if reverse:
        o_ref[...] = h
    else:
        o_ref[...] = ((h + hb_ref[...]) * _gelu_tanh(ya_ref[...])).astype(o_ref.dtype)


def _rglru(u, bsz, seq, reverse, cw, cb, wr, br, wi, bi, lam, hb=None):
    ts = 512
    ns = seq // ts
    t = bsz * seq
    ng = ts // SUBLANES
    xm, xp, xn = _row_specs(ts, ns, t, CB_XA, reverse)
    in_specs = [xm, xp, xn, _const_spec((4, W_GROUP)), _const_spec((1, W_GROUP)),
                _const_spec((W_GROUP, W_GROUP)), _const_spec((1, W_GROUP)),
                _const_spec((W_GROUP, W_GROUP)), _const_spec((1, W_GROUP)), _const_spec((1, W_GROUP))]
    args = [u, u, u, cw, cb, wr, br, wi, bi, lam]
    halves = (W_GROUP // LANES, ts, LANES)
    scratch = [pltpu.VMEM(halves, F32), pltpu.VMEM(halves, F32),
               pltpu.VMEM((SUBLANES, ng, W_GROUP), F32), pltpu.VMEM((SUBLANES, ng, W_GROUP), F32),
               pltpu.VMEM((ng, W_GROUP), F32), pltpu.VMEM((1, W_GROUP), F32), pltpu.VMEM(halves, F32)]
    if reverse:
        out_dtype = F32
        out_spec = pl.BlockSpec((ts, W_GROUP), lambda b, i: (b * ns + ns - 1 - i, 0))
    else:
        ya = _row_specs(ts, ns, t, CB_YA, False)[0]
        in_specs += [ya, pl.BlockSpec((ts, W_GROUP), lambda b, i: (b * ns + i, 0))]
        args += [u, hb]
        out_dtype = BF16
        out_spec = pl.BlockSpec((ts, W_GROUP), lambda b, i: (b * ns + i, 0))
    return pl.pallas_call(
        functools.partial(_rglru_kernel, reverse),
        grid=(bsz, ns),
        in_specs=in_specs,
        out_specs=out_spec,
        out_shape=jax.ShapeDtypeStruct((t, W_GROUP), out_dtype),
        scratch_shapes=scratch,
        compiler_params=_params(("parallel", "arbitrary")),
        name="rglru_bwd" if reverse else "rglru_fwd",
    )(*args)


GDN_BLOCK = 256


def _gdn_prep_kernel(qm, qp, qn, km, kp, kn, vm, vp, vn, ab_ref, cwq, cwk, cwv, alog_ref, dtb_ref, eb_ref, eg_ref,
                     *outs):
    ts = qm.shape[0]
    nc = ts // CHUNK
    c = CHUNK
    i = pl.program_id(1)
    first = i == 0
    last = i == pl.num_programs(1) - 1
    out_refs = (outs[0:5], outs[5:10])

    ones_h = _head_ones()
    q = _silu(_conv4(qm[...], qp[...], qn[...], cwq, first, last))
    k = _silu(_conv4(km[...], kp[...], kn[...], cwk, first, last))
    v = _silu(_conv4(vm[...], vp[...], vn[...], cwv, first, last))
    q = q * lax.rsqrt(_dot_sel(q * q, ones_h, 2) + EPS) * (HEAD_DIM ** -0.5)
    k = k * lax.rsqrt(_dot_sel(k * k, ones_h, 2) + EPS)
    ab = ab_ref[...]
    sig_parts = _split_bf16(_sigmoid(ab), 2)
    g_parts = _split_bf16(-jnp.exp(alog_ref[...]) * _softplus(ab + dtb_ref[...]), 3)
    beta = [sum(_dot(p, eb_ref[d]) for p in sig_parts) for d in range(2)]
    gate = [sum(_dot(p, eg_ref[d]) for p in g_parts) for d in range(2)]

    ii = _iota((c, W_GROUP), 0)
    jj = _iota((c, W_GROUP), 1) % c
    r2 = _iota((c, c), 0)
    c2 = _iota((c, c), 1)
    incl = (jj <= ii, jj >= ii)
    strict = (jj < ii, jj > ii)
    tri = ((c2 <= r2).astype(BF16), (c2 >= r2).astype(BF16))
    ucat = ((ii > jj).astype(F32), (ii < jj).astype(F32))
    eye_cat = (ii == jj).astype(F32)
    blk = _iota((W_GROUP, W_GROUP), 0) // HEAD_DIM == _iota((W_GROUP, W_GROUP), 1) // HEAD_DIM

    sls = [slice(ci * c, (ci + 1) * c) for ci in range(nc)]
    kq = [_dot_nt(jnp.concatenate([k[sl], q[sl]], axis=0).astype(BF16), _stack4(k[sl].astype(BF16), blk))
          for sl in sls]
    units = [(d, ci) for ci in range(nc) for d in range(2)]
    gd = [_sel_dot(tri[d], jnp.concatenate([gate[d][sls[ci]], gate[d][sls[ci]] * ucat[d]], axis=1), 3)
          for d, ci in units]
    gcum = [x[:, :W_GROUP] for x in gd]
    decay = [jnp.where(incl[d], jnp.exp(jnp.where(incl[d], x[:, W_GROUP:], 0.0)), 0.0)
             for (d, ci), x in zip(units, gd)]
    lmat = [jnp.where(strict[d], beta[d][sls[ci]] * kq[ci][:c] * dec, 0.0) for (d, ci), dec in zip(units, decay)]
    attn = [kq[ci][c:] * dec for (d, ci), dec in zip(units, decay)]
    pinv = [eye_cat - l for l in lmat]
    lb = [l.astype(BF16) for l in lmat]
    lpow = [_dot(x, _stack4(x, blk)) for x in lb]
    for lvl in range(5):
        z = [_stack4(x.astype(BF16), blk) for x in lpow]
        if lvl < 4:
            r = [_dot(jnp.concatenate([p, lp], axis=0).astype(BF16), zz) for p, lp, zz in zip(pinv, lpow, z)]
            pinv = [p + x[:c] for p, x in zip(pinv, r)]
            lpow = [x[c:] for x in r]
        else:
            pinv = [p + _dot(p.astype(BF16), zz) for p, zz in zip(pinv, z)]
    egc = [jnp.exp(x) for x in gcum]
    pb = [p.astype(BF16) for p in pinv]
    for n, (d, ci) in enumerate(units):
        sl = sls[ci]
        u_ref, wq_ref, at_ref, kg_ref, ds_ref = out_refs[d]
        bc = beta[d][sl]
        u_ref[sl, :] = _dot(pb[n], _stack4((v[sl] * bc).astype(BF16), blk))
        w_key = _dot(pb[n], _stack4((k[sl] * bc * egc[n]).astype(BF16), blk))
        g_last = gcum[n][0:1, :] if d else gcum[n][c - 1:c, :]
        wq_ref[2 * ci * c:(2 * ci + 1) * c, :] = w_key.astype(BF16)
        wq_ref[(2 * ci + 1) * c:(2 * ci + 2) * c, :] = (q[sl] * egc[n]).astype(BF16)
        at_ref[sl, :] = attn[n].astype(BF16)
        kg_ref[sl, :] = (k[sl] * jnp.exp(g_last - gcum[n])).astype(BF16)
        ds_ref[ci] = jnp.exp(g_last)


def _gdn_prep(u, bsz, seq, cwq, cwk, cwv, alog, dtb, eb, eg):
    ts = GDN_BLOCK
    ns = seq // ts
    t = bsz * seq
    nc = ts // CHUNK
    specs = []
    for cb in (CB_QB, CB_KB, CB_VB):
        specs += list(_row_specs(ts, ns, t, cb, False))
    specs.append(pl.BlockSpec((ts, LANES), lambda b, i: (b * ns + i, CB128_AB)))
    specs += [_const_spec((4, W_GROUP))] * 3
    specs += [_const_spec((1, LANES))] * 2 + [_const_spec((2, LANES, W_GROUP))] * 2

    def row(b, i):
        return (b * ns + i, 0)

    one_dir_specs = [pl.BlockSpec((ts, W_GROUP), row), pl.BlockSpec((2 * ts, W_GROUP), row),
                     pl.BlockSpec((ts, W_GROUP), row), pl.BlockSpec((ts, W_GROUP), row),
                     pl.BlockSpec((nc, 1, W_GROUP), lambda b, i: (b * ns + i, 0, 0))]
    one_dir_shapes = [jax.ShapeDtypeStruct((t, W_GROUP), F32), jax.ShapeDtypeStruct((2 * t, W_GROUP), BF16),
                      jax.ShapeDtypeStruct((t, W_GROUP), BF16), jax.ShapeDtypeStruct((t, W_GROUP), BF16),
                      jax.ShapeDtypeStruct((t // CHUNK, 1, W_GROUP), F32)]
    return pl.pallas_call(
        _gdn_prep_kernel,
        grid=(bsz, ns),
        in_specs=specs,
        out_specs=one_dir_specs * 2,
        out_shape=one_dir_shapes * 2,
        compiler_params=_params(("parallel", "parallel")),
        name="gdn_prep",
    )(*([u] * 10 + [cwq, cwk, cwv, alog, dtb, eb, eg]))


def _gdn_scan_kernel(uf, wqf, atf, kgf, dsf, ub, wqb, atb, kgb, dsb, of_ref, ob_ref, s_ref):
    bsz, ts, _ = uf.shape
    nc = ts // CHUNK
    c = CHUNK

    @pl.when(pl.program_id(0) == 0)
    def _():
        s_ref[...] = jnp.zeros_like(s_ref)

    blk = _iota((W_GROUP, W_GROUP), 0) // HEAD_DIM == _iota((W_GROUP, W_GROUP), 1) // HEAD_DIM
    ins = ((uf, wqf, atf, kgf, dsf, of_ref), (ub, wqb, atb, kgb, dsb, ob_ref))
    chains = [(d, b) for b in range(bsz) for d in range(2)]
    state = [s_ref[d, b] for d, b in chains]
    for step in range(nc):
        pos = (step, nc - 1 - step)
        ws = [_dot(ins[d][1][b, 2 * pos[d] * c:(2 * pos[d] + 2) * c, :], s.astype(BF16))
              for (d, b), s in zip(chains, state)]
        vnb = [(ins[d][0][b, pos[d] * c:(pos[d] + 1) * c, :] - x[:c]).astype(BF16) for (d, b), x in zip(chains, ws)]
        for (d, b), x, vn in zip(chains, ws, vnb):
            sl = slice(pos[d] * c, (pos[d] + 1) * c)
            ins[d][5][b, sl, :] = x[c:] + _dot(ins[d][2][b, sl, :], _stack4(vn, blk))
        state = [s * ins[d][4][b, pos[d]] + jnp.where(blk, _dot_tn(ins[d][3][b, pos[d] * c:(pos[d] + 1) * c, :], vn), 0.0)
                 for (d, b), s, vn in zip(chains, state, vnb)]
    for (d, b), s in zip(chains, state):
        s_ref[d, b] = s


def _gdn_scan(prep, bsz, seq):
    ts = GDN_BLOCK
    ns = seq // ts
    nc = ts // CHUNK
    specs, args = [], []
    for d in range(2):
        tb = (lambda i: ns - 1 - i) if d else (lambda i: i)
        u_, wq, at, kg, ds = prep[5 * d:5 * d + 5]
        args += [u_.reshape(bsz, seq, W_GROUP), wq.reshape(bsz, 2 * seq, W_GROUP), at.reshape(bsz, seq, W_GROUP),
                 kg.reshape(bsz, seq, W_GROUP), ds.reshape(bsz, seq // CHUNK, 1, W_GROUP)]

        def blk3(rows, tb=tb):
            return pl.BlockSpec((bsz, rows, W_GROUP), lambda i: (0, tb(i), 0))

        specs += [blk3(ts), blk3(2 * ts), blk3(ts), blk3(ts),
                  pl.BlockSpec((bsz, nc, 1, W_GROUP), lambda i, tb=tb: (0, tb(i), 0, 0))]
    out_shape = jax.ShapeDtypeStruct((bsz, seq, W_GROUP), F32)
    o_fwd, o_bwd = pl.pallas_call(
        _gdn_scan_kernel,
        grid=(ns,),
        in_specs=specs,
        out_specs=[pl.BlockSpec((bsz, ts, W_GROUP), lambda i: (0, i, 0)),
                   pl.BlockSpec((bsz, ts, W_GROUP), lambda i: (0, ns - 1 - i, 0))],
        out_shape=[out_shape, out_shape],
        scratch_shapes=[pltpu.VMEM((2, bsz, W_GROUP, W_GROUP), F32)],
        compiler_params=_params(("arbitrary",)),
        name="gdn_scan",
    )(*args)
    return o_fwd.reshape(bsz * seq, W_GROUP), o_bwd.reshape(bsz * seq, W_GROUP)


RET_CHUNK = 128
RET_BLOCK = 512


def _ret_kernel(reverse, *refs):
    if reverse:
        (q_ref, k_ref, v_ref, cos_ref, sin_ref, lgq_ref, lgv_ref, o_ref, s_ref) = refs
    else:
        (q_ref, k_ref, v_ref, cos_ref, sin_ref, lgq_ref, lgv_ref, lgfc_ref, lgbc_ref, g_ref, ob_ref, rn_ref,
         o_ref, s_ref, dm_ref) = refs
    ts = q_ref.shape[0]
    cr = RET_CHUNK
    nc = ts // cr
    nk = N_HEADS * cr

    @pl.when(pl.program_id(1) == 0)
    def _():
        s_ref[...] = jnp.zeros_like(s_ref)
        if not reverse:
            ii = _iota((cr, nk), 0)
            jj = _iota((cr, nk), 1) % cr
            dist = (ii - jj).astype(F32)
            lower, upper = jj <= ii, jj >= ii
            dm_ref[...] = (jnp.where(lower, jnp.exp(jnp.where(lower, dist * lgfc_ref[...], 0.0)), 0.0)
                           + jnp.where(upper, jnp.exp(jnp.where(upper, -dist * lgbc_ref[...], 0.0)), 0.0))

    cos = cos_ref[...]
    sin = sin_ref[...]

    def rope(t):
        t1, t2 = t[:, :LANES], t[:, LANES:]
        return jnp.concatenate([t1 * cos - t2 * sin, t1 * sin + t2 * cos], axis=-1)

    q = rope(q_ref[...])
    k = rope(k_ref[...]) * (HEAD_DIM ** -0.5)
    v = v_ref[...]
    lgq = lgq_ref[...]
    lgv = lgv_ref[...]
    pos = _iota((cr, W_GROUP), 0).astype(F32)
    rows_qk = (_iota((W_GROUP, W_GROUP), 0) % LANES) // 32
    blk_state = rows_qk == _iota((W_GROUP, W_GROUP), 1) // HEAD_DIM
    if reverse:
        q_fac = jnp.exp((cr - pos) * lgq)
        k_fac = jnp.exp(pos * lgq)
    else:
        q_fac = jnp.exp((pos + 1.0) * lgq)
        k_fac = jnp.exp((cr - 1.0 - pos) * lgq)
    s_fac = jnp.exp(cr * lgv)
    sls = [slice(ci * cr, (ci + 1) * cr) for ci in range(nc)]
    kv = [jnp.where(blk_state, _dot_tn((k[sl] * k_fac).astype(BF16), v[sl].astype(BF16)), 0.0) for sl in sls]
    s = s_ref[...]
    cross = [None] * nc
    for ci in (range(nc - 1, -1, -1) if reverse else range(nc)):
        cross[ci] = _dot((q[sls[ci]] * q_fac).astype(BF16), s.astype(BF16))
        s = s * s_fac + kv[ci]
    s_ref[...] = s
    if reverse:
        o_ref[...] = jnp.concatenate(cross, axis=0)
        return
    stack_head = _iota((nk, W_GROUP), 0) // cr
    k_mask = stack_head == (_iota((nk, W_GROUP), 1) % LANES) // 32
    v_mask = stack_head == _iota((nk, W_GROUP), 1) // HEAD_DIM
    qk = [_dot_nt(q[sl].astype(BF16), _stack4(k[sl].astype(BF16), k_mask)) for sl in sls]
    dm = dm_ref[...]
    intra = [_dot((x * dm).astype(BF16), _stack4(v[sl].astype(BF16), v_mask)) for x, sl in zip(qk, sls)]
    o = jnp.concatenate([a + b for a, b in zip(intra, cross)], axis=0) + ob_ref[...]
    ones_h = _head_ones()
    oc = o - _dot_sel(o, ones_h, 3) * (1.0 / HEAD_DIM)
    var = _dot_sel(oc * oc, ones_h, 2) * (1.0 / HEAD_DIM)
    o_ref[...] = (oc * lax.rsqrt(var + EPS) * rn_ref[...] * _silu(g_ref[...])).astype(o_ref.dtype)


def _ret(u, bsz, seq, reverse, cos, sin, lgq, lgv, lgfc=None, lgbc=None, rn=None, ob=None):
    cr = RET_CHUNK
    ts = RET_BLOCK
    ns = seq // ts
    t = bsz * seq
    tbf = (lambda i: ns - 1 - i) if reverse else (lambda i: i)

    def col(cb):
        return pl.BlockSpec((ts, W_GROUP), lambda b, i: (b * ns + tbf(i), cb))

    tab = pl.BlockSpec((ts, LANES), lambda b, i: (tbf(i), 0))
    specs = [col(CB_QC), col(CB_KC), col(CB_VC), tab, tab, _const_spec((1, W_GROUP)), _const_spec((1, W_GROUP))]
    args = [u, u, u, cos, sin, lgq, lgv]
    scratch = [pltpu.VMEM((W_GROUP, W_GROUP), F32)]
    if reverse:
        out_dtype = F32
    else:
        specs += [_const_spec((1, N_HEADS * cr)), _const_spec((1, N_HEADS * cr)), col(CB_GC),
                  pl.BlockSpec((ts, W_GROUP), lambda b, i: (b * ns + i, 0)), _const_spec((1, W_GROUP))]
        args += [lgfc, lgbc, u, ob, rn]
        scratch.append(pltpu.VMEM((cr, N_HEADS * cr), F32))
        out_dtype = BF16
    return pl.pallas_call(
        functools.partial(_ret_kernel, reverse),
        grid=(bsz, ns),
        in_specs=specs,
        out_specs=pl.BlockSpec((ts, W_GROUP), lambda b, i: (b * ns + tbf(i), 0)),
        out_shape=jax.ShapeDtypeStruct((t, W_GROUP), out_dtype),
        scratch_shapes=scratch,
        compiler_params=_params(("parallel", "arbitrary")),
        name="ret_bwd" if reverse else "ret_fwd",
    )(*args)


NA_GROUP = 8
NA_WIN = NA_KH * GRID_W
NA_ROWS_PER_PASS = 4


def _na_kernel(rows, q_ref, kp_ref, kc_ref, kn_ref, vp_ref, vc_ref, vn_ref, tab_ref, o_ref, kw, vw):
    g = pl.program_id(1)
    blk = NA_GROUP * GRID_W
    kw[0:blk, :] = kp_ref[...]
    kw[blk:2 * blk, :] = kc_ref[...]
    kw[2 * blk:3 * blk, :] = kn_ref[...]
    vw[0:blk, :] = vp_ref[...]
    vw[blk:2 * blk, :] = vc_ref[...]
    vw[2 * blk:3 * blk, :] = vn_ref[...]
    row_head = _iota((N_HEADS * GRID_W, W_GROUP), 0) // GRID_W
    lane_head = _iota((N_HEADS * GRID_W, W_GROUP), 1) // HEAD_DIM
    blk = row_head == lane_head
    out_head = _iota((GRID_W, W_GROUP), 1) // HEAD_DIM
    for j0 in range(0, NA_GROUP, NA_ROWS_PER_PASS):
        js = list(range(j0, j0 + NA_ROWS_PER_PASS))
        vwins, sc = [], []
        for j in js:
            r = g * NA_GROUP + j
            r0 = jnp.clip(r - NA_KH // 2, 0, rows - NA_KH)
            off = pl.multiple_of((r0 - g * NA_GROUP + NA_GROUP) * GRID_W, GRID_W)
            qst = _stack4(q_ref[j * GRID_W:(j + 1) * GRID_W, :], blk)
            vwins.append(vw[pl.ds(off, NA_WIN), :])
            sc.append(_dot_nt(qst, kw[pl.ds(off, NA_WIN), :]) + tab_ref[r - r0])
        mx = [jnp.max(s, axis=-1, keepdims=True) for s in sc]
        pr = [jnp.exp(s - m) for s, m in zip(sc, mx)]
        den = [jnp.sum(p, axis=-1, keepdims=True) for p in pr]
        pv = [_dot(p.astype(BF16), vw_) / l for p, vw_, l in zip(pr, vwins, den)]
        for j, o in zip(js, pv):
            res = o[0:GRID_W]
            for h in range(1, N_HEADS):
                res = jnp.where(out_head == h, o[h * GRID_W:(h + 1) * GRID_W], res)
            o_ref[j * GRID_W:(j + 1) * GRID_W, :] = res.astype(o_ref.dtype)


def _na(qn, kn, vn, table, bsz, seq):
    rows = seq // GRID_W
    ng = rows // NA_GROUP
    blk = NA_GROUP * GRID_W
    t = bsz * seq
    cur = pl.BlockSpec((blk, W_GROUP), lambda b, g: (b * ng + g, 0))
    prev = pl.BlockSpec((blk, W_GROUP), lambda b, g: (b * ng + jnp.maximum(g - 1, 0), 0))
    nxt = pl.BlockSpec((blk, W_GROUP), lambda b, g: (b * ng + jnp.minimum(g + 1, ng - 1), 0))
    return pl.pallas_call(
        functools.partial(_na_kernel, rows),
        grid=(bsz, ng),
        in_specs=[cur, prev, cur, nxt, prev, cur, nxt, _const_spec((NA_KH, N_HEADS * GRID_W, NA_WIN))],
        out_specs=cur,
        out_shape=jax.ShapeDtypeStruct((t, W_GROUP), BF16),
        scratch_shapes=[pltpu.VMEM((3 * blk, W_GROUP), BF16), pltpu.VMEM((3 * blk, W_GROUP), BF16)],
        compiler_params=_params(("parallel", "parallel")),
        name="na",
    )(qn, kn, kn, kn, vn, vn, vn, table)


def _na_table(rpb):
    didx = np.arange(NA_KH)[:, None]
    kk = np.arange(NA_KH)[None, :]
    dr = (kk + (NA_KH - 1) - didx).reshape(-1)
    cc = np.arange(GRID_W)[:, None]
    kc = np.arange(GRID_W)[None, :]
    c0 = np.clip(cc - NA_KW // 2, 0, GRID_W - NA_KW)
    valid = (kc >= c0) & (kc < c0 + NA_KW)
    dc = np.clip(kc - cc + (NA_KW - 1), 0, 2 * NA_KW - 2).reshape(-1)
    row_sel = np.zeros((dr.size, 2 * NA_KH - 1), np.float32)
    row_sel[np.arange(dr.size), dr] = 1.0
    col_sel = np.zeros((2 * NA_KW - 1, dc.size), np.float32)
    col_sel[dc, np.arange(dc.size)] = 1.0
    bias = jnp.einsum('ar,hrs,sb->hab', row_sel, rpb, col_sel, precision=lax.Precision.HIGHEST)
    bias = bias.reshape(N_HEADS, NA_KH, NA_KH, GRID_W, GRID_W)
    tab = jnp.where(valid[None, None, None], bias, NEG_BIG)
    return jnp.transpose(tab, (1, 0, 3, 2, 4)).reshape(NA_KH, N_HEADS * GRID_W, NA_WIN)


FF_CHUNK = 256
POST_HALO = 16


def _mix_residual(x, a, bf, bb, z, gn, c, d, w_ref):
    ob = bf + bb
    ms = _dot_sel(ob * ob, _head_ones(), 2) * (1.0 / HEAD_DIM)
    out_b = (ob * lax.rsqrt(ms + EPS) * gn * _silu(z)).astype(BF16)
    acc = x
    for gi, m in enumerate((a, out_b, c, d)):
        acc = acc + _dot(m, w_ref[gi * W_GROUP:(gi + 1) * W_GROUP, :])
    return acc


def _post_kernel(*refs):
    streams = [refs[3 * n:3 * n + 3] for n in range(7)]
    (p_ref, gn_ref, wo_ref, n2_ref, wg_ref, wu_ref, cw_ref, cb_ref, wd_ref, n3_ref, pg_ref, pp_ref,
     o_ref, act_s) = refs[21:]
    tm = streams[0][0].shape[0]
    i = pl.program_id(1)
    first = i == 0
    last = i == pl.num_programs(1) - 1
    gn = gn_ref[...]
    mains = [s[0][...] for s in streams]
    halos = [jnp.concatenate([s[1][...], s[2][...]], axis=0) for s in streams]
    x = _mix_residual(mains[0], *mains[1:5], gn, *mains[5:], wo_ref)
    xh = _mix_residual(halos[0], *halos[1:5], gn, *halos[5:], wo_ref)
    n2 = n2_ref[...]
    h2f = _rms(x, n2)
    h2h = _rms(xh, n2)
    h_before = jnp.where(first, 0.0, h2h[POST_HALO - SUBLANES:POST_HALO])
    h_after = jnp.where(last, 0.0, h2h[POST_HALO:POST_HALO + SUBLANES])
    h2 = h2f.astype(BF16)
    h2e = jnp.concatenate([h_before, h2f, h_after], axis=0).astype(BF16)
    for f in range(D_FF // FF_CHUNK):
        fs = slice(f * FF_CHUNK, (f + 1) * FF_CHUNK)
        ge = _dot(h2e, wg_ref[:, fs])
        ne = tm + 2 * SUBLANES
        g_prev = pltpu.roll(ge, 1, 0)[SUBLANES:SUBLANES + tm]
        g_next = pltpu.roll(ge, ne - 1, 0)[SUBLANES:SUBLANES + tm]
        gate = (cw_ref[0:1, fs] * g_prev + cw_ref[1:2, fs] * ge[SUBLANES:SUBLANES + tm]
                + cw_ref[2:3, fs] * g_next + cb_ref[:, fs])
        up = _dot(h2, wu_ref[:, fs])
        act_s[:, fs] = (_gelu_tanh(gate) * up).astype(BF16)
    acc = x + _dot(act_s[...], wd_ref[...])
    h3 = _rms(acc, n3_ref[...]).astype(BF16)
    gate3 = _sigmoid(_dot(h3, pg_ref[...]))
    o_ref[...] = acc + gate3 * _dot(p_ref[...].astype(BF16), pp_ref[...])


def _post(x, u, oa, obf, obb, gn, oc, od, w_out, p, bsz, seq, n2, wg, wu, cw, cb, wd, n3, pg, pp):
    tm = 512
    ns = seq // tm
    t = bsz * seq
    specs = list(_row_specs(tm, ns, t, 0, False, width=D_MODEL, halo=POST_HALO))
    arrays = [x] * 3
    for arr, cb_ in ((oa, 0), (obf, 0), (obb, 0), (u, CB_ZB), (oc, 0), (od, 0)):
        specs += list(_row_specs(tm, ns, t, cb_, False, halo=POST_HALO))
        arrays += [arr] * 3
    specs += [pl.BlockSpec((tm, PLE_DIM), lambda b, i: (b * ns + i, 0)),
              _const_spec((1, W_GROUP)), _const_spec((D_MODEL, D_MODEL)),
              _const_spec((1, D_MODEL)), _const_spec((D_MODEL, D_FF)), _const_spec((D_MODEL, D_FF)),
              _const_spec((3, D_FF)), _const_spec((1, D_FF)), _const_spec((D_FF, D_MODEL)),
              _const_spec((1, D_MODEL)), _const_spec((D_MODEL, D_MODEL)), _const_spec((PLE_DIM, D_MODEL))]
    return pl.pallas_call(
        _post_kernel,
        grid=(bsz, ns),
        in_specs=specs,
        out_specs=pl.BlockSpec((tm, D_MODEL), lambda b, i: (b * ns + i, 0)),
        out_shape=jax.ShapeDtypeStruct((t, D_MODEL), F32),
        compiler_params=_params(("parallel", "parallel")),
        scratch_shapes=[pltpu.VMEM((tm, D_FF), BF16)],
        name="post",
    )(*arrays, p, gn, w_out, n2, wg, wu, cw, cb, wd, n3, pg, pp)


def _block_diag(w):
    h, n, _ = w.shape
    eye = jnp.eye(h, dtype=w.dtype)
    return (eye[:, None, :, None] * w[:, :, None, :]).reshape(h * n, h * n)


def _row(v):
    return v.reshape(1, -1).astype(F32)


def _lane_pad(v8, offset):
    return jnp.zeros((1, LANES), F32).at[0, offset:offset + 2 * N_HEADS].set(v8.reshape(-1).astype(F32))


def _expander(offset):
    e = np.zeros((LANES, W_GROUP), np.float32)
    for h in range(N_HEADS):
        e[offset + h, h * HEAD_DIM:(h + 1) * HEAD_DIM] = 1.0
    return jnp.asarray(e, BF16)


def _layer(x, p, lw, bsz, seq):
    w_in = _reorder_in_cols(lw['w_in'].astype(BF16))
    u, qn, kn, vn = _inproj(x, _row(lw['norm1']), w_in,
                            _row(jnp.tile(lw['na_qnorm'], N_HEADS)), _row(jnp.tile(lw['na_knorm'], N_HEADS)))

    cw_a, cb_a = lw['conv_a_w'].astype(F32), _row(lw['conv_a_b'])
    h_dir = None
    for d in (1, 0):
        h_dir = _rglru(u, bsz, seq, d == 1, cw_a, cb_a,
                       _block_diag(lw['lru_wr'][d]).astype(BF16), _row(lw['lru_br'][d]),
                       _block_diag(lw['lru_wi'][d]).astype(BF16), _row(lw['lru_bi'][d]),
                       _row(lw['lru_lambda'][d]), hb=h_dir)
    out_a = h_dir

    gconv = lw['gdn_conv'].astype(F32)
    cwq, cwk, cwv = gconv[:, :W_GROUP], gconv[:, W_GROUP:2 * W_GROUP], gconv[:, 2 * W_GROUP:]
    alog = _lane_pad(lw['gdn_a_log'], 2 * N_HEADS)
    dtb = _lane_pad(lw['gdn_dt_bias'], 2 * N_HEADS)
    gn = _row(jnp.tile(lw['gdn_norm'], N_HEADS))
    eb = jnp.stack([_expander(d * N_HEADS) for d in range(2)])
    eg = jnp.stack([_expander(2 * N_HEADS + d * N_HEADS) for d in range(2)])
    o_b_fwd, o_b_bwd = _gdn_scan(_gdn_prep(u, bsz, seq, cwq, cwk, cwv, alog, dtb, eb, eg), bsz, seq)

    half = HEAD_DIM // 2
    inv_freq = ROPE_BASE ** (-jnp.arange(half, dtype=F32) / half)
    ang = jnp.arange(seq, dtype=F32)[:, None] * inv_freq[None, :]
    cos = jnp.tile(jnp.cos(ang), (1, N_HEADS))
    sin = jnp.tile(jnp.sin(ang), (1, N_HEADS))
    log_gamma = jnp.log1p(-jnp.exp2(-lw['ret_decay'].astype(F32)))
    rn = _row(jnp.tile(lw['ret_norm'], N_HEADS))
    o_dir = None
    for d in (1, 0):
        lg = log_gamma[d]
        lgq = _row(jnp.tile(jnp.repeat(lg, half), 2))
        lgv = _row(jnp.repeat(lg, HEAD_DIM))
        if d == 1:
            o_dir = _ret(u, bsz, seq, True, cos, sin, lgq, lgv)
        else:
            o_dir = _ret(u, bsz, seq, False, cos, sin, lgq, lgv,
                         lgfc=_row(jnp.repeat(log_gamma[0], RET_CHUNK)), lgbc=_row(jnp.repeat(log_gamma[1], RET_CHUNK)),
                         rn=rn, ob=o_dir)
    out_c = o_dir

    out_d = _na(qn, kn, vn, _na_table(lw['na_rpb'].astype(F32)), bsz, seq)

    return _post(x, u, out_a, o_b_fwd, o_b_bwd, gn, out_c, out_d, lw['w_out'].astype(BF16), p, bsz, seq,
                 _row(lw['norm2']), lw['ffn_wg'].astype(BF16), lw['ffn_wu'].astype(BF16),
                 lw['ffn_conv_w'].astype(F32), _row(lw['ffn_conv_b']), lw['ffn_wd'].astype(BF16),
                 _row(lw['norm3']), lw['ple_gate'].astype(BF16), lw['ple_proj'].astype(BF16))


_WEIGHT_NAMES = ('norm1', 'norm2', 'norm3', 'w_in', 'w_out', 'conv_a_w', 'conv_a_b', 'lru_wr', 'lru_br',
                 'lru_wi', 'lru_bi', 'lru_lambda', 'gdn_conv', 'gdn_a_log', 'gdn_dt_bias', 'gdn_norm',
                 'ret_decay', 'ret_norm', 'na_qnorm', 'na_knorm', 'na_rpb', 'ffn_wg', 'ffn_wu',
                 'ffn_conv_w', 'ffn_conv_b', 'ffn_wd', 'ple_proj', 'ple_gate')


def _trunk(x, p, weights):
    bsz, seq, _ = x.shape
    xf = x.reshape(bsz * seq, D_MODEL)
    for i in range(DEPTH):
        lw = {n: w[i] for n, w in weights.items()}
        xf = _layer(xf, p[i].reshape(bsz * seq, PLE_DIM), lw, bsz, seq)
    return xf.reshape(bsz, seq, D_MODEL)


def kernel(x_prompt, x_sample, p_prompt, p_sample, norm1, norm2, norm3, w_in, w_out, conv_a_w, conv_a_b, lru_wr, lru_br, lru_wi, lru_bi, lru_lambda, gdn_conv, gdn_a_log, gdn_dt_bias, gdn_norm, ret_decay, ret_norm, na_qnorm, na_knorm, na_rpb, ffn_wg, ffn_wu, ffn_conv_w, ffn_conv_b, ffn_wd, ple_proj, ple_gate):
    weights = dict(zip(_WEIGHT_NAMES, (norm1, norm2, norm3, w_in, w_out, conv_a_w, conv_a_b, lru_wr, lru_br,
                                       lru_wi, lru_bi, lru_lambda, gdn_conv, gdn_a_log, gdn_dt_bias, gdn_norm,
                                       ret_decay, ret_norm, na_qnorm, na_knorm, na_rpb, ffn_wg, ffn_wu,
                                       ffn_conv_w, ffn_conv_b, ffn_wd, ple_proj, ple_gate)))
    return (_trunk(x_prompt, p_prompt, weights), _trunk(x_sample, p_sample, weights))
```

```python
import functools

import jax
import jax.numpy as jnp
import numpy as np
from jax import lax
from jax.experimental import pallas as pl
from jax.experimental.pallas import tpu as pltpu

F32 = jnp.float32
BF16 = jnp.bfloat16

D_MODEL = 1024
DEPTH = 2
PLE_DIM = 256
GRID_W = 64
W_GROUP = 256
HEAD_DIM = 64
N_HEADS = 4
LRU_C = 8.0
CHUNK = 64
ROPE_BASE = 10000.0
NA_KH = 8
NA_KW = 16
D_FF = 2816
EPS = 1e-6
NEG_BIG = -1e30

SUBLANES = 8
LANES = 128
VMEM_LIMIT = 56 * 1024 * 1024

CB_XA, CB_YA, CB_QB, CB_KB, CB_VB, CB_ZB, CB_QC, CB_KC, CB_VC, CB_GC = range(10)
CB128_AB = 20
U_COLS = 10 * W_GROUP + LANES
IN_COLS_PAD = U_COLS + 3 * W_GROUP
IN_COLS = 13 * W_GROUP + 4 * N_HEADS


def _reorder_in_cols(w):
    rows = w.shape[0]
    half = HEAD_DIM // 2

    def halves_first(cols):
        return cols.reshape(rows, N_HEADS, 2, half).transpose(0, 2, 1, 3).reshape(rows, W_GROUP)

    ab0 = 6 * W_GROUP
    qc0 = ab0 + 4 * N_HEADS
    qd0 = qc0 + 4 * W_GROUP
    parts = [w[:, :ab0], halves_first(w[:, qc0:qc0 + W_GROUP]), halves_first(w[:, qc0 + W_GROUP:qc0 + 2 * W_GROUP]),
             w[:, qc0 + 2 * W_GROUP:qd0], w[:, ab0:qc0], jnp.zeros((rows, LANES - 4 * N_HEADS), w.dtype), w[:, qd0:]]
    out = jnp.concatenate(parts, axis=1)
    assert out.shape[1] == IN_COLS_PAD
    return out


def _sigmoid(x):
    return 1.0 / (1.0 + jnp.exp(-x))


def _silu(x):
    return x * _sigmoid(x)


def _softplus(x):
    return jnp.maximum(x, 0.0) + jnp.log1p(jnp.exp(-jnp.abs(x)))


def _gelu_tanh(x):
    return x * (0.5 * (1.0 + jnp.tanh(0.7978845608028654 * (x + 0.044715 * (x * x * x)))))


def _rms(x, gain):
    ms = jnp.mean(x * x, axis=-1, keepdims=True)
    return x * lax.rsqrt(ms + EPS) * gain


def _dot(a, b):
    return jnp.dot(a, b, preferred_element_type=F32)


def _split_bf16(x, terms):
    parts = []
    for _ in range(terms):
        p = x.astype(BF16)
        parts.append(p)
        x = x - p.astype(F32)
    return parts


def _dot_sel(x, sel, terms):
    return sum(_dot(p, sel) for p in _split_bf16(x, terms))


def _sel_dot(sel, x, terms):
    return sum(_dot(sel, p) for p in _split_bf16(x, terms))


def _dot_nt(a, b):
    return lax.dot_general(a, b, (((1,), (1,)), ((), ())), preferred_element_type=F32)


def _dot_tn(a, b):
    return lax.dot_general(a, b, (((0,), (0,)), ((), ())), preferred_element_type=F32)


def _iota(shape, dim):
    return lax.broadcasted_iota(jnp.int32, shape, dim)


def _head_ones():
    return (_iota((W_GROUP, W_GROUP), 0) // HEAD_DIM == _iota((W_GROUP, W_GROUP), 1) // HEAD_DIM).astype(BF16)


def _stack4(x, mask):
    return jnp.where(mask, jnp.concatenate([x] * N_HEADS, axis=0), jnp.zeros((), x.dtype))


def _conv4(xm, xp, xn, w_ref, first, last):
    ts = xm.shape[0]
    xp = jnp.where(first, 0.0, xp)
    xn = jnp.where(last, 0.0, xn)
    xe = jnp.concatenate([xp, xm, xn], axis=0)
    n = ts + 2 * SUBLANES

    def tap(offset):
        rolled = xe if offset == 0 else pltpu.roll(xe, (-offset) % n, 0)
        return rolled[SUBLANES:SUBLANES + ts]

    out = w_ref[0:1, :] * tap(-2)
    out = out + w_ref[1:2, :] * tap(-1)
    out = out + w_ref[2:3, :] * tap(0)
    out = out + w_ref[3:4, :] * tap(1)
    return out


def _time_block(reverse):
    i = pl.program_id(1)
    n = pl.num_programs(1)
    tb = (n - 1 - i) if reverse else i
    return tb, tb == 0, tb == n - 1


def _row_specs(ts, ns, total_rows, col_block, reverse, width=W_GROUP, halo=SUBLANES):
    per_block = ts // halo
    last = total_rows // halo - 1

    def tb(i):
        return (ns - 1 - i) if reverse else i

    main = pl.BlockSpec((ts, width), lambda b, i: (b * ns + tb(i), col_block))
    prev = pl.BlockSpec((halo, width), lambda b, i: (jnp.maximum((b * ns + tb(i)) * per_block - 1, 0), col_block))
    nxt = pl.BlockSpec((halo, width), lambda b, i: (jnp.minimum((b * ns + tb(i) + 1) * per_block, last), col_block))
    return main, prev, nxt


def _const_spec(shape):
    nd = len(shape)
    return pl.BlockSpec(shape, lambda *_: (0,) * nd, pipeline_mode=pl.Buffered(1))


def _params(sem):
    return pltpu.CompilerParams(dimension_semantics=sem, vmem_limit_bytes=VMEM_LIMIT)


def _inproj_kernel(x_ref, g_ref, w_ref, gq_ref, gk_ref, u_ref, qd_ref, kd_ref, vd_ref):
    h = _rms(x_ref[...], g_ref[...]).astype(BF16)
    u_ref[...] = _dot(h, w_ref[:, :U_COLS])
    qkv = _dot(h, w_ref[:, U_COLS:])
    q, k = qkv[:, :W_GROUP], qkv[:, W_GROUP:2 * W_GROUP]
    ones_h = _head_ones()
    inv = 1.0 / HEAD_DIM
    qd_ref[...] = (q * lax.rsqrt(_dot_sel(q * q, ones_h, 2) * inv + EPS) * gq_ref[...] * (HEAD_DIM ** -0.5)).astype(BF16)
    kd_ref[...] = (k * lax.rsqrt(_dot_sel(k * k, ones_h, 2) * inv + EPS) * gk_ref[...]).astype(BF16)
    vd_ref[...] = qkv[:, 2 * W_GROUP:].astype(BF16)


def _inproj(x, gain, w, gq, gk):
    t = x.shape[0]
    tm = 512
    grp = pl.BlockSpec((tm, W_GROUP), lambda i: (i, 0))
    grp_shape = jax.ShapeDtypeStruct((t, W_GROUP), BF16)
    return pl.pallas_call(
        _inproj_kernel,
        grid=(t // tm,),
        in_specs=[pl.BlockSpec((tm, D_MODEL), lambda i: (i, 0)),
                  _const_spec((1, D_MODEL)),
                  _const_spec((D_MODEL, IN_COLS_PAD)),
                  _const_spec((1, W_GROUP)), _const_spec((1, W_GROUP))],
        out_specs=[pl.BlockSpec((tm, U_COLS), lambda i: (i, 0)), grp, grp, grp],
        out_shape=[jax.ShapeDtypeStruct((t, U_COLS), F32), grp_shape, grp_shape, grp_shape],
        compiler_params=_params(("parallel",)),
        name="inproj",
    )(x, gain, w, gq, gk)


def _rglru_kernel(reverse, *refs):
    if reverse:
        (xm_ref, xp_ref, xn_ref, cw_ref, cb_ref, wr_ref, br_ref, wi_ref, bi_ref, lam_ref,
         o_ref, a_s, b_s, p_s, ac_s, c_s, carry, h_s) = refs
    else:
        (xm_ref, xp_ref, xn_ref, cw_ref, cb_ref, wr_ref, br_ref, wi_ref, bi_ref, lam_ref, ya_ref, hb_ref,
         o_ref, a_s, b_s, p_s, ac_s, c_s, carry, h_s) = refs
    ts = xm_ref.shape[0]
    ng = ts // SUBLANES
    tb, first, last = _time_block(reverse)

    @pl.when(pl.program_id(1) == 0)
    def _():
        carry[...] = jnp.zeros_like(carry)

    x = _conv4(xm_ref[...], xp_ref[...], xn_ref[...], cw_ref, first, last) + cb_ref[...]
    xb = x.astype(BF16)
    r = _sigmoid(_dot(xb, wr_ref[...]) + br_ref[...])
    ig = _sigmoid(_dot(xb, wi_ref[...]) + bi_ref[...])
    log_a = (-LRU_C) * r * _softplus(-lam_ref[...])
    a = jnp.exp(log_a)
    th = jnp.tanh(-log_a)
    bx = jnp.sqrt(2.0 * th / (1.0 + th)) * (ig * x)
    nl = W_GROUP // LANES
    for hf in range(nl):
        a_s[hf] = a[:, hf * LANES:(hf + 1) * LANES]
        b_s[hf] = bx[:, hf * LANES:(hf + 1) * LANES]

    def slab(ref, k):
        return jnp.concatenate([ref[hf, pl.ds(k, ng, stride=SUBLANES), :] for hf in range(nl)], axis=-1)

    order = list(range(SUBLANES))[::-1] if reverse else list(range(SUBLANES))
    p = None
    ac = None
    for k in order:
        a_k = slab(a_s, k)
        b_k = slab(b_s, k)
        if p is None:
            p, ac = b_k, a_k
        else:
            p = a_k * p + b_k
            ac = a_k * ac
        p_s[k] = p
        ac_s[k] = ac

    k_last = order[-1]

    def body(g, c):
        gi = (ng - 1 - g) if reverse else g
        c_s[pl.ds(gi, 1), :] = c
        return ac_s[k_last, pl.ds(gi, 1), :] * c + p_s[k_last, pl.ds(gi, 1), :]

    carry[...] = lax.fori_loop(0, ng, body, carry[...], unroll=8)
    c_in = c_s[...]
    for k in order:
        h_k = p_s[k] + ac_s[k] * c_in
        for hf in range(nl):
            h_s[hf, pl.ds(k, ng, stride=SUBLANES), :] = h_k[:, hf * LANES:(hf + 1) * LANES]
    h = jnp.concatenate([h_s[hf] for hf in range(nl)], axis=-1)
    if reverse:
        o_ref[...] = h
    else:
        o_ref[...] = ((h + hb_ref[...]) * _gelu_tanh(ya_ref[...])).astype(o_ref.dtype)


def _rglru(u, bsz, seq, reverse, cw, cb, wr, br, wi, bi, lam, hb=None):
    ts = 1024
    ns = seq // ts
    t = bsz * seq
    ng = ts // SUBLANES
    xm, xp, xn = _row_specs(ts, ns, t, CB_XA, reverse)
    in_specs = [xm, xp, xn, _const_spec((4, W_GROUP)), _const_spec((1, W_GROUP)),
                _const_spec((W_GROUP, W_GROUP)), _const_spec((1, W_GROUP)),
                _const_spec((W_GROUP, W_GROUP)), _const_spec((1, W_GROUP)), _const_spec((1, W_GROUP))]
    args = [u, u, u, cw, cb, wr, br, wi, bi, lam]
    halves = (W_GROUP // LANES, ts, LANES)
    scratch = [pltpu.VMEM(halves, F32), pltpu.VMEM(halves, F32),
               pltpu.VMEM((SUBLANES, ng, W_GROUP), F32), pltpu.VMEM((SUBLANES, ng, W_GROUP), F32),
               pltpu.VMEM((ng, W_GROUP), F32), pltpu.VMEM((1, W_GROUP), F32), pltpu.VMEM(halves, F32)]
    if reverse:
        out_dtype = F32
        out_spec = pl.BlockSpec((ts, W_GROUP), lambda b, i: (b * ns + ns - 1 - i, 0))
    else:
        ya = _row_specs(ts, ns, t, CB_YA, False)[0]
        in_specs += [ya, pl.BlockSpec((ts, W_GROUP), lambda b, i: (b * ns + i, 0))]
        args += [u, hb]
        out_dtype = BF16
        out_spec = pl.BlockSpec((ts, W_GROUP), lambda b, i: (b * ns + i, 0))
    return pl.pallas_call(
        functools.partial(_rglru_kernel, reverse),
        grid=(bsz, ns),
        in_specs=in_specs,
        out_specs=out_spec,
        out_shape=jax.ShapeDtypeStruct((t, W_GROUP), out_dtype),
        scratch_shapes=scratch,
        compiler_params=_params(("parallel", "arbitrary")),
        name="rglru_bwd" if reverse else "rglru_fwd",
    )(*args)


GDN_PREP_BLOCK = 512
GDN_SCAN_BLOCK = 256


def _gdn_prep_kernel(qm, qp, qn, km, kp, kn, vm, vp, vn, ab_ref, cwq, cwk, cwv, alog_ref, dtb_ref, eb_ref, eg_ref,
                     *outs):
    ts = qm.shape[0]
    nc = ts // CHUNK
    c = CHUNK
    i = pl.program_id(1)
    first = i == 0
    last = i == pl.num_programs(1) - 1
    out_refs = (outs[0:5], outs[5:10])

    ones_h = _head_ones()
    q = _silu(_conv4(qm[...], qp[...], qn[...], cwq, first, last))
    k = _silu(_conv4(km[...], kp[...], kn[...], cwk, first, last))
    v = _silu(_conv4(vm[...], vp[...], vn[...], cwv, first, last))
    q = q * lax.rsqrt(_dot_sel(q * q, ones_h, 2) + EPS) * (HEAD_DIM ** -0.5)
    k = k * lax.rsqrt(_dot_sel(k * k, ones_h, 2) + EPS)
    ab = ab_ref[...]
    sig_parts = _split_bf16(_sigmoid(ab), 2)
    g_parts = _split_bf16(-jnp.exp(alog_ref[...]) * _softplus(ab + dtb_ref[...]), 3)
    beta = [sum(_dot(p, eb_ref[d]) for p in sig_parts) for d in range(2)]
    gate = [sum(_dot(p, eg_ref[d]) for p in g_parts) for d in range(2)]

    ii = _iota((c, W_GROUP), 0)
    jj = _iota((c, W_GROUP), 1) % c
    r2 = _iota((c, c), 0)
    c2 = _iota((c, c), 1)
    incl = (jj <= ii, jj >= ii)
    strict = (jj < ii, jj > ii)
    tri = ((c2 <= r2).astype(BF16), (c2 >= r2).astype(BF16))
    ucat = ((ii > jj).astype(F32), (ii < jj).astype(F32))
    eye_cat = (ii == jj).astype(F32)
    blk = _iota((W_GROUP, W_GROUP), 0) // HEAD_DIM == _iota((W_GROUP, W_GROUP), 1) // HEAD_DIM

    sls = [slice(ci * c, (ci + 1) * c) for ci in range(nc)]
    kq = [_dot_nt(jnp.concatenate([k[sl], q[sl]], axis=0).astype(BF16), _stack4(k[sl].astype(BF16), blk))
          for sl in sls]
    units = [(d, ci) for ci in range(nc) for d in range(2)]
    gd = [_sel_dot(tri[d], jnp.concatenate([gate[d][sls[ci]], gate[d][sls[ci]] * ucat[d]], axis=1), 3)
          for d, ci in units]
    gcum = [x[:, :W_GROUP] for x in gd]
    decay = [jnp.where(incl[d], jnp.exp(jnp.where(incl[d], x[:, W_GROUP:], 0.0)), 0.0)
             for (d, ci), x in zip(units, gd)]
    lmat = [jnp.where(strict[d], beta[d][sls[ci]] * kq[ci][:c] * dec, 0.0) for (d, ci), dec in zip(units, decay)]
    attn = [kq[ci][c:] * dec for (d, ci), dec in zip(units, decay)]
    pinv = [eye_cat - l for l in lmat]
    lb = [l.astype(BF16) for l in lmat]
    lpow = [_dot(x, _stack4(x, blk)) for x in lb]
    for lvl in range(5):
        z = [_stack4(x.astype(BF16), blk) for x in lpow]
        if lvl < 4:
            r = [_dot(jnp.concatenate([p, lp], axis=0).astype(BF16), zz) for p, lp, zz in zip(pinv, lpow, z)]
            pinv = [p + x[:c] for p, x in zip(pinv, r)]
            lpow = [x[c:] for x in r]
        else:
            pinv = [p + _dot(p.astype(BF16), zz) for p, zz in zip(pinv, z)]
    egc = [jnp.exp(x) for x in gcum]
    pb = [p.astype(BF16) for p in pinv]
    for n, (d, ci) in enumerate(units):
        sl = sls[ci]
        u_ref, wq_ref, at_ref, kg_ref, ds_ref = out_refs[d]
        bc = beta[d][sl]
        u_ref[sl, :] = _dot(pb[n], _stack4((v[sl] * bc).astype(BF16), blk))
        w_key = _dot(pb[n], _stack4((k[sl] * bc * egc[n]).astype(BF16), blk))
        g_last = gcum[n][0:1, :] if d else gcum[n][c - 1:c, :]
        wq_ref[2 * ci * c:(2 * ci + 1) * c, :] = w_key.astype(BF16)
        wq_ref[(2 * ci + 1) * c:(2 * ci + 2) * c, :] = (q[sl] * egc[n]).astype(BF16)
        at_ref[sl, :] = attn[n].astype(BF16)
        kg_ref[sl, :] = (k[sl] * jnp.exp(g_last - gcum[n])).astype(BF16)
        ds_ref[ci] = jnp.exp(g_last)


def _gdn_prep(u, bsz, seq, cwq, cwk, cwv, alog, dtb, eb, eg):
    ts = GDN_PREP_BLOCK
    ns = seq // ts
    t = bsz * seq
    nc = ts // CHUNK
    specs = []
    for cb in (CB_QB, CB_KB, CB_VB):
        specs += list(_row_specs(ts, ns, t, cb, False))
    specs.append(pl.BlockSpec((ts, LANES), lambda b, i: (b * ns + i, CB128_AB)))
    specs += [_const_spec((4, W_GROUP))] * 3
    specs += [_const_spec((1, LANES))] * 2 + [_const_spec((2, LANES, W_GROUP))] * 2

    def row(b, i):
        return (b * ns + i, 0)

    one_dir_specs = [pl.BlockSpec((ts, W_GROUP), row), pl.BlockSpec((2 * ts, W_GROUP), row),
                     pl.BlockSpec((ts, W_GROUP), row), pl.BlockSpec((ts, W_GROUP), row),
                     pl.BlockSpec((nc, 1, W_GROUP), lambda b, i: (b * ns + i, 0, 0))]
    one_dir_shapes = [jax.ShapeDtypeStruct((t, W_GROUP), F32), jax.ShapeDtypeStruct((2 * t, W_GROUP), BF16),
                      jax.ShapeDtypeStruct((t, W_GROUP), BF16), jax.ShapeDtypeStruct((t, W_GROUP), BF16),
                      jax.ShapeDtypeStruct((t // CHUNK, 1, W_GROUP), F32)]
    return pl.pallas_call(
        _gdn_prep_kernel,
        grid=(bsz, ns),
        in_specs=specs,
        out_specs=one_dir_specs * 2,
        out_shape=one_dir_shapes * 2,
        compiler_params=_params(("parallel", "parallel")),
        name="gdn_prep",
    )(*([u] * 10 + [cwq, cwk, cwv, alog, dtb, eb, eg]))


def _gdn_scan_kernel(uf, wqf, atf, kgf, dsf, ub, wqb, atb, kgb, dsb, of_ref, ob_ref, s_ref):
    bsz, ts, _ = uf.shape
    nc = ts // CHUNK
    c = CHUNK

    @pl.when(pl.program_id(0) == 0)
    def _():
        s_ref[...] = jnp.zeros_like(s_ref)

    blk = _iota((W_GROUP, W_GROUP), 0) // HEAD_DIM == _iota((W_GROUP, W_GROUP), 1) // HEAD_DIM
    ins = ((uf, wqf, atf, kgf, dsf, of_ref), (ub, wqb, atb, kgb, dsb, ob_ref))
    chains = [(d, b) for b in range(bsz) for d in range(2)]
    state = [s_ref[d, b] for d, b in chains]
    for step in range(nc):
        pos = (step, nc - 1 - step)
        ws = [_dot(ins[d][1][b, 2 * pos[d] * c:(2 * pos[d] + 2) * c, :], s.astype(BF16))
              for (d, b), s in zip(chains, state)]
        vnb = [(ins[d][0][b, pos[d] * c:(pos[d] + 1) * c, :] - x[:c]).astype(BF16) for (d, b), x in zip(chains, ws)]
        for (d, b), x, vn in zip(chains, ws, vnb):
            sl = slice(pos[d] * c, (pos[d] + 1) * c)
            ins[d][5][b, sl, :] = x[c:] + _dot(ins[d][2][b, sl, :], _stack4(vn, blk))
        state = [s * ins[d][4][b, pos[d]] + jnp.where(blk, _dot_tn(ins[d][3][b, pos[d] * c:(pos[d] + 1) * c, :], vn), 0.0)
                 for (d, b), s, vn in zip(chains, state, vnb)]
    for (d, b), s in zip(chains, state):
        s_ref[d, b] = s


def _gdn_scan(prep, bsz, seq):
    ts = GDN_SCAN_BLOCK
    ns = seq // ts
    nc = ts // CHUNK
    specs, args = [], []
    for d in range(2):
        tb = (lambda i: ns - 1 - i) if d else (lambda i: i)
        u_, wq, at, kg, ds = prep[5 * d:5 * d + 5]
        args += [u_.reshape(bsz, seq, W_GROUP), wq.reshape(bsz, 2 * seq, W_GROUP), at.reshape(bsz, seq, W_GROUP),
                 kg.reshape(bsz, seq, W_GROUP), ds.reshape(bsz, seq // CHUNK, 1, W_GROUP)]

        def blk3(rows, tb=tb):
            return pl.BlockSpec((bsz, rows, W_GROUP), lambda i: (0, tb(i), 0))

        specs += [blk3(ts), blk3(2 * ts), blk3(ts), blk3(ts),
                  pl.BlockSpec((bsz, nc, 1, W_GROUP), lambda i, tb=tb: (0, tb(i), 0, 0))]
    out_shape = jax.ShapeDtypeStruct((bsz, seq, W_GROUP), F32)
    o_fwd, o_bwd = pl.pallas_call(
        _gdn_scan_kernel,
        grid=(ns,),
        in_specs=specs,
        out_specs=[pl.BlockSpec((bsz, ts, W_GROUP), lambda i: (0, i, 0)),
                   pl.BlockSpec((bsz, ts, W_GROUP), lambda i: (0, ns - 1 - i, 0))],
        out_shape=[out_shape, out_shape],
        scratch_shapes=[pltpu.VMEM((2, bsz, W_GROUP, W_GROUP), F32)],
        compiler_params=_params(("arbitrary",)),
        name="gdn_scan",
    )(*args)
    return o_fwd.reshape(bsz * seq, W_GROUP), o_bwd.reshape(bsz * seq, W_GROUP)


RET_CHUNK = 128
RET_BLOCK = 1024


def _ret_kernel(reverse, *refs):
    if reverse:
        (q_ref, k_ref, v_ref, cos_ref, sin_ref, lgq_ref, lgv_ref, o_ref, s_ref) = refs
    else:
        (q_ref, k_ref, v_ref, cos_ref, sin_ref, lgq_ref, lgv_ref, lgfc_ref, lgbc_ref, g_ref, ob_ref, rn_ref,
         o_ref, s_ref, dm_ref) = refs
    ts = q_ref.shape[0]
    cr = RET_CHUNK
    nc = ts // cr
    nk = N_HEADS * cr

    @pl.when(pl.program_id(1) == 0)
    def _():
        s_ref[...] = jnp.zeros_like(s_ref)
        if not reverse:
            ii = _iota((cr, nk), 0)
            jj = _iota((cr, nk), 1) % cr
            dist = (ii - jj).astype(F32)
            lower, upper = jj <= ii, jj >= ii
            dm_ref[...] = (jnp.where(lower, jnp.exp(jnp.where(lower, dist * lgfc_ref[...], 0.0)), 0.0)
                           + jnp.where(upper, jnp.exp(jnp.where(upper, -dist * lgbc_ref[...], 0.0)), 0.0))

    cos = cos_ref[...]
    sin = sin_ref[...]

    def rope(t):
        t1, t2 = t[:, :LANES], t[:, LANES:]
        return jnp.concatenate([t1 * cos - t2 * sin, t1 * sin + t2 * cos], axis=-1)

    q = rope(q_ref[...])
    k = rope(k_ref[...]) * (HEAD_DIM ** -0.5)
    v = v_ref[...]
    lgq = lgq_ref[...]
    lgv = lgv_ref[...]
    pos = _iota((cr, W_GROUP), 0).astype(F32)
    rows_qk = (_iota((W_GROUP, W_GROUP), 0) % LANES) // 32
    blk_state = rows_qk == _iota((W_GROUP, W_GROUP), 1) // HEAD_DIM
    if reverse:
        q_fac = jnp.exp((cr - pos) * lgq)
        k_fac = jnp.exp(pos * lgq)
    else:
        q_fac = jnp.exp((pos + 1.0) * lgq)
        k_fac = jnp.exp((cr - 1.0 - pos) * lgq)
    s_fac = jnp.exp(cr * lgv)
    sls = [slice(ci * cr, (ci + 1) * cr) for ci in range(nc)]
    kv = [jnp.where(blk_state, _dot_tn((k[sl] * k_fac).astype(BF16), v[sl].astype(BF16)), 0.0) for sl in sls]
    s = s_ref[...]
    cross = [None] * nc
    for ci in (range(nc - 1, -1, -1) if reverse else range(nc)):
        cross[ci] = _dot((q[sls[ci]] * q_fac).astype(BF16), s.astype(BF16))
        s = s * s_fac + kv[ci]
    s_ref[...] = s
    if reverse:
        o_ref[...] = jnp.concatenate(cross, axis=0)
        return
    stack_head = _iota((nk, W_GROUP), 0) // cr
    k_mask = stack_head == (_iota((nk, W_GROUP), 1) % LANES) // 32
    v_mask = stack_head == _iota((nk, W_GROUP), 1) // HEAD_DIM
    qk = [_dot_nt(q[sl].astype(BF16), _stack4(k[sl].astype(BF16), k_mask)) for sl in sls]
    dm = dm_ref[...]
    intra = [_dot((x * dm).astype(BF16), _stack4(v[sl].astype(BF16), v_mask)) for x, sl in zip(qk, sls)]
    o = jnp.concatenate([a + b for a, b in zip(intra, cross)], axis=0) + ob_ref[...]
    ones_h = _head_ones()
    oc = o - _dot_sel(o, ones_h, 3) * (1.0 / HEAD_DIM)
    var = _dot_sel(oc * oc, ones_h, 2) * (1.0 / HEAD_DIM)
    o_ref[...] = (oc * lax.rsqrt(var + EPS) * rn_ref[...] * _silu(g_ref[...])).astype(o_ref.dtype)


def _ret(u, bsz, seq, reverse, cos, sin, lgq, lgv, lgfc=None, lgbc=None, rn=None, ob=None):
    cr = RET_CHUNK
    ts = RET_BLOCK
    ns = seq // ts
    t = bsz * seq
    tbf = (lambda i: ns - 1 - i) if reverse else (lambda i: i)

    def col(cb):
        return pl.BlockSpec((ts, W_GROUP), lambda b, i: (b * ns + tbf(i), cb))

    tab = pl.BlockSpec((ts, LANES), lambda b, i: (tbf(i), 0))
    specs = [col(CB_QC), col(CB_KC), col(CB_VC), tab, tab, _const_spec((1, W_GROUP)), _const_spec((1, W_GROUP))]
    args = [u, u, u, cos, sin, lgq, lgv]
    scratch = [pltpu.VMEM((W_GROUP, W_GROUP), F32)]
    if reverse:
        out_dtype = F32
    else:
        specs += [_const_spec((1, N_HEADS * cr)), _const_spec((1, N_HEADS * cr)), col(CB_GC),
                  pl.BlockSpec((ts, W_GROUP), lambda b, i: (b * ns + i, 0)), _const_spec((1, W_GROUP))]
        args += [lgfc, lgbc, u, ob, rn]
        scratch.append(pltpu.VMEM((cr, N_HEADS * cr), F32))
        out_dtype = BF16
    return pl.pallas_call(
        functools.partial(_ret_kernel, reverse),
        grid=(bsz, ns),
        in_specs=specs,
        out_specs=pl.BlockSpec((ts, W_GROUP), lambda b, i: (b * ns + tbf(i), 0)),
        out_shape=jax.ShapeDtypeStruct((t, W_GROUP), out_dtype),
        scratch_shapes=scratch,
        compiler_params=_params(("parallel", "arbitrary")),
        name="ret_bwd" if reverse else "ret_fwd",
    )(*args)


NA_GROUP = 16
NA_WIN = NA_KH * GRID_W
NA_ROWS_PER_PASS = 4


def _na_kernel(rows, q_ref, kp_ref, kc_ref, kn_ref, vp_ref, vc_ref, vn_ref, tab_ref, o_ref, kw, vw):
    g = pl.program_id(1)
    blk = NA_GROUP * GRID_W
    kw[0:blk, :] = kp_ref[...]
    kw[blk:2 * blk, :] = kc_ref[...]
    kw[2 * blk:3 * blk, :] = kn_ref[...]
    vw[0:blk, :] = vp_ref[...]
    vw[blk:2 * blk, :] = vc_ref[...]
    vw[2 * blk:3 * blk, :] = vn_ref[...]
    row_head = _iota((N_HEADS * GRID_W, W_GROUP), 0) // GRID_W
    lane_head = _iota((N_HEADS * GRID_W, W_GROUP), 1) // HEAD_DIM
    blk = row_head == lane_head
    out_head = _iota((GRID_W, W_GROUP), 1) // HEAD_DIM
    for j0 in range(0, NA_GROUP, NA_ROWS_PER_PASS):
        js = list(range(j0, j0 + NA_ROWS_PER_PASS))
        vwins, sc = [], []
        for j in js:
            r = g * NA_GROUP + j
            r0 = jnp.clip(r - NA_KH // 2, 0, rows - NA_KH)
            off = pl.multiple_of((r0 - g * NA_GROUP + NA_GROUP) * GRID_W, GRID_W)
            qst = _stack4(q_ref[j * GRID_W:(j + 1) * GRID_W, :], blk)
            vwins.append(vw[pl.ds(off, NA_WIN), :])
            sc.append(_dot_nt(qst, kw[pl.ds(off, NA_WIN), :]) + tab_ref[r - r0])
        mx = [jnp.max(s, axis=-1, keepdims=True) for s in sc]
        pr = [jnp.exp(s - m) for s, m in zip(sc, mx)]
        den = [jnp.sum(p, axis=-1, keepdims=True) for p in pr]
        pv = [_dot(p.astype(BF16), vw_) / l for p, vw_, l in zip(pr, vwins, den)]
        for j, o in zip(js, pv):
            res = o[0:GRID_W]
            for h in range(1, N_HEADS):
                res = jnp.where(out_head == h, o[h * GRID_W:(h + 1) * GRID_W], res)
            o_ref[j * GRID_W:(j + 1) * GRID_W, :] = res.astype(o_ref.dtype)


def _na(qn, kn, vn, table, bsz, seq):
    rows = seq // GRID_W
    ng = rows // NA_GROUP
    blk = NA_GROUP * GRID_W
    t = bsz * seq
    cur = pl.BlockSpec((blk, W_GROUP), lambda b, g: (b * ng + g, 0))
    prev = pl.BlockSpec((blk, W_GROUP), lambda b, g: (b * ng + jnp.maximum(g - 1, 0), 0))
    nxt = pl.BlockSpec((blk, W_GROUP), lambda b, g: (b * ng + jnp.minimum(g + 1, ng - 1), 0))
    return pl.pallas_call(
        functools.partial(_na_kernel, rows),
        grid=(bsz, ng),
        in_specs=[cur, prev, cur, nxt, prev, cur, nxt, _const_spec((NA_KH, N_HEADS * GRID_W, NA_WIN))],
        out_specs=cur,
        out_shape=jax.ShapeDtypeStruct((t, W_GROUP), BF16),
        scratch_shapes=[pltpu.VMEM((3 * blk, W_GROUP), BF16), pltpu.VMEM((3 * blk, W_GROUP), BF16)],
        compiler_params=_params(("parallel", "parallel")),
        name="na",
    )(qn, kn, kn, kn, vn, vn, vn, table)


def _na_table(rpb):
    didx = np.arange(NA_KH)[:, None]
    kk = np.arange(NA_KH)[None, :]
    dr = (kk + (NA_KH - 1) - didx).reshape(-1)
    cc = np.arange(GRID_W)[:, None]
    kc = np.arange(GRID_W)[None, :]
    c0 = np.clip(cc - NA_KW // 2, 0, GRID_W - NA_KW)
    valid = (kc >= c0) & (kc < c0 + NA_KW)
    dc = np.clip(kc - cc + (NA_KW - 1), 0, 2 * NA_KW - 2).reshape(-1)
    row_sel = np.zeros((dr.size, 2 * NA_KH - 1), np.float32)
    row_sel[np.arange(dr.size), dr] = 1.0
    col_sel = np.zeros((2 * NA_KW - 1, dc.size), np.float32)
    col_sel[dc, np.arange(dc.size)] = 1.0
    bias = jnp.einsum('ar,hrs,sb->hab', row_sel, rpb, col_sel, precision=lax.Precision.HIGHEST)
    bias = bias.reshape(N_HEADS, NA_KH, NA_KH, GRID_W, GRID_W)
    tab = jnp.where(valid[None, None, None], bias, NEG_BIG)
    return jnp.transpose(tab, (1, 0, 3, 2, 4)).reshape(NA_KH, N_HEADS * GRID_W, NA_WIN)


FF_CHUNK = 256
POST_HALO = 16


def _mix_residual(x, a, bf, bb, z, gn, c, d, w_ref):
    ob = bf + bb
    ms = _dot_sel(ob * ob, _head_ones(), 2) * (1.0 / HEAD_DIM)
    out_b = (ob * lax.rsqrt(ms + EPS) * gn * _silu(z)).astype(BF16)
    acc = x
    for gi, m in enumerate((a, out_b, c, d)):
        acc = acc + _dot(m, w_ref[gi * W_GROUP:(gi + 1) * W_GROUP, :])
    return acc


def _post_kernel(*refs):
    streams = [refs[3 * n:3 * n + 3] for n in range(7)]
    (p_ref, gn_ref, wo_ref, n2_ref, wg_ref, wu_ref, cw_ref, cb_ref, wd_ref, n3_ref, pg_ref, pp_ref,
     o_ref, act_s) = refs[21:]
    tm = streams[0][0].shape[0]
    i = pl.program_id(1)
    first = i == 0
    last = i == pl.num_programs(1) - 1
    gn = gn_ref[...]
    mains = [s[0][...] for s in streams]
    halos = [jnp.concatenate([s[1][...], s[2][...]], axis=0) for s in streams]
    x = _mix_residual(mains[0], *mains[1:5], gn, *mains[5:], wo_ref)
    xh = _mix_residual(halos[0], *halos[1:5], gn, *halos[5:], wo_ref)
    n2 = n2_ref[...]
    h2f = _rms(x, n2)
    h2h = _rms(xh, n2)
    h_before = jnp.where(first, 0.0, h2h[POST_HALO - SUBLANES:POST_HALO])
    h_after = jnp.where(last, 0.0, h2h[POST_HALO:POST_HALO + SUBLANES])
    h2 = h2f.astype(BF16)
    h2e = jnp.concatenate([h_before, h2f, h_after], axis=0).astype(BF16)
    for f in range(D_FF // FF_CHUNK):
        fs = slice(f * FF_CHUNK, (f + 1) * FF_CHUNK)
        ge = _dot(h2e, wg_ref[:, fs])
        ne = tm + 2 * SUBLANES
        g_prev = pltpu.roll(ge, 1, 0)[SUBLANES:SUBLANES + tm]
        g_next = pltpu.roll(ge, ne - 1, 0)[SUBLANES:SUBLANES + tm]
        gate = (cw_ref[0:1, fs] * g_prev + cw_ref[1:2, fs] * ge[SUBLANES:SUBLANES + tm]
                + cw_ref[2:3, fs] * g_next + cb_ref[:, fs])
        up = _dot(h2, wu_ref[:, fs])
        act_s[:, fs] = (_gelu_tanh(gate) * up).astype(BF16)
    acc = x + _dot(act_s[...], wd_ref[...])
    h3 = _rms(acc, n3_ref[...]).astype(BF16)
    gate3 = _sigmoid(_dot(h3, pg_ref[...]))
    o_ref[...] = acc + gate3 * _dot(p_ref[...].astype(BF16), pp_ref[...])


def _post(x, u, oa, obf, obb, gn, oc, od, w_out, p, bsz, seq, n2, wg, wu, cw, cb, wd, n3, pg, pp):
    tm = 512
    ns = seq // tm
    t = bsz * seq
    specs = list(_row_specs(tm, ns, t, 0, False, width=D_MODEL, halo=POST_HALO))
    arrays = [x] * 3
    for arr, cb_ in ((oa, 0), (obf, 0), (obb, 0), (u, CB_ZB), (oc, 0), (od, 0)):
        specs += list(_row_specs(tm, ns, t, cb_, False, halo=POST_HALO))
        arrays += [arr] * 3
    specs += [pl.BlockSpec((tm, PLE_DIM), lambda b, i: (b * ns + i, 0)),
              _const_spec((1, W_GROUP)), _const_spec((D_MODEL, D_MODEL)),
              _const_spec((1, D_MODEL)), _const_spec((D_MODEL, D_FF)), _const_spec((D_MODEL, D_FF)),
              _const_spec((3, D_FF)), _const_spec((1, D_FF)), _const_spec((D_FF, D_MODEL)),
              _const_spec((1, D_MODEL)), _const_spec((D_MODEL, D_MODEL)), _const_spec((PLE_DIM, D_MODEL))]
    return pl.pallas_call(
        _post_kernel,
        grid=(bsz, ns),
        in_specs=specs,
        out_specs=pl.BlockSpec((tm, D_MODEL), lambda b, i: (b * ns + i, 0)),
        out_shape=jax.ShapeDtypeStruct((t, D_MODEL), F32),
        compiler_params=_params(("parallel", "parallel")),
        scratch_shapes=[pltpu.VMEM((tm, D_FF), BF16)],
        name="post",
    )(*arrays, p, gn, w_out, n2, wg, wu, cw, cb, wd, n3, pg, pp)


def _block_diag(w):
    h, n, _ = w.shape
    eye = jnp.eye(h, dtype=w.dtype)
    return (eye[:, None, :, None] * w[:, :, None, :]).reshape(h * n, h * n)


def _row(v):
    return v.reshape(1, -1).astype(F32)


def _lane_pad(v8, offset):
    return jnp.zeros((1, LANES), F32).at[0, offset:offset + 2 * N_HEADS].set(v8.reshape(-1).astype(F32))


def _expander(offset):
    e = np.zeros((LANES, W_GROUP), np.float32)
    for h in range(N_HEADS):
        e[offset + h, h * HEAD_DIM:(h + 1) * HEAD_DIM] = 1.0
    return jnp.asarray(e, BF16)


def _layer(x, p, lw, bsz, seq):
    w_in = _reorder_in_cols(lw['w_in'].astype(BF16))
    u, qn, kn, vn = _inproj(x, _row(lw['norm1']), w_in,
                            _row(jnp.tile(lw['na_qnorm'], N_HEADS)), _row(jnp.tile(lw['na_knorm'], N_HEADS)))

    cw_a, cb_a = lw['conv_a_w'].astype(F32), _row(lw['conv_a_b'])
    h_dir = None
    for d in (1, 0):
        h_dir = _rglru(u, bsz, seq, d == 1, cw_a, cb_a,
                       _block_diag(lw['lru_wr'][d]).astype(BF16), _row(lw['lru_br'][d]),
                       _block_diag(lw['lru_wi'][d]).astype(BF16), _row(lw['lru_bi'][d]),
                       _row(lw['lru_lambda'][d]), hb=h_dir)
    out_a = h_dir

    gconv = lw['gdn_conv'].astype(F32)
    cwq, cwk, cwv = gconv[:, :W_GROUP], gconv[:, W_GROUP:2 * W_GROUP], gconv[:, 2 * W_GROUP:]
    alog = _lane_pad(lw['gdn_a_log'], 2 * N_HEADS)
    dtb = _lane_pad(lw['gdn_dt_bias'], 2 * N_HEADS)
    gn = _row(jnp.tile(lw['gdn_norm'], N_HEADS))
    eb = jnp.stack([_expander(d * N_HEADS) for d in range(2)])
    eg = jnp.stack([_expander(2 * N_HEADS + d * N_HEADS) for d in range(2)])
    o_b_fwd, o_b_bwd = _gdn_scan(_gdn_prep(u, bsz, seq, cwq, cwk, cwv, alog, dtb, eb, eg), bsz, seq)

    half = HEAD_DIM // 2
    inv_freq = ROPE_BASE ** (-jnp.arange(half, dtype=F32) / half)
    ang = jnp.arange(seq, dtype=F32)[:, None] * inv_freq[None, :]
    cos = jnp.tile(jnp.cos(ang), (1, N_HEADS))
    sin = jnp.tile(jnp.sin(ang), (1, N_HEADS))
    log_gamma = jnp.log1p(-jnp.exp2(-lw['ret_decay'].astype(F32)))
    rn = _row(jnp.tile(lw['ret_norm'], N_HEADS))
    o_dir = None
    for d in (1, 0):
        lg = log_gamma[d]
        lgq = _row(jnp.tile(jnp.repeat(lg, half), 2))
        lgv = _row(jnp.repeat(lg, HEAD_DIM))
        if d == 1:
            o_dir = _ret(u, bsz, seq, True, cos, sin, lgq, lgv)
        else:
            o_dir = _ret(u, bsz, seq, False, cos, sin, lgq, lgv,
                         lgfc=_row(jnp.repeat(log_gamma[0], RET_CHUNK)), lgbc=_row(jnp.repeat(log_gamma[1], RET_CHUNK)),
                         rn=rn, ob=o_dir)
    out_c = o_dir

    out_d = _na(qn, kn, vn, _na_table(lw['na_rpb'].astype(F32)), bsz, seq)

    return _post(x, u, out_a, o_b_fwd, o_b_bwd, gn, out_c, out_d, lw['w_out'].astype(BF16), p, bsz, seq,
                 _row(lw['norm2']), lw['ffn_wg'].astype(BF16), lw['ffn_wu'].astype(BF16),
                 lw['ffn_conv_w'].astype(F32), _row(lw['ffn_conv_b']), lw['ffn_wd'].astype(BF16),
                 _row(lw['norm3']), lw['ple_gate'].astype(BF16), lw['ple_proj'].astype(BF16))


_WEIGHT_NAMES = ('norm1', 'norm2', 'norm3', 'w_in', 'w_out', 'conv_a_w', 'conv_a_b', 'lru_wr', 'lru_br',
                 'lru_wi', 'lru_bi', 'lru_lambda', 'gdn_conv', 'gdn_a_log', 'gdn_dt_bias', 'gdn_norm',
                 'ret_decay', 'ret_norm', 'na_qnorm', 'na_knorm', 'na_rpb', 'ffn_wg', 'ffn_wu',
                 'ffn_conv_w', 'ffn_conv_b', 'ffn_wd', 'ple_proj', 'ple_gate')


def _trunk(x, p, weights):
    bsz, seq, _ = x.shape
    xf = x.reshape(bsz * seq, D_MODEL)
    for i in range(DEPTH):
        lw = {n: w[i] for n, w in weights.items()}
        xf = _layer(xf, p[i].reshape(bsz * seq, PLE_DIM), lw, bsz, seq)
    return xf.reshape(bsz, seq, D_MODEL)


def kernel(x_prompt, x_sample, p_prompt, p_sample, norm1, norm2, norm3, w_in, w_out, conv_a_w, conv_a_b, lru_wr, lru_br, lru_wi, lru_bi, lru_lambda, gdn_conv, gdn_a_log, gdn_dt_bias, gdn_norm, ret_decay, ret_norm, na_qnorm, na_knorm, na_rpb, ffn_wg, ffn_wu, ffn_conv_w, ffn_conv_b, ffn_wd, ple_proj, ple_gate):
    weights = dict(zip(_WEIGHT_NAMES, (norm1, norm2, norm3, w_in, w_out, conv_a_w, conv_a_b, lru_wr, lru_br,
                                       lru_wi, lru_bi, lru_lambda, gdn_conv, gdn_a_log, gdn_dt_bias, gdn_norm,
                                       ret_decay, ret_norm, na_qnorm, na_knorm, na_rpb, ffn_wg, ffn_wu,
                                       ffn_conv_w, ffn_conv_b, ffn_wd, ple_proj, ple_gate)))
    return (_trunk(x_prompt, p_prompt, weights), _trunk(x_sample, p_sample, weights))
```

```python
import functools

import jax
import jax.numpy as jnp
import numpy as np
from jax import lax
from jax.experimental import pallas as pl
from jax.experimental.pallas import tpu as pltpu

F32 = jnp.float32
BF16 = jnp.bfloat16

D_MODEL = 1024
DEPTH = 2
PLE_DIM = 256
GRID_W = 64
W_GROUP = 256
HEAD_DIM = 64
N_HEADS = 4
LRU_C = 8.0
CHUNK = 64
ROPE_BASE = 10000.0
NA_KH = 8
NA_KW = 16
D_FF = 2816
EPS = 1e-6
NEG_BIG = -1e30

SUBLANES = 8
LANES = 128
VMEM_LIMIT = 56 * 1024 * 1024

CB_XA, CB_YA, CB_QB, CB_KB, CB_VB, CB_ZB, CB_QC, CB_KC, CB_VC, CB_GC = range(10)
CB128_AB = 20
U_COLS = 10 * W_GROUP + LANES
IN_COLS_PAD = U_COLS + 3 * W_GROUP
IN_COLS = 13 * W_GROUP + 4 * N_HEADS


def _reorder_in_cols(w):
    rows = w.shape[0]
    half = HEAD_DIM // 2

    def halves_first(cols):
        return cols.reshape(rows, N_HEADS, 2, half).transpose(0, 2, 1, 3).reshape(rows, W_GROUP)

    ab0 = 6 * W_GROUP
    qc0 = ab0 + 4 * N_HEADS
    qd0 = qc0 + 4 * W_GROUP
    parts = [w[:, :ab0], halves_first(w[:, qc0:qc0 + W_GROUP]), halves_first(w[:, qc0 + W_GROUP:qc0 + 2 * W_GROUP]),
             w[:, qc0 + 2 * W_GROUP:qd0], w[:, ab0:qc0], jnp.zeros((rows, LANES - 4 * N_HEADS), w.dtype), w[:, qd0:]]
    out = jnp.concatenate(parts, axis=1)
    assert out.shape[1] == IN_COLS_PAD
    return out


def _sigmoid(x):
    return 1.0 / (1.0 + jnp.exp(-x))


def _silu(x):
    return x * _sigmoid(x)


def _softplus(x):
    return jnp.maximum(x, 0.0) + jnp.log1p(jnp.exp(-jnp.abs(x)))


def _gelu_tanh(x):
    return x * (0.5 * (1.0 + jnp.tanh(0.7978845608028654 * (x + 0.044715 * (x * x * x)))))


def _rms(x, gain):
    ms = jnp.mean(x * x, axis=-1, keepdims=True)
    return x * lax.rsqrt(ms + EPS) * gain


def _dot(a, b):
    return jnp.dot(a, b, preferred_element_type=F32)


def _split_bf16(x, terms):
    parts = []
    for _ in range(terms):
        p = x.astype(BF16)
        parts.append(p)
        x = x - p.astype(F32)
    return parts


def _dot_sel(x, sel, terms):
    return sum(_dot(p, sel) for p in _split_bf16(x, terms))


def _sel_dot(sel, x, terms):
    return sum(_dot(sel, p) for p in _split_bf16(x, terms))


def _dot_nt(a, b):
    return lax.dot_general(a, b, (((1,), (1,)), ((), ())), preferred_element_type=F32)


def _dot_tn(a, b):
    return lax.dot_general(a, b, (((0,), (0,)), ((), ())), preferred_element_type=F32)


def _iota(shape, dim):
    return lax.broadcasted_iota(jnp.int32, shape, dim)


def _head_ones():
    return (_iota((W_GROUP, W_GROUP), 0) // HEAD_DIM == _iota((W_GROUP, W_GROUP), 1) // HEAD_DIM).astype(BF16)


def _stack4(x, mask):
    return jnp.where(mask, jnp.concatenate([x] * N_HEADS, axis=0), jnp.zeros((), x.dtype))


def _conv4(xm, xp, xn, w_ref, first, last):
    ts = xm.shape[0]
    xp = jnp.where(first, 0.0, xp)
    xn = jnp.where(last, 0.0, xn)
    xe = jnp.concatenate([xp, xm, xn], axis=0)
    n = ts + 2 * SUBLANES

    def tap(offset):
        rolled = xe if offset == 0 else pltpu.roll(xe, (-offset) % n, 0)
        return rolled[SUBLANES:SUBLANES + ts]

    out = w_ref[0:1, :] * tap(-2)
    out = out + w_ref[1:2, :] * tap(-1)
    out = out + w_ref[2:3, :] * tap(0)
    out = out + w_ref[3:4, :] * tap(1)
    return out


def _time_block(reverse):
    i = pl.program_id(1)
    n = pl.num_programs(1)
    tb = (n - 1 - i) if reverse else i
    return tb, tb == 0, tb == n - 1


def _row_specs(ts, ns, total_rows, col_block, reverse, width=W_GROUP, halo=SUBLANES):
    per_block = ts // halo
    last = total_rows // halo - 1

    def tb(i):
        return (ns - 1 - i) if reverse else i

    main = pl.BlockSpec((ts, width), lambda b, i: (b * ns + tb(i), col_block))
    prev = pl.BlockSpec((halo, width), lambda b, i: (jnp.maximum((b * ns + tb(i)) * per_block - 1, 0), col_block))
    nxt = pl.BlockSpec((halo, width), lambda b, i: (jnp.minimum((b * ns + tb(i) + 1) * per_block, last), col_block))
    return main, prev, nxt


def _const_spec(shape):
    nd = len(shape)
    return pl.BlockSpec(shape, lambda *_: (0,) * nd, pipeline_mode=pl.Buffered(1))


def _params(sem):
    return pltpu.CompilerParams(dimension_semantics=sem, vmem_limit_bytes=VMEM_LIMIT)


def _inproj_kernel(x_ref, g_ref, w_ref, gq_ref, gk_ref, u_ref, qd_ref, kd_ref, vd_ref):
    h = _rms(x_ref[...], g_ref[...]).astype(BF16)
    u_ref[...] = _dot(h, w_ref[:, :U_COLS])
    qkv = _dot(h, w_ref[:, U_COLS:])
    q, k = qkv[:, :W_GROUP], qkv[:, W_GROUP:2 * W_GROUP]
    ones_h = _head_ones()
    inv = 1.0 / HEAD_DIM
    qd_ref[...] = (q * lax.rsqrt(_dot_sel(q * q, ones_h, 2) * inv + EPS) * gq_ref[...] * (HEAD_DIM ** -0.5)).astype(BF16)
    kd_ref[...] = (k * lax.rsqrt(_dot_sel(k * k, ones_h, 2) * inv + EPS) * gk_ref[...]).astype(BF16)
    vd_ref[...] = qkv[:, 2 * W_GROUP:].astype(BF16)


def _inproj(x, gain, w, gq, gk):
    t = x.shape[0]
    tm = 1024
    grp = pl.BlockSpec((tm, W_GROUP), lambda i: (i, 0))
    grp_shape = jax.ShapeDtypeStruct((t, W_GROUP), BF16)
    return pl.pallas_call(
        _inproj_kernel,
        grid=(t // tm,),
        in_specs=[pl.BlockSpec((tm, D_MODEL), lambda i: (i, 0)),
                  _const_spec((1, D_MODEL)),
                  _const_spec((D_MODEL, IN_COLS_PAD)),
                  _const_spec((1, W_GROUP)), _const_spec((1, W_GROUP))],
        out_specs=[pl.BlockSpec((tm, U_COLS), lambda i: (i, 0)), grp, grp, grp],
        out_shape=[jax.ShapeDtypeStruct((t, U_COLS), F32), grp_shape, grp_shape, grp_shape],
        compiler_params=_params(("parallel",)),
        name="inproj",
    )(x, gain, w, gq, gk)


def _rglru_kernel(reverse, *refs):
    if reverse:
        (xm_ref, xp_ref, xn_ref, cw_ref, cb_ref, wr_ref, br_ref, wi_ref, bi_ref, lam_ref,
         o_ref, a_s, b_s, p_s, ac_s, c_s, carry, h_s) = refs
    else:
        (xm_ref, xp_ref, xn_ref, cw_ref, cb_ref, wr_ref, br_ref, wi_ref, bi_ref, lam_ref, ya_ref, hb_ref,
         o_ref, a_s, b_s, p_s, ac_s, c_s, carry, h_s) = refs
    ts = xm_ref.shape[0]
    ng = ts // SUBLANES
    tb, first, last = _time_block(reverse)

    @pl.when(pl.program_id(1) == 0)
    def _():
        carry[...] = jnp.zeros_like(carry)

    x = _conv4(xm_ref[...], xp_ref[...], xn_ref[...], cw_ref, first, last) + cb_ref[...]
    xb = x.astype(BF16)
    r = _sigmoid(_dot(xb, wr_ref[...]) + br_ref[...])
    ig = _sigmoid(_dot(xb, wi_ref[...]) + bi_ref[...])
    log_a = (-LRU_C) * r * _softplus(-lam_ref[...])
    a = jnp.exp(log_a)
    th = jnp.tanh(-log_a)
    bx = jnp.sqrt(2.0 * th / (1.0 + th)) * (ig * x)
    nl = W_GROUP // LANES
    for hf in range(nl):
        a_s[hf] = a[:, hf * LANES:(hf + 1) * LANES]
        b_s[hf] = bx[:, hf * LANES:(hf + 1) * LANES]

    def slab(ref, k):
        return jnp.concatenate([ref[hf, pl.ds(k, ng, stride=SUBLANES), :] for hf in range(nl)], axis=-1)

    order = list(range(SUBLANES))[::-1] if reverse else list(range(SUBLANES))
    p = None
    ac = None
    for k in order:
        a_k = slab(a_s, k)
        b_k = slab(b_s, k)
        if p is None:
            p, ac = b_k, a_k
        else:
            p = a_k * p + b_k
            ac = a_k * ac
        p_s[k] = p
        ac_s[k] = ac

    k_last = order[-1]

    def body(g, c):
        gi = (ng - 1 - g) if reverse else g
        c_s[pl.ds(gi, 1), :] = c
        return ac_s[k_last, pl.ds(gi, 1), :] * c + p_s[k_last, pl.ds(gi, 1), :]

    carry[...] = lax.fori_loop(0, ng, body, carry[...], unroll=8)
    c_in = c_s[...]
    for k in order:
        h_k = p_s[k] + ac_s[k] * c_in
        for hf in range(nl):
            h_s[hf, pl.ds(k, ng, stride=SUBLANES), :] = h_k[:, hf * LANES:(hf + 1) * LANES]
    h = jnp.concatenate([h_s[hf] for hf in range(nl)], axis=-1)
    if reverse:
        o_ref[...] = h
    else:
        o_ref[...] = ((h + hb_ref[...]) * _gelu_tanh(ya_ref[...])).astype(o_ref.dtype)


def _rglru(u, bsz, seq, reverse, cw, cb, wr, br, wi, bi, lam, hb=None):
    ts = 1024
    ns = seq // ts
    t = bsz * seq
    ng = ts // SUBLANES
    xm, xp, xn = _row_specs(ts, ns, t, CB_XA, reverse)
    in_specs = [xm, xp, xn, _const_spec((4, W_GROUP)), _const_spec((1, W_GROUP)),
                _const_spec((W_GROUP, W_GROUP)), _const_spec((1, W_GROUP)),
                _const_spec((W_GROUP, W_GROUP)), _const_spec((1, W_GROUP)), _const_spec((1, W_GROUP))]
    args = [u, u, u, cw, cb, wr, br, wi, bi, lam]
    halves = (W_GROUP // LANES, ts, LANES)
    scratch = [pltpu.VMEM(halves, F32), pltpu.VMEM(halves, F32),
               pltpu.VMEM((SUBLANES, ng, W_GROUP), F32), pltpu.VMEM((SUBLANES, ng, W_GROUP), F32),
               pltpu.VMEM((ng, W_GROUP), F32), pltpu.VMEM((1, W_GROUP), F32), pltpu.VMEM(halves, F32)]
    if reverse:
        out_dtype = F32
        out_spec = pl.BlockSpec((ts, W_GROUP), lambda b, i: (b * ns + ns - 1 - i, 0))
    else:
        ya = _row_specs(ts, ns, t, CB_YA, False)[0]
        in_specs += [ya, pl.BlockSpec((ts, W_GROUP), lambda b, i: (b * ns + i, 0))]
        args += [u, hb]
        out_dtype = BF16
        out_spec = pl.BlockSpec((ts, W_GROUP), lambda b, i: (b * ns + i, 0))
    return pl.pallas_call(
        functools.partial(_rglru_kernel, reverse),
        grid=(bsz, ns),
        in_specs=in_specs,
        out_specs=out_spec,
        out_shape=jax.ShapeDtypeStruct((t, W_GROUP), out_dtype),
        scratch_shapes=scratch,
        compiler_params=_params(("parallel", "arbitrary")),
        name="rglru_bwd" if reverse else "rglru_fwd",
    )(*args)


GDN_PREP_BLOCK = 512
GDN_SCAN_BLOCK = 256


def _gdn_prep_kernel(qm, qp, qn, km, kp, kn, vm, vp, vn, ab_ref, cwq, cwk, cwv, alog_ref, dtb_ref, eb_ref, eg_ref,
                     *outs):
    ts = qm.shape[0]
    nc = ts // CHUNK
    c = CHUNK
    i = pl.program_id(1)
    first = i == 0
    last = i == pl.num_programs(1) - 1
    out_refs = (outs[0:5], outs[5:10])

    ones_h = _head_ones()
    q = _silu(_conv4(qm[...], qp[...], qn[...], cwq, first, last))
    k = _silu(_conv4(km[...], kp[...], kn[...], cwk, first, last))
    v = _silu(_conv4(vm[...], vp[...], vn[...], cwv, first, last))
    q = q * lax.rsqrt(_dot_sel(q * q, ones_h, 2) + EPS) * (HEAD_DIM ** -0.5)
    k = k * lax.rsqrt(_dot_sel(k * k, ones_h, 2) + EPS)
    ab = ab_ref[...]
    sig_parts = _split_bf16(_sigmoid(ab), 2)
    g_parts = _split_bf16(-jnp.exp(alog_ref[...]) * _softplus(ab + dtb_ref[...]), 3)
    beta = [sum(_dot(p, eb_ref[d]) for p in sig_parts) for d in range(2)]
    gate = [sum(_dot(p, eg_ref[d]) for p in g_parts) for d in range(2)]

    ii = _iota((c, W_GROUP), 0)
    jj = _iota((c, W_GROUP), 1) % c
    r2 = _iota((c, c), 0)
    c2 = _iota((c, c), 1)
    incl = (jj <= ii, jj >= ii)
    strict = (jj < ii, jj > ii)
    tri = ((c2 <= r2).astype(BF16), (c2 >= r2).astype(BF16))
    ucat = ((ii > jj).astype(F32), (ii < jj).astype(F32))
    eye_cat = (ii == jj).astype(F32)
    blk = _iota((W_GROUP, W_GROUP), 0) // HEAD_DIM == _iota((W_GROUP, W_GROUP), 1) // HEAD_DIM

    sls = [slice(ci * c, (ci + 1) * c) for ci in range(nc)]
    kq = [_dot_nt(jnp.concatenate([k[sl], q[sl]], axis=0).astype(BF16), _stack4(k[sl].astype(BF16), blk))
          for sl in sls]
    units = [(d, ci) for ci in range(nc) for d in range(2)]
    gd = [_sel_dot(tri[d], jnp.concatenate([gate[d][sls[ci]], gate[d][sls[ci]] * ucat[d]], axis=1), 3)
          for d, ci in units]
    gcum = [x[:, :W_GROUP] for x in gd]
    decay = [jnp.where(incl[d], jnp.exp(jnp.where(incl[d], x[:, W_GROUP:], 0.0)), 0.0)
             for (d, ci), x in zip(units, gd)]
    lmat = [jnp.where(strict[d], beta[d][sls[ci]] * kq[ci][:c] * dec, 0.0) for (d, ci), dec in zip(units, decay)]
    attn = [kq[ci][c:] * dec for (d, ci), dec in zip(units, decay)]
    pinv = [eye_cat - l for l in lmat]
    lb = [l.astype(BF16) for l in lmat]
    lpow = [_dot(x, _stack4(x, blk)) for x in lb]
    for lvl in range(5):
        z = [_stack4(x.astype(BF16), blk) for x in lpow]
        if lvl < 4:
            r = [_dot(jnp.concatenate([p, lp], axis=0).astype(BF16), zz) for p, lp, zz in zip(pinv, lpow, z)]
            pinv = [p + x[:c] for p, x in zip(pinv, r)]
            lpow = [x[c:] for x in r]
        else:
            pinv = [p + _dot(p.astype(BF16), zz) for p, zz in zip(pinv, z)]
    egc = [jnp.exp(x) for x in gcum]
    pb = [p.astype(BF16) for p in pinv]
    for n, (d, ci) in enumerate(units):
        sl = sls[ci]
        u_ref, wq_ref, at_ref, kg_ref, ds_ref = out_refs[d]
        bc = beta[d][sl]
        u_ref[sl, :] = _dot(pb[n], _stack4((v[sl] * bc).astype(BF16), blk))
        w_key = _dot(pb[n], _stack4((k[sl] * bc * egc[n]).astype(BF16), blk))
        g_last = gcum[n][0:1, :] if d else gcum[n][c - 1:c, :]
        wq_ref[2 * ci * c:(2 * ci + 1) * c, :] = w_key.astype(BF16)
        wq_ref[(2 * ci + 1) * c:(2 * ci + 2) * c, :] = (q[sl] * egc[n]).astype(BF16)
        at_ref[sl, :] = attn[n].astype(BF16)
        kg_ref[sl, :] = (k[sl] * jnp.exp(g_last - gcum[n])).astype(BF16)
        ds_ref[ci] = jnp.exp(g_last)


def _gdn_prep(u, bsz, seq, cwq, cwk, cwv, alog, dtb, eb, eg):
    ts = GDN_PREP_BLOCK
    ns = seq // ts
    t = bsz * seq
    nc = ts // CHUNK
    specs = []
    for cb in (CB_QB, CB_KB, CB_VB):
        specs += list(_row_specs(ts, ns, t, cb, False))
    specs.append(pl.BlockSpec((ts, LANES), lambda b, i: (b * ns + i, CB128_AB)))
    specs += [_const_spec((4, W_GROUP))] * 3
    specs += [_const_spec((1, LANES))] * 2 + [_const_spec((2, LANES, W_GROUP))] * 2

    def row(b, i):
        return (b * ns + i, 0)

    one_dir_specs = [pl.BlockSpec((ts, W_GROUP), row), pl.BlockSpec((2 * ts, W_GROUP), row),
                     pl.BlockSpec((ts, W_GROUP), row), pl.BlockSpec((ts, W_GROUP), row),
                     pl.BlockSpec((nc, 1, W_GROUP), lambda b, i: (b * ns + i, 0, 0))]
    one_dir_shapes = [jax.ShapeDtypeStruct((t, W_GROUP), F32), jax.ShapeDtypeStruct((2 * t, W_GROUP), BF16),
                      jax.ShapeDtypeStruct((t, W_GROUP), BF16), jax.ShapeDtypeStruct((t, W_GROUP), BF16),
                      jax.ShapeDtypeStruct((t // CHUNK, 1, W_GROUP), F32)]
    return pl.pallas_call(
        _gdn_prep_kernel,
        grid=(bsz, ns),
        in_specs=specs,
        out_specs=one_dir_specs * 2,
        out_shape=one_dir_shapes * 2,
        compiler_params=_params(("parallel", "parallel")),
        name="gdn_prep",
    )(*([u] * 10 + [cwq, cwk, cwv, alog, dtb, eb, eg]))


def _gdn_scan_kernel(group_sizes, periods, *refs):
    n_groups = len(group_sizes)
    ins = refs[:10 * n_groups]
    outs = refs[10 * n_groups:12 * n_groups]
    s_ref = refs[12 * n_groups]
    ts = ins[0].shape[1]
    nc = ts // CHUNK
    c = CHUNK

    chains = []
    for gi, nb in enumerate(group_sizes):
        first_chain = len(chains)
        for b in range(nb):
            for d in range(2):
                chains.append(tuple(ins[10 * gi + 5 * d:10 * gi + 5 * d + 5]) + (outs[2 * gi + d], d, b))

        @pl.when(pl.program_id(0) % periods[gi] == 0)
        def _(lo=first_chain, hi=len(chains)):
            s_ref[lo:hi] = jnp.zeros((hi - lo, W_GROUP, W_GROUP), F32)

    blk = _iota((W_GROUP, W_GROUP), 0) // HEAD_DIM == _iota((W_GROUP, W_GROUP), 1) // HEAD_DIM
    state = [s_ref[n] for n in range(len(chains))]
    for step in range(nc):
        pos = (step, nc - 1 - step)
        ws = [_dot(wq[b, 2 * pos[d] * c:(2 * pos[d] + 2) * c, :], s.astype(BF16))
              for (u_, wq, at, kg, ds, o, d, b), s in zip(chains, state)]
        vnb = [(u_[b, pos[d] * c:(pos[d] + 1) * c, :] - x[:c]).astype(BF16)
               for (u_, wq, at, kg, ds, o, d, b), x in zip(chains, ws)]
        for (u_, wq, at, kg, ds, o, d, b), x, vn in zip(chains, ws, vnb):
            sl = slice(pos[d] * c, (pos[d] + 1) * c)
            o[b, sl, :] = x[c:] + _dot(at[b, sl, :], _stack4(vn, blk))
        state = [s * ds[b, pos[d]] + jnp.where(blk, _dot_tn(kg[b, pos[d] * c:(pos[d] + 1) * c, :], vn), 0.0)
                 for (u_, wq, at, kg, ds, o, d, b), s, vn in zip(chains, state, vnb)]
    for n, s in enumerate(state):
        s_ref[n] = s


def _gdn_scan(trunks):
    ts = GDN_SCAN_BLOCK
    nc = ts // CHUNK
    n_steps = max(seq // ts for _, _, seq in trunks)
    specs, args, out_specs, out_shapes, group_sizes, periods = [], [], [], [], [], []
    for prep, bsz, seq in trunks:
        ns = seq // ts
        assert n_steps % ns == 0 and bsz % (n_steps // ns) == 0
        nb = bsz // (n_steps // ns)
        group_sizes.append(nb)
        periods.append(ns)
        for d in range(2):
            u_, wq, at, kg, ds = prep[5 * d:5 * d + 5]
            args += [u_.reshape(bsz, seq, W_GROUP), wq.reshape(bsz, 2 * seq, W_GROUP),
                     at.reshape(bsz, seq, W_GROUP), kg.reshape(bsz, seq, W_GROUP),
                     ds.reshape(bsz, seq // CHUNK, 1, W_GROUP)]

            def tb(i, ns=ns, d=d):
                return (ns - 1 - i % ns) if d else (i % ns)

            def blk3(rows, nb=nb, ns=ns, tb=tb):
                return pl.BlockSpec((nb, rows, W_GROUP), lambda i: (i // ns, tb(i), 0))

            specs += [blk3(ts), blk3(2 * ts), blk3(ts), blk3(ts),
                      pl.BlockSpec((nb, nc, 1, W_GROUP), lambda i, ns=ns, tb=tb: (i // ns, tb(i), 0, 0))]
            out_specs.append(blk3(ts))
            out_shapes.append(jax.ShapeDtypeStruct((bsz, seq, W_GROUP), F32))
    outs = pl.pallas_call(
        functools.partial(_gdn_scan_kernel, tuple(group_sizes), tuple(periods)),
        grid=(n_steps,),
        in_specs=specs,
        out_specs=out_specs,
        out_shape=out_shapes,
        scratch_shapes=[pltpu.VMEM((2 * sum(group_sizes), W_GROUP, W_GROUP), F32)],
        compiler_params=_params(("arbitrary",)),
        name="gdn_scan",
    )(*args)
    return [(outs[2 * n].reshape(bsz * seq, W_GROUP), outs[2 * n + 1].reshape(bsz * seq, W_GROUP))
            for n, (_, bsz, seq) in enumerate(trunks)]


RET_CHUNK = 128
RET_BLOCK = 1024


def _ret_kernel(reverse, *refs):
    if reverse:
        (q_ref, k_ref, v_ref, cos_ref, sin_ref, lgq_ref, lgv_ref, o_ref, s_ref) = refs
    else:
        (q_ref, k_ref, v_ref, cos_ref, sin_ref, lgq_ref, lgv_ref, lgfc_ref, lgbc_ref, g_ref, ob_ref, rn_ref,
         o_ref, s_ref, dm_ref) = refs
    ts = q_ref.shape[0]
    cr = RET_CHUNK
    nc = ts // cr
    nk = N_HEADS * cr

    @pl.when(pl.program_id(1) == 0)
    def _():
        s_ref[...] = jnp.zeros_like(s_ref)
        if not reverse:
            ii = _iota((cr, nk), 0)
            jj = _iota((cr, nk), 1) % cr
            dist = (ii - jj).astype(F32)
            lower, upper = jj <= ii, jj >= ii
            dm_ref[...] = (jnp.where(lower, jnp.exp(jnp.where(lower, dist * lgfc_ref[...], 0.0)), 0.0)
                           + jnp.where(upper, jnp.exp(jnp.where(upper, -dist * lgbc_ref[...], 0.0)), 0.0))

    cos = cos_ref[...]
    sin = sin_ref[...]

    def rope(t):
        t1, t2 = t[:, :LANES], t[:, LANES:]
        return jnp.concatenate([t1 * cos - t2 * sin, t1 * sin + t2 * cos], axis=-1)

    q = rope(q_ref[...])
    k = rope(k_ref[...]) * (HEAD_DIM ** -0.5)
    v = v_ref[...]
    lgq = lgq_ref[...]
    lgv = lgv_ref[...]
    pos = _iota((cr, W_GROUP), 0).astype(F32)
    rows_qk = (_iota((W_GROUP, W_GROUP), 0) % LANES) // 32
    blk_state = rows_qk == _iota((W_GROUP, W_GROUP), 1) // HEAD_DIM
    if reverse:
        q_fac = jnp.exp((cr - pos) * lgq)
        k_fac = jnp.exp(pos * lgq)
    else:
        q_fac = jnp.exp((pos + 1.0) * lgq)
        k_fac = jnp.exp((cr - 1.0 - pos) * lgq)
    s_fac = jnp.exp(cr * lgv)
    sls = [slice(ci * cr, (ci + 1) * cr) for ci in range(nc)]
    kv = [jnp.where(blk_state, _dot_tn((k[sl] * k_fac).astype(BF16), v[sl].astype(BF16)), 0.0) for sl in sls]
    s = s_ref[...]
    cross = [None] * nc
    for ci in (range(nc - 1, -1, -1) if reverse else range(nc)):
        cross[ci] = _dot((q[sls[ci]] * q_fac).astype(BF16), s.astype(BF16))
        s = s * s_fac + kv[ci]
    s_ref[...] = s
    if reverse:
        o_ref[...] = jnp.concatenate(cross, axis=0)
        return
    stack_head = _iota((nk, W_GROUP), 0) // cr
    k_mask = stack_head == (_iota((nk, W_GROUP), 1) % LANES) // 32
    v_mask = stack_head == _iota((nk, W_GROUP), 1) // HEAD_DIM
    qk = [_dot_nt(q[sl].astype(BF16), _stack4(k[sl].astype(BF16), k_mask)) for sl in sls]
    dm = dm_ref[...]
    intra = [_dot((x * dm).astype(BF16), _stack4(v[sl].astype(BF16), v_mask)) for x, sl in zip(qk, sls)]
    o = jnp.concatenate([a + b for a, b in zip(intra, cross)], axis=0) + ob_ref[...]
    ones_h = _head_ones()
    oc = o - _dot_sel(o, ones_h, 3) * (1.0 / HEAD_DIM)
    var = _dot_sel(oc * oc, ones_h, 2) * (1.0 / HEAD_DIM)
    o_ref[...] = (oc * lax.rsqrt(var + EPS) * rn_ref[...] * _silu(g_ref[...])).astype(o_ref.dtype)


def _ret(u, bsz, seq, reverse, cos, sin, lgq, lgv, lgfc=None, lgbc=None, rn=None, ob=None):
    cr = RET_CHUNK
    ts = RET_BLOCK
    ns = seq // ts
    t = bsz * seq
    tbf = (lambda i: ns - 1 - i) if reverse else (lambda i: i)

    def col(cb):
        return pl.BlockSpec((ts, W_GROUP), lambda b, i: (b * ns + tbf(i), cb))

    tab = pl.BlockSpec((ts, LANES), lambda b, i: (tbf(i), 0))
    specs = [col(CB_QC), col(CB_KC), col(CB_VC), tab, tab, _const_spec((1, W_GROUP)), _const_spec((1, W_GROUP))]
    args = [u, u, u, cos, sin, lgq, lgv]
    scratch = [pltpu.VMEM((W_GROUP, W_GROUP), F32)]
    if reverse:
        out_dtype = F32
    else:
        specs += [_const_spec((1, N_HEADS * cr)), _const_spec((1, N_HEADS * cr)), col(CB_GC),
                  pl.BlockSpec((ts, W_GROUP), lambda b, i: (b * ns + i, 0)), _const_spec((1, W_GROUP))]
        args += [lgfc, lgbc, u, ob, rn]
        scratch.append(pltpu.VMEM((cr, N_HEADS * cr), F32))
        out_dtype = BF16
    return pl.pallas_call(
        functools.partial(_ret_kernel, reverse),
        grid=(bsz, ns),
        in_specs=specs,
        out_specs=pl.BlockSpec((ts, W_GROUP), lambda b, i: (b * ns + tbf(i), 0)),
        out_shape=jax.ShapeDtypeStruct((t, W_GROUP), out_dtype),
        scratch_shapes=scratch,
        compiler_params=_params(("parallel", "arbitrary")),
        name="ret_bwd" if reverse else "ret_fwd",
    )(*args)


NA_GROUP = 16
NA_WIN = NA_KH * GRID_W
NA_ROWS_PER_PASS = 4


def _na_kernel(rows, q_ref, kp_ref, kc_ref, kn_ref, vp_ref, vc_ref, vn_ref, tab_ref, o_ref, kw, vw):
    g = pl.program_id(1)
    blk = NA_GROUP * GRID_W
    kw[0:blk, :] = kp_ref[...]
    kw[blk:2 * blk, :] = kc_ref[...]
    kw[2 * blk:3 * blk, :] = kn_ref[...]
    vw[0:blk, :] = vp_ref[...]
    vw[blk:2 * blk, :] = vc_ref[...]
    vw[2 * blk:3 * blk, :] = vn_ref[...]
    row_head = _iota((N_HEADS * GRID_W, W_GROUP), 0) // GRID_W
    lane_head = _iota((N_HEADS * GRID_W, W_GROUP), 1) // HEAD_DIM
    blk = row_head == lane_head
    out_head = _iota((GRID_W, W_GROUP), 1) // HEAD_DIM
    for j0 in range(0, NA_GROUP, NA_ROWS_PER_PASS):
        js = list(range(j0, j0 + NA_ROWS_PER_PASS))
        vwins, sc = [], []
        for j in js:
            r = g * NA_GROUP + j
            r0 = jnp.clip(r - NA_KH // 2, 0, rows - NA_KH)
            off = pl.multiple_of((r0 - g * NA_GROUP + NA_GROUP) * GRID_W, GRID_W)
            qst = _stack4(q_ref[j * GRID_W:(j + 1) * GRID_W, :], blk)
            vwins.append(vw[pl.ds(off, NA_WIN), :])
            sc.append(_dot_nt(qst, kw[pl.ds(off, NA_WIN), :]) + tab_ref[r - r0])
        mx = [jnp.max(s, axis=-1, keepdims=True) for s in sc]
        pr = [jnp.exp(s - m) for s, m in zip(sc, mx)]
        den = [jnp.sum(p, axis=-1, keepdims=True) for p in pr]
        pv = [_dot(p.astype(BF16), vw_) / l for p, vw_, l in zip(pr, vwins, den)]
        for j, o in zip(js, pv):
            res = o[0:GRID_W]
            for h in range(1, N_HEADS):
                res = jnp.where(out_head == h, o[h * GRID_W:(h + 1) * GRID_W], res)
            o_ref[j * GRID_W:(j + 1) * GRID_W, :] = res.astype(o_ref.dtype)


def _na(qn, kn, vn, table, bsz, seq):
    rows = seq // GRID_W
    ng = rows // NA_GROUP
    blk = NA_GROUP * GRID_W
    t = bsz * seq
    cur = pl.BlockSpec((blk, W_GROUP), lambda b, g: (b * ng + g, 0))
    prev = pl.BlockSpec((blk, W_GROUP), lambda b, g: (b * ng + jnp.maximum(g - 1, 0), 0))
    nxt = pl.BlockSpec((blk, W_GROUP), lambda b, g: (b * ng + jnp.minimum(g + 1, ng - 1), 0))
    return pl.pallas_call(
        functools.partial(_na_kernel, rows),
        grid=(bsz, ng),
        in_specs=[cur, prev, cur, nxt, prev, cur, nxt, _const_spec((NA_KH, N_HEADS * GRID_W, NA_WIN))],
        out_specs=cur,
        out_shape=jax.ShapeDtypeStruct((t, W_GROUP), BF16),
        scratch_shapes=[pltpu.VMEM((3 * blk, W_GROUP), BF16), pltpu.VMEM((3 * blk, W_GROUP), BF16)],
        compiler_params=_params(("parallel", "parallel")),
        name="na",
    )(qn, kn, kn, kn, vn, vn, vn, table)


def _na_table(rpb):
    didx = np.arange(NA_KH)[:, None]
    kk = np.arange(NA_KH)[None, :]
    dr = (kk + (NA_KH - 1) - didx).reshape(-1)
    cc = np.arange(GRID_W)[:, None]
    kc = np.arange(GRID_W)[None, :]
    c0 = np.clip(cc - NA_KW // 2, 0, GRID_W - NA_KW)
    valid = (kc >= c0) & (kc < c0 + NA_KW)
    dc = np.clip(kc - cc + (NA_KW - 1), 0, 2 * NA_KW - 2).reshape(-1)
    row_sel = np.zeros((dr.size, 2 * NA_KH - 1), np.float32)
    row_sel[np.arange(dr.size), dr] = 1.0
    col_sel = np.zeros((2 * NA_KW - 1, dc.size), np.float32)
    col_sel[dc, np.arange(dc.size)] = 1.0
    bias = jnp.einsum('ar,hrs,sb->hab', row_sel, rpb, col_sel, precision=lax.Precision.HIGHEST)
    bias = bias.reshape(N_HEADS, NA_KH, NA_KH, GRID_W, GRID_W)
    tab = jnp.where(valid[None, None, None], bias, NEG_BIG)
    return jnp.transpose(tab, (1, 0, 3, 2, 4)).reshape(NA_KH, N_HEADS * GRID_W, NA_WIN)


FF_CHUNK = 256
POST_HALO = 16


def _mix_residual(x, a, bf, bb, z, gn, c, d, w_ref):
    ob = bf + bb
    ms = _dot_sel(ob * ob, _head_ones(), 2) * (1.0 / HEAD_DIM)
    out_b = (ob * lax.rsqrt(ms + EPS) * gn * _silu(z)).astype(BF16)
    acc = x
    for gi, m in enumerate((a, out_b, c, d)):
        acc = acc + _dot(m, w_ref[gi * W_GROUP:(gi + 1) * W_GROUP, :])
    return acc


def _post_kernel(*refs):
    streams = [refs[3 * n:3 * n + 3] for n in range(7)]
    (p_ref, gn_ref, wo_ref, n2_ref, wg_ref, wu_ref, cw_ref, cb_ref, wd_ref, n3_ref, pg_ref, pp_ref,
     o_ref, act_s) = refs[21:]
    tm = streams[0][0].shape[0]
    i = pl.program_id(1)
    first = i == 0
    last = i == pl.num_programs(1) - 1
    gn = gn_ref[...]
    mains = [s[0][...] for s in streams]
    halos = [jnp.concatenate([s[1][...], s[2][...]], axis=0) for s in streams]
    x = _mix_residual(mains[0], *mains[1:5], gn, *mains[5:], wo_ref)
    xh = _mix_residual(halos[0], *halos[1:5], gn, *halos[5:], wo_ref)
    n2 = n2_ref[...]
    h2f = _rms(x, n2)
    h2h = _rms(xh, n2)
    h_before = jnp.where(first, 0.0, h2h[POST_HALO - SUBLANES:POST_HALO])
    h_after = jnp.where(last, 0.0, h2h[POST_HALO:POST_HALO + SUBLANES])
    h2 = h2f.astype(BF16)
    h2e = jnp.concatenate([h_before, h2f, h_after], axis=0).astype(BF16)
    for f in range(D_FF // FF_CHUNK):
        fs = slice(f * FF_CHUNK, (f + 1) * FF_CHUNK)
        ge = _dot(h2e, wg_ref[:, fs])
        ne = tm + 2 * SUBLANES
        g_prev = pltpu.roll(ge, 1, 0)[SUBLANES:SUBLANES + tm]
        g_next = pltpu.roll(ge, ne - 1, 0)[SUBLANES:SUBLANES + tm]
        gate = (cw_ref[0:1, fs] * g_prev + cw_ref[1:2, fs] * ge[SUBLANES:SUBLANES + tm]
                + cw_ref[2:3, fs] * g_next + cb_ref[:, fs])
        up = _dot(h2, wu_ref[:, fs])
        act_s[:, fs] = (_gelu_tanh(gate) * up).astype(BF16)
    acc = x + _dot(act_s[...], wd_ref[...])
    h3 = _rms(acc, n3_ref[...]).astype(BF16)
    gate3 = _sigmoid(_dot(h3, pg_ref[...]))
    o_ref[...] = acc + gate3 * _dot(p_ref[...].astype(BF16), pp_ref[...])


def _post(x, u, oa, obf, obb, gn, oc, od, w_out, p, bsz, seq, n2, wg, wu, cw, cb, wd, n3, pg, pp):
    tm = 512
    ns = seq // tm
    t = bsz * seq
    specs = list(_row_specs(tm, ns, t, 0, False, width=D_MODEL, halo=POST_HALO))
    arrays = [x] * 3
    for arr, cb_ in ((oa, 0), (obf, 0), (obb, 0), (u, CB_ZB), (oc, 0), (od, 0)):
        specs += list(_row_specs(tm, ns, t, cb_, False, halo=POST_HALO))
        arrays += [arr] * 3
    specs += [pl.BlockSpec((tm, PLE_DIM), lambda b, i: (b * ns + i, 0)),
              _const_spec((1, W_GROUP)), _const_spec((D_MODEL, D_MODEL)),
              _const_spec((1, D_MODEL)), _const_spec((D_MODEL, D_FF)), _const_spec((D_MODEL, D_FF)),
              _const_spec((3, D_FF)), _const_spec((1, D_FF)), _const_spec((D_FF, D_MODEL)),
              _const_spec((1, D_MODEL)), _const_spec((D_MODEL, D_MODEL)), _const_spec((PLE_DIM, D_MODEL))]
    return pl.pallas_call(
        _post_kernel,
        grid=(bsz, ns),
        in_specs=specs,
        out_specs=pl.BlockSpec((tm, D_MODEL), lambda b, i: (b * ns + i, 0)),
        out_shape=jax.ShapeDtypeStruct((t, D_MODEL), F32),
        compiler_params=_params(("parallel", "parallel")),
        scratch_shapes=[pltpu.VMEM((tm, D_FF), BF16)],
        name="post",
    )(*arrays, p, gn, w_out, n2, wg, wu, cw, cb, wd, n3, pg, pp)


def _block_diag(w):
    h, n, _ = w.shape
    eye = jnp.eye(h, dtype=w.dtype)
    return (eye[:, None, :, None] * w[:, :, None, :]).reshape(h * n, h * n)


def _row(v):
    return v.reshape(1, -1).astype(F32)


def _lane_pad(v8, offset):
    return jnp.zeros((1, LANES), F32).at[0, offset:offset + 2 * N_HEADS].set(v8.reshape(-1).astype(F32))


def _expander(offset):
    e = np.zeros((LANES, W_GROUP), np.float32)
    for h in range(N_HEADS):
        e[offset + h, h * HEAD_DIM:(h + 1) * HEAD_DIM] = 1.0
    return jnp.asarray(e, BF16)


def _layer_front(x, lw, bsz, seq):
    w_in = _reorder_in_cols(lw['w_in'].astype(BF16))
    u, qn, kn, vn = _inproj(x, _row(lw['norm1']), w_in,
                            _row(jnp.tile(lw['na_qnorm'], N_HEADS)), _row(jnp.tile(lw['na_knorm'], N_HEADS)))

    cw_a, cb_a = lw['conv_a_w'].astype(F32), _row(lw['conv_a_b'])
    h_dir = None
    for d in (1, 0):
        h_dir = _rglru(u, bsz, seq, d == 1, cw_a, cb_a,
                       _block_diag(lw['lru_wr'][d]).astype(BF16), _row(lw['lru_br'][d]),
                       _block_diag(lw['lru_wi'][d]).astype(BF16), _row(lw['lru_bi'][d]),
                       _row(lw['lru_lambda'][d]), hb=h_dir)
    out_a = h_dir

    gconv = lw['gdn_conv'].astype(F32)
    cwq, cwk, cwv = gconv[:, :W_GROUP], gconv[:, W_GROUP:2 * W_GROUP], gconv[:, 2 * W_GROUP:]
    alog = _lane_pad(lw['gdn_a_log'], 2 * N_HEADS)
    dtb = _lane_pad(lw['gdn_dt_bias'], 2 * N_HEADS)
    eb = jnp.stack([_expander(d * N_HEADS) for d in range(2)])
    eg = jnp.stack([_expander(2 * N_HEADS + d * N_HEADS) for d in range(2)])
    prep = _gdn_prep(u, bsz, seq, cwq, cwk, cwv, alog, dtb, eb, eg)

    half = HEAD_DIM // 2
    inv_freq = ROPE_BASE ** (-jnp.arange(half, dtype=F32) / half)
    ang = jnp.arange(seq, dtype=F32)[:, None] * inv_freq[None, :]
    cos = jnp.tile(jnp.cos(ang), (1, N_HEADS))
    sin = jnp.tile(jnp.sin(ang), (1, N_HEADS))
    log_gamma = jnp.log1p(-jnp.exp2(-lw['ret_decay'].astype(F32)))
    rn = _row(jnp.tile(lw['ret_norm'], N_HEADS))
    o_dir = None
    for d in (1, 0):
        lg = log_gamma[d]
        lgq = _row(jnp.tile(jnp.repeat(lg, half), 2))
        lgv = _row(jnp.repeat(lg, HEAD_DIM))
        if d == 1:
            o_dir = _ret(u, bsz, seq, True, cos, sin, lgq, lgv)
        else:
            o_dir = _ret(u, bsz, seq, False, cos, sin, lgq, lgv,
                         lgfc=_row(jnp.repeat(log_gamma[0], RET_CHUNK)), lgbc=_row(jnp.repeat(log_gamma[1], RET_CHUNK)),
                         rn=rn, ob=o_dir)
    out_c = o_dir

    out_d = _na(qn, kn, vn, _na_table(lw['na_rpb'].astype(F32)), bsz, seq)
    return dict(u=u, out_a=out_a, out_c=out_c, out_d=out_d, prep=prep)


def _layer_back(x, p, front, o_b, lw, bsz, seq):
    return _post(x, front['u'], front['out_a'], o_b[0], o_b[1], _row(jnp.tile(lw['gdn_norm'], N_HEADS)),
                 front['out_c'], front['out_d'], lw['w_out'].astype(BF16), p, bsz, seq,
                 _row(lw['norm2']), lw['ffn_wg'].astype(BF16), lw['ffn_wu'].astype(BF16),
                 lw['ffn_conv_w'].astype(F32), _row(lw['ffn_conv_b']), lw['ffn_wd'].astype(BF16),
                 _row(lw['norm3']), lw['ple_gate'].astype(BF16), lw['ple_proj'].astype(BF16))


def _layer_multi(xs, ps, lw, shapes):
    fronts = [_layer_front(x, lw, bsz, seq) for x, (bsz, seq) in zip(xs, shapes)]
    scans = _gdn_scan([(f['prep'], bsz, seq) for f, (bsz, seq) in zip(fronts, shapes)])
    return [_layer_back(x, p, f, o_b, lw, bsz, seq)
            for x, p, f, o_b, (bsz, seq) in zip(xs, ps, fronts, scans, shapes)]


def _layer(x, p, lw, bsz, seq):
    return _layer_multi([x], [p], lw, [(bsz, seq)])[0]


_WEIGHT_NAMES = ('norm1', 'norm2', 'norm3', 'w_in', 'w_out', 'conv_a_w', 'conv_a_b', 'lru_wr', 'lru_br',
                 'lru_wi', 'lru_bi', 'lru_lambda', 'gdn_conv', 'gdn_a_log', 'gdn_dt_bias', 'gdn_norm',
                 'ret_decay', 'ret_norm', 'na_qnorm', 'na_knorm', 'na_rpb', 'ffn_wg', 'ffn_wu',
                 'ffn_conv_w', 'ffn_conv_b', 'ffn_wd', 'ple_proj', 'ple_gate')


def _trunks(xs, ps, weights):
    shapes = [x.shape[:2] for x in xs]
    flat = [x.reshape(bsz * seq, D_MODEL) for x, (bsz, seq) in zip(xs, shapes)]
    for i in range(DEPTH):
        lw = {n: w[i] for n, w in weights.items()}
        flat = _layer_multi(flat, [p[i].reshape(bsz * seq, PLE_DIM) for p, (bsz, seq) in zip(ps, shapes)], lw, shapes)
    return tuple(x.reshape(bsz, seq, D_MODEL) for x, (bsz, seq) in zip(flat, shapes))


def kernel(x_prompt, x_sample, p_prompt, p_sample, norm1, norm2, norm3, w_in, w_out, conv_a_w, conv_a_b, lru_wr, lru_br, lru_wi, lru_bi, lru_lambda, gdn_conv, gdn_a_log, gdn_dt_bias, gdn_norm, ret_decay, ret_norm, na_qnorm, na_knorm, na_rpb, ffn_wg, ffn_wu, ffn_conv_w, ffn_conv_b, ffn_wd, ple_proj, ple_gate):
    weights = dict(zip(_WEIGHT_NAMES, (norm1, norm2, norm3, w_in, w_out, conv_a_w, conv_a_b, lru_wr, lru_br,
                                       lru_wi, lru_bi, lru_lambda, gdn_conv, gdn_a_log, gdn_dt_bias, gdn_norm,
                                       ret_decay, ret_norm, na_qnorm, na_knorm, na_rpb, ffn_wg, ffn_wu,
                                       ffn_conv_w, ffn_conv_b, ffn_wd, ple_proj, ple_gate)))
    return _trunks((x_prompt, x_sample), (p_prompt, p_sample), weights)
```

```python
import functools

import jax
import jax.numpy as jnp
import numpy as np
from jax import lax
from jax.experimental import pallas as pl
from jax.experimental.pallas import tpu as pltpu

F32 = jnp.float32
BF16 = jnp.bfloat16

D_MODEL = 1024
DEPTH = 2
PLE_DIM = 256
GRID_W = 64
W_GROUP = 256
HEAD_DIM = 64
N_HEADS = 4
LRU_C = 8.0
CHUNK = 64
ROPE_BASE = 10000.0
NA_KH = 8
NA_KW = 16
D_FF = 2816
EPS = 1e-6
NEG_BIG = -1e30

SUBLANES = 8
LANES = 128
VMEM_LIMIT = 56 * 1024 * 1024

CB_XA, CB_YA, CB_QB, CB_KB, CB_VB, CB_ZB, CB_QC, CB_KC, CB_VC, CB_GC = range(10)
CB128_AB = 20
U_COLS = 10 * W_GROUP + LANES
IN_COLS_PAD = U_COLS + 3 * W_GROUP
IN_COLS = 13 * W_GROUP + 4 * N_HEADS


def _reorder_in_cols(w):
    rows = w.shape[0]
    half = HEAD_DIM // 2

    def halves_first(cols):
        return cols.reshape(rows, N_HEADS, 2, half).transpose(0, 2, 1, 3).reshape(rows, W_GROUP)

    ab0 = 6 * W_GROUP
    qc0 = ab0 + 4 * N_HEADS
    qd0 = qc0 + 4 * W_GROUP
    parts = [w[:, :ab0], halves_first(w[:, qc0:qc0 + W_GROUP]), halves_first(w[:, qc0 + W_GROUP:qc0 + 2 * W_GROUP]),
             w[:, qc0 + 2 * W_GROUP:qd0], w[:, ab0:qc0], jnp.zeros((rows, LANES - 4 * N_HEADS), w.dtype), w[:, qd0:]]
    out = jnp.concatenate(parts, axis=1)
    assert out.shape[1] == IN_COLS_PAD
    return out


def _sigmoid(x):
    return 1.0 / (1.0 + jnp.exp(-x))


def _silu(x):
    return x * _sigmoid(x)


def _softplus(x):
    return jnp.maximum(x, 0.0) + jnp.log1p(jnp.exp(-jnp.abs(x)))


def _gelu_tanh(x):
    return x * (0.5 * (1.0 + jnp.tanh(0.7978845608028654 * (x + 0.044715 * (x * x * x)))))


def _rms(x, gain):
    ms = jnp.mean(x * x, axis=-1, keepdims=True)
    return x * lax.rsqrt(ms + EPS) * gain


def _dot(a, b):
    return jnp.dot(a, b, preferred_element_type=F32)


def _split_bf16(x, terms):
    parts = []
    for _ in range(terms):
        p = x.astype(BF16)
        parts.append(p)
        x = x - p.astype(F32)
    return parts


def _dot_sel(x, sel, terms):
    return sum(_dot(p, sel) for p in _split_bf16(x, terms))


def _sel_dot(sel, x, terms):
    return sum(_dot(sel, p) for p in _split_bf16(x, terms))


def _dot_nt(a, b):
    return lax.dot_general(a, b, (((1,), (1,)), ((), ())), preferred_element_type=F32)


def _dot_tn(a, b):
    return lax.dot_general(a, b, (((0,), (0,)), ((), ())), preferred_element_type=F32)


def _iota(shape, dim):
    return lax.broadcasted_iota(jnp.int32, shape, dim)


def _head_ones():
    return (_iota((W_GROUP, W_GROUP), 0) // HEAD_DIM == _iota((W_GROUP, W_GROUP), 1) // HEAD_DIM).astype(BF16)


def _stack4(x, mask):
    return jnp.where(mask, jnp.concatenate([x] * N_HEADS, axis=0), jnp.zeros((), x.dtype))


def _conv4(xm, xp, xn, w_ref, first, last):
    ts = xm.shape[0]
    xp = jnp.where(first, 0.0, xp)
    xn = jnp.where(last, 0.0, xn)
    xe = jnp.concatenate([xp, xm, xn], axis=0)
    n = ts + 2 * SUBLANES

    def tap(offset):
        rolled = xe if offset == 0 else pltpu.roll(xe, (-offset) % n, 0)
        return rolled[SUBLANES:SUBLANES + ts]

    out = w_ref[0:1, :] * tap(-2)
    out = out + w_ref[1:2, :] * tap(-1)
    out = out + w_ref[2:3, :] * tap(0)
    out = out + w_ref[3:4, :] * tap(1)
    return out


def _time_block(reverse):
    i = pl.program_id(1)
    n = pl.num_programs(1)
    tb = (n - 1 - i) if reverse else i
    return tb, tb == 0, tb == n - 1


def _row_specs(ts, ns, total_rows, col_block, reverse, width=W_GROUP, halo=SUBLANES):
    per_block = ts // halo
    last = total_rows // halo - 1

    def tb(i):
        return (ns - 1 - i) if reverse else i

    main = pl.BlockSpec((ts, width), lambda b, i: (b * ns + tb(i), col_block))
    prev = pl.BlockSpec((halo, width), lambda b, i: (jnp.maximum((b * ns + tb(i)) * per_block - 1, 0), col_block))
    nxt = pl.BlockSpec((halo, width), lambda b, i: (jnp.minimum((b * ns + tb(i) + 1) * per_block, last), col_block))
    return main, prev, nxt


def _const_spec(shape):
    nd = len(shape)
    return pl.BlockSpec(shape, lambda *_: (0,) * nd, pipeline_mode=pl.Buffered(1))


def _params(sem):
    return pltpu.CompilerParams(dimension_semantics=sem, vmem_limit_bytes=VMEM_LIMIT)


def _inproj_kernel(x_ref, g_ref, w_ref, gq_ref, gk_ref, u_ref, qd_ref, kd_ref, vd_ref):
    h = _rms(x_ref[...], g_ref[...]).astype(BF16)
    u_ref[...] = _dot(h, w_ref[:, :U_COLS])
    qkv = _dot(h, w_ref[:, U_COLS:])
    q, k = qkv[:, :W_GROUP], qkv[:, W_GROUP:2 * W_GROUP]
    ones_h = _head_ones()
    inv = 1.0 / HEAD_DIM
    qd_ref[...] = (q * lax.rsqrt(_dot_sel(q * q, ones_h, 2) * inv + EPS) * gq_ref[...] * (HEAD_DIM ** -0.5)).astype(BF16)
    kd_ref[...] = (k * lax.rsqrt(_dot_sel(k * k, ones_h, 2) * inv + EPS) * gk_ref[...]).astype(BF16)
    vd_ref[...] = qkv[:, 2 * W_GROUP:].astype(BF16)


def _inproj(x, gain, w, gq, gk):
    t = x.shape[0]
    tm = 1024
    grp = pl.BlockSpec((tm, W_GROUP), lambda i: (i, 0))
    grp_shape = jax.ShapeDtypeStruct((t, W_GROUP), BF16)
    return pl.pallas_call(
        _inproj_kernel,
        grid=(t // tm,),
        in_specs=[pl.BlockSpec((tm, D_MODEL), lambda i: (i, 0)),
                  _const_spec((1, D_MODEL)),
                  _const_spec((D_MODEL, IN_COLS_PAD)),
                  _const_spec((1, W_GROUP)), _const_spec((1, W_GROUP))],
        out_specs=[pl.BlockSpec((tm, U_COLS), lambda i: (i, 0)), grp, grp, grp],
        out_shape=[jax.ShapeDtypeStruct((t, U_COLS), F32), grp_shape, grp_shape, grp_shape],
        compiler_params=_params(("parallel",)),
        name="inproj",
    )(x, gain, w, gq, gk)


def _rglru_kernel(reverse, *refs):
    if reverse:
        (xm_ref, xp_ref, xn_ref, cw_ref, cb_ref, wr_ref, br_ref, wi_ref, bi_ref, lam_ref,
         o_ref, a_s, b_s, p_s, ac_s, c_s, carry, h_s) = refs
    else:
        (xm_ref, xp_ref, xn_ref, cw_ref, cb_ref, wr_ref, br_ref, wi_ref, bi_ref, lam_ref, ya_ref, hb_ref,
         o_ref, a_s, b_s, p_s, ac_s, c_s, carry, h_s) = refs
    ts = xm_ref.shape[0]
    ng = ts // SUBLANES
    tb, first, last = _time_block(reverse)

    @pl.when(pl.program_id(1) == 0)
    def _():
        carry[...] = jnp.zeros_like(carry)

    x = _conv4(xm_ref[...], xp_ref[...], xn_ref[...], cw_ref, first, last) + cb_ref[...]
    xb = x.astype(BF16)
    r = _sigmoid(_dot(xb, wr_ref[...]) + br_ref[...])
    ig = _sigmoid(_dot(xb, wi_ref[...]) + bi_ref[...])
    log_a = (-LRU_C) * r * _softplus(-lam_ref[...])
    a = jnp.exp(log_a)
    th = jnp.tanh(-log_a)
    bx = jnp.sqrt(2.0 * th / (1.0 + th)) * (ig * x)
    nl = W_GROUP // LANES
    for hf in range(nl):
        a_s[hf] = a[:, hf * LANES:(hf + 1) * LANES]
        b_s[hf] = bx[:, hf * LANES:(hf + 1) * LANES]

    def slab(ref, k):
        return jnp.concatenate([ref[hf, pl.ds(k, ng, stride=SUBLANES), :] for hf in range(nl)], axis=-1)

    order = list(range(SUBLANES))[::-1] if reverse else list(range(SUBLANES))
    p = None
    ac = None
    for k in order:
        a_k = slab(a_s, k)
        b_k = slab(b_s, k)
        if p is None:
            p, ac = b_k, a_k
        else:
            p = a_k * p + b_k
            ac = a_k * ac
        p_s[k] = p
        ac_s[k] = ac

    k_last = order[-1]

    def body(g, c):
        gi = (ng - 1 - g) if reverse else g
        c_s[pl.ds(gi, 1), :] = c
        return ac_s[k_last, pl.ds(gi, 1), :] * c + p_s[k_last, pl.ds(gi, 1), :]

    carry[...] = lax.fori_loop(0, ng, body, carry[...], unroll=True)
    c_in = c_s[...]
    for k in order:
        h_k = p_s[k] + ac_s[k] * c_in
        for hf in range(nl):
            h_s[hf, pl.ds(k, ng, stride=SUBLANES), :] = h_k[:, hf * LANES:(hf + 1) * LANES]
    h = jnp.concatenate([h_s[hf] for hf in range(nl)], axis=-1)
    if reverse:
        o_ref[...] = h
    else:
        o_ref[...] = ((h + hb_ref[...]) * _gelu_tanh(ya_ref[...])).astype(o_ref.dtype)


def _rglru(u, bsz, seq, reverse, cw, cb, wr, br, wi, bi, lam, hb=None):
    ts = 1024
    ns = seq // ts
    t = bsz * seq
    ng = ts // SUBLANES
    xm, xp, xn = _row_specs(ts, ns, t, CB_XA, reverse)
    in_specs = [xm, xp, xn, _const_spec((4, W_GROUP)), _const_spec((1, W_GROUP)),
                _const_spec((W_GROUP, W_GROUP)), _const_spec((1, W_GROUP)),
                _const_spec((W_GROUP, W_GROUP)), _const_spec((1, W_GROUP)), _const_spec((1, W_GROUP))]
    args = [u, u, u, cw, cb, wr, br, wi, bi, lam]
    halves = (W_GROUP // LANES, ts, LANES)
    scratch = [pltpu.VMEM(halves, F32), pltpu.VMEM(halves, F32),
               pltpu.VMEM((SUBLANES, ng, W_GROUP), F32), pltpu.VMEM((SUBLANES, ng, W_GROUP), F32),
               pltpu.VMEM((ng, W_GROUP), F32), pltpu.VMEM((1, W_GROUP), F32), pltpu.VMEM(halves, F32)]
    if reverse:
        out_dtype = F32
        out_spec = pl.BlockSpec((ts, W_GROUP), lambda b, i: (b * ns + ns - 1 - i, 0))
    else:
        ya = _row_specs(ts, ns, t, CB_YA, False)[0]
        in_specs += [ya, pl.BlockSpec((ts, W_GROUP), lambda b, i: (b * ns + i, 0))]
        args += [u, hb]
        out_dtype = BF16
        out_spec = pl.BlockSpec((ts, W_GROUP), lambda b, i: (b * ns + i, 0))
    return pl.pallas_call(
        functools.partial(_rglru_kernel, reverse),
        grid=(bsz, ns),
        in_specs=in_specs,
        out_specs=out_spec,
        out_shape=jax.ShapeDtypeStruct((t, W_GROUP), out_dtype),
        scratch_shapes=scratch,
        compiler_params=_params(("parallel", "arbitrary")),
        name="rglru_bwd" if reverse else "rglru_fwd",
    )(*args)


GDN_PREP_BLOCK = 512
GDN_SCAN_BLOCK = 512


def _gdn_prep_kernel(qm, qp, qn, km, kp, kn, vm, vp, vn, ab_ref, cwq, cwk, cwv, alog_ref, dtb_ref, eb_ref, eg_ref,
                     *outs):
    ts = qm.shape[0]
    nc = ts // CHUNK
    c = CHUNK
    i = pl.program_id(1)
    first = i == 0
    last = i == pl.num_programs(1) - 1
    out_refs = (outs[0:5], outs[5:10])

    ones_h = _head_ones()
    q = _silu(_conv4(qm[...], qp[...], qn[...], cwq, first, last))
    k = _silu(_conv4(km[...], kp[...], kn[...], cwk, first, last))
    v = _silu(_conv4(vm[...], vp[...], vn[...], cwv, first, last))
    q = q * lax.rsqrt(_dot_sel(q * q, ones_h, 2) + EPS) * (HEAD_DIM ** -0.5)
    k = k * lax.rsqrt(_dot_sel(k * k, ones_h, 2) + EPS)
    ab = ab_ref[...]
    sig_parts = _split_bf16(_sigmoid(ab), 2)
    g_parts = _split_bf16(-jnp.exp(alog_ref[...]) * _softplus(ab + dtb_ref[...]), 3)
    beta = [sum(_dot(p, eb_ref[d]) for p in sig_parts) for d in range(2)]
    gate = [sum(_dot(p, eg_ref[d]) for p in g_parts) for d in range(2)]

    ii = _iota((c, W_GROUP), 0)
    jj = _iota((c, W_GROUP), 1) % c
    r2 = _iota((c, c), 0)
    c2 = _iota((c, c), 1)
    incl = (jj <= ii, jj >= ii)
    strict = (jj < ii, jj > ii)
    tri = ((c2 <= r2).astype(BF16), (c2 >= r2).astype(BF16))
    ucat = ((ii > jj).astype(F32), (ii < jj).astype(F32))
    eye_cat = (ii == jj).astype(F32)
    blk = _iota((W_GROUP, W_GROUP), 0) // HEAD_DIM == _iota((W_GROUP, W_GROUP), 1) // HEAD_DIM

    sls = [slice(ci * c, (ci + 1) * c) for ci in range(nc)]
    kq = [_dot_nt(jnp.concatenate([k[sl], q[sl]], axis=0).astype(BF16), _stack4(k[sl].astype(BF16), blk))
          for sl in sls]
    units = [(d, ci) for ci in range(nc) for d in range(2)]
    gd = [_sel_dot(tri[d], jnp.concatenate([gate[d][sls[ci]], gate[d][sls[ci]] * ucat[d]], axis=1), 3)
          for d, ci in units]
    gcum = [x[:, :W_GROUP] for x in gd]
    decay = [jnp.exp(x[:, W_GROUP:]) for x in gd]
    lmat = [jnp.where(strict[d], beta[d][sls[ci]] * kq[ci][:c] * dec, 0.0) for (d, ci), dec in zip(units, decay)]
    attn = [jnp.where(incl[d], kq[ci][c:] * dec, 0.0) for (d, ci), dec in zip(units, decay)]
    pinv = [eye_cat - l for l in lmat]
    lb = [l.astype(BF16) for l in lmat]
    lpow = [_dot(x, _stack4(x, blk)) for x in lb]
    for lvl in range(5):
        z = [_stack4(x.astype(BF16), blk) for x in lpow]
        if lvl < 4:
            r = [_dot(jnp.concatenate([p, lp], axis=0).astype(BF16), zz) for p, lp, zz in zip(pinv, lpow, z)]
            pinv = [p + x[:c] for p, x in zip(pinv, r)]
            lpow = [x[c:] for x in r]
        else:
            pinv = [p + _dot(p.astype(BF16), zz) for p, zz in zip(pinv, z)]
    egc = [jnp.exp(x) for x in gcum]
    pb = [p.astype(BF16) for p in pinv]
    for n, (d, ci) in enumerate(units):
        sl = sls[ci]
        u_ref, wq_ref, at_ref, kg_ref, ds_ref = out_refs[d]
        bc = beta[d][sl]
        u_ref[sl, :] = _dot(pb[n], _stack4((v[sl] * bc).astype(BF16), blk))
        w_key = _dot(pb[n], _stack4((k[sl] * bc * egc[n]).astype(BF16), blk))
        g_last = gcum[n][0:1, :] if d else gcum[n][c - 1:c, :]
        wq_ref[2 * ci * c:(2 * ci + 1) * c, :] = w_key.astype(BF16)
        wq_ref[(2 * ci + 1) * c:(2 * ci + 2) * c, :] = (q[sl] * egc[n]).astype(BF16)
        at_ref[sl, :] = attn[n].astype(BF16)
        kg_ref[sl, :] = (k[sl] * jnp.exp(g_last - gcum[n])).astype(BF16)
        ds_ref[ci] = jnp.exp(g_last)


def _gdn_prep(u, bsz, seq, cwq, cwk, cwv, alog, dtb, eb, eg):
    ts = GDN_PREP_BLOCK
    ns = seq // ts
    t = bsz * seq
    nc = ts // CHUNK
    specs = []
    for cb in (CB_QB, CB_KB, CB_VB):
        specs += list(_row_specs(ts, ns, t, cb, False))
    specs.append(pl.BlockSpec((ts, LANES), lambda b, i: (b * ns + i, CB128_AB)))
    specs += [_const_spec((4, W_GROUP))] * 3
    specs += [_const_spec((1, LANES))] * 2 + [_const_spec((2, LANES, W_GROUP))] * 2

    def row(b, i):
        return (b * ns + i, 0)

    one_dir_specs = [pl.BlockSpec((ts, W_GROUP), row), pl.BlockSpec((2 * ts, W_GROUP), row),
                     pl.BlockSpec((ts, W_GROUP), row), pl.BlockSpec((ts, W_GROUP), row),
                     pl.BlockSpec((nc, 1, W_GROUP), lambda b, i: (b * ns + i, 0, 0))]
    one_dir_shapes = [jax.ShapeDtypeStruct((t, W_GROUP), F32), jax.ShapeDtypeStruct((2 * t, W_GROUP), BF16),
                      jax.ShapeDtypeStruct((t, W_GROUP), BF16), jax.ShapeDtypeStruct((t, W_GROUP), BF16),
                      jax.ShapeDtypeStruct((t // CHUNK, 1, W_GROUP), F32)]
    return pl.pallas_call(
        _gdn_prep_kernel,
        grid=(bsz, ns),
        in_specs=specs,
        out_specs=one_dir_specs * 2,
        out_shape=one_dir_shapes * 2,
        compiler_params=_params(("parallel", "parallel")),
        name="gdn_prep",
    )(*([u] * 10 + [cwq, cwk, cwv, alog, dtb, eb, eg]))


def _gdn_scan_kernel(group_sizes, periods, *refs):
    n_groups = len(group_sizes)
    ins = refs[:10 * n_groups]
    outs = refs[10 * n_groups:12 * n_groups]
    s_ref = refs[12 * n_groups]
    ts = ins[0].shape[1]
    nc = ts // CHUNK
    c = CHUNK

    chains = []
    for gi, nb in enumerate(group_sizes):
        first_chain = len(chains)
        for b in range(nb):
            for d in range(2):
                chains.append(tuple(ins[10 * gi + 5 * d:10 * gi + 5 * d + 5]) + (outs[2 * gi + d], d, b))

        @pl.when(pl.program_id(0) % periods[gi] == 0)
        def _(lo=first_chain, hi=len(chains)):
            s_ref[lo:hi] = jnp.zeros((hi - lo, W_GROUP, W_GROUP), F32)

    blk = _iota((W_GROUP, W_GROUP), 0) // HEAD_DIM == _iota((W_GROUP, W_GROUP), 1) // HEAD_DIM
    state = [s_ref[n] for n in range(len(chains))]
    for step in range(nc):
        pos = (step, nc - 1 - step)
        ws = [_dot(wq[b, 2 * pos[d] * c:(2 * pos[d] + 2) * c, :], s.astype(BF16))
              for (u_, wq, at, kg, ds, o, d, b), s in zip(chains, state)]
        vnb = [(u_[b, pos[d] * c:(pos[d] + 1) * c, :] - x[:c]).astype(BF16)
               for (u_, wq, at, kg, ds, o, d, b), x in zip(chains, ws)]
        for (u_, wq, at, kg, ds, o, d, b), x, vn in zip(chains, ws, vnb):
            sl = slice(pos[d] * c, (pos[d] + 1) * c)
            o[b, sl, :] = x[c:] + _dot(at[b, sl, :], _stack4(vn, blk))
        state = [s * ds[b, pos[d]] + jnp.where(blk, _dot_tn(kg[b, pos[d] * c:(pos[d] + 1) * c, :], vn), 0.0)
                 for (u_, wq, at, kg, ds, o, d, b), s, vn in zip(chains, state, vnb)]
    for n, s in enumerate(state):
        s_ref[n] = s


def _gdn_scan(trunks):
    ts = GDN_SCAN_BLOCK
    nc = ts // CHUNK
    n_steps = max(seq // ts for _, _, seq in trunks)
    specs, args, out_specs, out_shapes, group_sizes, periods = [], [], [], [], [], []
    for prep, bsz, seq in trunks:
        ns = seq // ts
        assert n_steps % ns == 0 and bsz % (n_steps // ns) == 0
        nb = bsz // (n_steps // ns)
        group_sizes.append(nb)
        periods.append(ns)
        for d in range(2):
            u_, wq, at, kg, ds = prep[5 * d:5 * d + 5]
            args += [u_.reshape(bsz, seq, W_GROUP), wq.reshape(bsz, 2 * seq, W_GROUP),
                     at.reshape(bsz, seq, W_GROUP), kg.reshape(bsz, seq, W_GROUP),
                     ds.reshape(bsz, seq // CHUNK, 1, W_GROUP)]

            def tb(i, ns=ns, d=d):
                return (ns - 1 - i % ns) if d else (i % ns)

            def blk3(rows, nb=nb, ns=ns, tb=tb):
                return pl.BlockSpec((nb, rows, W_GROUP), lambda i: (i // ns, tb(i), 0))

            specs += [blk3(ts), blk3(2 * ts), blk3(ts), blk3(ts),
                      pl.BlockSpec((nb, nc, 1, W_GROUP), lambda i, ns=ns, tb=tb: (i // ns, tb(i), 0, 0))]
            out_specs.append(blk3(ts))
            out_shapes.append(jax.ShapeDtypeStruct((bsz, seq, W_GROUP), F32))
    outs = pl.pallas_call(
        functools.partial(_gdn_scan_kernel, tuple(group_sizes), tuple(periods)),
        grid=(n_steps,),
        in_specs=specs,
        out_specs=out_specs,
        out_shape=out_shapes,
        scratch_shapes=[pltpu.VMEM((2 * sum(group_sizes), W_GROUP, W_GROUP), F32)],
        compiler_params=_params(("arbitrary",)),
        name="gdn_scan",
    )(*args)
    return [(outs[2 * n].reshape(bsz * seq, W_GROUP), outs[2 * n + 1].reshape(bsz * seq, W_GROUP))
            for n, (_, bsz, seq) in enumerate(trunks)]


RET_CHUNK = 128
RET_BLOCK = 1024


def _ret_kernel(reverse, *refs):
    if reverse:
        (q_ref, k_ref, v_ref, cos_ref, sin_ref, lgq_ref, lgv_ref, o_ref, s_ref) = refs
    else:
        (q_ref, k_ref, v_ref, cos_ref, sin_ref, lgq_ref, lgv_ref, lgfc_ref, lgbc_ref, g_ref, ob_ref, rn_ref,
         o_ref, s_ref, dm_ref) = refs
    ts = q_ref.shape[0]
    cr = RET_CHUNK
    nc = ts // cr
    nk = N_HEADS * cr

    @pl.when(pl.program_id(1) == 0)
    def _():
        s_ref[...] = jnp.zeros_like(s_ref)
        if not reverse:
            ii = _iota((cr, nk), 0)
            jj = _iota((cr, nk), 1) % cr
            dist = (ii - jj).astype(F32)
            lower, upper = jj <= ii, jj >= ii
            dm_ref[...] = (jnp.where(lower, jnp.exp(jnp.where(lower, dist * lgfc_ref[...], 0.0)), 0.0)
                           + jnp.where(upper, jnp.exp(jnp.where(upper, -dist * lgbc_ref[...], 0.0)), 0.0))

    cos = cos_ref[...]
    sin = sin_ref[...]

    def rope(t):
        t1, t2 = t[:, :LANES], t[:, LANES:]
        return jnp.concatenate([t1 * cos - t2 * sin, t1 * sin + t2 * cos], axis=-1)

    q = rope(q_ref[...])
    k = rope(k_ref[...]) * (HEAD_DIM ** -0.5)
    v = v_ref[...]
    lgq = lgq_ref[...]
    lgv = lgv_ref[...]
    pos = _iota((cr, W_GROUP), 0).astype(F32)
    rows_qk = (_iota((W_GROUP, W_GROUP), 0) % LANES) // 32
    blk_state = rows_qk == _iota((W_GROUP, W_GROUP), 1) // HEAD_DIM
    if reverse:
        q_fac = jnp.exp((cr - pos) * lgq)
        k_fac = jnp.exp(pos * lgq)
    else:
        q_fac = jnp.exp((pos + 1.0) * lgq)
        k_fac = jnp.exp((cr - 1.0 - pos) * lgq)
    s_fac = jnp.exp(cr * lgv)
    sls = [slice(ci * cr, (ci + 1) * cr) for ci in range(nc)]
    kv = [jnp.where(blk_state, _dot_tn((k[sl] * k_fac).astype(BF16), v[sl].astype(BF16)), 0.0) for sl in sls]
    s = s_ref[...]
    cross = [None] * nc
    for ci in (range(nc - 1, -1, -1) if reverse else range(nc)):
        cross[ci] = _dot((q[sls[ci]] * q_fac).astype(BF16), s.astype(BF16))
        s = s * s_fac + kv[ci]
    s_ref[...] = s
    if reverse:
        o_ref[...] = jnp.concatenate(cross, axis=0)
        return
    stack_head = _iota((nk, W_GROUP), 0) // cr
    k_mask = stack_head == (_iota((nk, W_GROUP), 1) % LANES) // 32
    v_mask = stack_head == _iota((nk, W_GROUP), 1) // HEAD_DIM
    qk = [_dot_nt(q[sl].astype(BF16), _stack4(k[sl].astype(BF16), k_mask)) for sl in sls]
    dm = dm_ref[...]
    intra = [_dot((x * dm).astype(BF16), _stack4(v[sl].astype(BF16), v_mask)) for x, sl in zip(qk, sls)]
    o = jnp.concatenate([a + b for a, b in zip(intra, cross)], axis=0) + ob_ref[...]
    ones_h = _head_ones()
    oc = o - _dot_sel(o, ones_h, 3) * (1.0 / HEAD_DIM)
    var = _dot_sel(oc * oc, ones_h, 2) * (1.0 / HEAD_DIM)
    o_ref[...] = (oc * lax.rsqrt(var + EPS) * rn_ref[...] * _silu(g_ref[...])).astype(o_ref.dtype)


def _ret(u, bsz, seq, reverse, cos, sin, lgq, lgv, lgfc=None, lgbc=None, rn=None, ob=None):
    cr = RET_CHUNK
    ts = RET_BLOCK
    ns = seq // ts
    t = bsz * seq
    tbf = (lambda i: ns - 1 - i) if reverse else (lambda i: i)

    def col(cb):
        return pl.BlockSpec((ts, W_GROUP), lambda b, i: (b * ns + tbf(i), cb))

    tab = pl.BlockSpec((ts, LANES), lambda b, i: (tbf(i), 0))
    specs = [col(CB_QC), col(CB_KC), col(CB_VC), tab, tab, _const_spec((1, W_GROUP)), _const_spec((1, W_GROUP))]
    args = [u, u, u, cos, sin, lgq, lgv]
    scratch = [pltpu.VMEM((W_GROUP, W_GROUP), F32)]
    if reverse:
        out_dtype = F32
    else:
        specs += [_const_spec((1, N_HEADS * cr)), _const_spec((1, N_HEADS * cr)), col(CB_GC),
                  pl.BlockSpec((ts, W_GROUP), lambda b, i: (b * ns + i, 0)), _const_spec((1, W_GROUP))]
        args += [lgfc, lgbc, u, ob, rn]
        scratch.append(pltpu.VMEM((cr, N_HEADS * cr), F32))
        out_dtype = BF16
    return pl.pallas_call(
        functools.partial(_ret_kernel, reverse),
        grid=(bsz, ns),
        in_specs=specs,
        out_specs=pl.BlockSpec((ts, W_GROUP), lambda b, i: (b * ns + tbf(i), 0)),
        out_shape=jax.ShapeDtypeStruct((t, W_GROUP), out_dtype),
        scratch_shapes=scratch,
        compiler_params=_params(("parallel", "arbitrary")),
        name="ret_bwd" if reverse else "ret_fwd",
    )(*args)


NA_GROUP = 16
NA_WIN = NA_KH * GRID_W
NA_ROWS_PER_PASS = 4


def _na_kernel(rows, q_ref, kp_ref, kc_ref, kn_ref, vp_ref, vc_ref, vn_ref, tab_ref, o_ref, kw, vw):
    g = pl.program_id(1)
    blk = NA_GROUP * GRID_W
    kw[0:blk, :] = kp_ref[...]
    kw[blk:2 * blk, :] = kc_ref[...]
    kw[2 * blk:3 * blk, :] = kn_ref[...]
    vw[0:blk, :] = vp_ref[...]
    vw[blk:2 * blk, :] = vc_ref[...]
    vw[2 * blk:3 * blk, :] = vn_ref[...]
    row_head = _iota((N_HEADS * GRID_W, W_GROUP), 0) // GRID_W
    lane_head = _iota((N_HEADS * GRID_W, W_GROUP), 1) // HEAD_DIM
    blk = row_head == lane_head
    out_head = _iota((GRID_W, W_GROUP), 1) // HEAD_DIM
    for j0 in range(0, NA_GROUP, NA_ROWS_PER_PASS):
        js = list(range(j0, j0 + NA_ROWS_PER_PASS))
        vwins, sc = [], []
        for j in js:
            r = g * NA_GROUP + j
            r0 = jnp.clip(r - NA_KH // 2, 0, rows - NA_KH)
            off = pl.multiple_of((r0 - g * NA_GROUP + NA_GROUP) * GRID_W, GRID_W)
            qst = _stack4(q_ref[j * GRID_W:(j + 1) * GRID_W, :], blk)
            vwins.append(vw[pl.ds(off, NA_WIN), :])
            sc.append(_dot_nt(qst, kw[pl.ds(off, NA_WIN), :]) + tab_ref[r - r0])
        mx = [jnp.max(s, axis=-1, keepdims=True) for s in sc]
        pr = [jnp.exp(s - m) for s, m in zip(sc, mx)]
        den = [jnp.sum(p, axis=-1, keepdims=True) for p in pr]
        pv = [_dot(p.astype(BF16), vw_) / l for p, vw_, l in zip(pr, vwins, den)]
        for j, o in zip(js, pv):
            res = o[0:GRID_W]
            for h in range(1, N_HEADS):
                res = jnp.where(out_head == h, o[h * GRID_W:(h + 1) * GRID_W], res)
            o_ref[j * GRID_W:(j + 1) * GRID_W, :] = res.astype(o_ref.dtype)


def _na(qn, kn, vn, table, bsz, seq):
    rows = seq // GRID_W
    ng = rows // NA_GROUP
    blk = NA_GROUP * GRID_W
    t = bsz * seq
    cur = pl.BlockSpec((blk, W_GROUP), lambda b, g: (b * ng + g, 0))
    prev = pl.BlockSpec((blk, W_GROUP), lambda b, g: (b * ng + jnp.maximum(g - 1, 0), 0))
    nxt = pl.BlockSpec((blk, W_GROUP), lambda b, g: (b * ng + jnp.minimum(g + 1, ng - 1), 0))
    return pl.pallas_call(
        functools.partial(_na_kernel, rows),
        grid=(bsz, ng),
        in_specs=[cur, prev, cur, nxt, prev, cur, nxt, _const_spec((NA_KH, N_HEADS * GRID_W, NA_WIN))],
        out_specs=cur,
        out_shape=jax.ShapeDtypeStruct((t, W_GROUP), BF16),
        scratch_shapes=[pltpu.VMEM((3 * blk, W_GROUP), BF16), pltpu.VMEM((3 * blk, W_GROUP), BF16)],
        compiler_params=_params(("parallel", "parallel")),
        name="na",
    )(qn, kn, kn, kn, vn, vn, vn, table)


def _na_table(rpb):
    didx = np.arange(NA_KH)[:, None]
    kk = np.arange(NA_KH)[None, :]
    dr = (kk + (NA_KH - 1) - didx).reshape(-1)
    cc = np.arange(GRID_W)[:, None]
    kc = np.arange(GRID_W)[None, :]
    c0 = np.clip(cc - NA_KW // 2, 0, GRID_W - NA_KW)
    valid = (kc >= c0) & (kc < c0 + NA_KW)
    dc = np.clip(kc - cc + (NA_KW - 1), 0, 2 * NA_KW - 2).reshape(-1)
    row_sel = np.zeros((dr.size, 2 * NA_KH - 1), np.float32)
    row_sel[np.arange(dr.size), dr] = 1.0
    col_sel = np.zeros((2 * NA_KW - 1, dc.size), np.float32)
    col_sel[dc, np.arange(dc.size)] = 1.0
    bias = jnp.einsum('ar,hrs,sb->hab', row_sel, rpb, col_sel, precision=lax.Precision.HIGHEST)
    bias = bias.reshape(N_HEADS, NA_KH, NA_KH, GRID_W, GRID_W)
    tab = jnp.where(valid[None, None, None], bias, NEG_BIG)
    return jnp.transpose(tab, (1, 0, 3, 2, 4)).reshape(NA_KH, N_HEADS * GRID_W, NA_WIN)


FF_CHUNK = 256
POST_HALO = 16


def _mix_residual(x, a, bf, bb, z, gn, c, d, w_ref):
    ob = bf + bb
    ms = _dot_sel(ob * ob, _head_ones(), 2) * (1.0 / HEAD_DIM)
    out_b = (ob * lax.rsqrt(ms + EPS) * gn * _silu(z)).astype(BF16)
    acc = x
    for gi, m in enumerate((a, out_b, c, d)):
        acc = acc + _dot(m, w_ref[gi * W_GROUP:(gi + 1) * W_GROUP, :])
    return acc


def _post_kernel(*refs):
    streams = [refs[3 * n:3 * n + 3] for n in range(7)]
    (p_ref, gn_ref, wo_ref, n2_ref, wg_ref, wu_ref, cw_ref, cb_ref, wd_ref, n3_ref, pg_ref, pp_ref,
     o_ref, act_s) = refs[21:]
    tm = streams[0][0].shape[0]
    i = pl.program_id(1)
    first = i == 0
    last = i == pl.num_programs(1) - 1
    gn = gn_ref[...]
    mains = [s[0][...] for s in streams]
    halos = [jnp.concatenate([s[1][...], s[2][...]], axis=0) for s in streams]
    x = _mix_residual(mains[0], *mains[1:5], gn, *mains[5:], wo_ref)
    xh = _mix_residual(halos[0], *halos[1:5], gn, *halos[5:], wo_ref)
    n2 = n2_ref[...]
    h2f = _rms(x, n2)
    h2h = _rms(xh, n2)
    h_before = jnp.where(first, 0.0, h2h[POST_HALO - SUBLANES:POST_HALO])
    h_after = jnp.where(last, 0.0, h2h[POST_HALO:POST_HALO + SUBLANES])
    h2 = h2f.astype(BF16)
    h2e = jnp.concatenate([h_before, h2f, h_after], axis=0).astype(BF16)
    for f in range(D_FF // FF_CHUNK):
        fs = slice(f * FF_CHUNK, (f + 1) * FF_CHUNK)
        ge = _dot(h2e, wg_ref[:, fs])
        ne = tm + 2 * SUBLANES
        g_prev = pltpu.roll(ge, 1, 0)[SUBLANES:SUBLANES + tm]
        g_next = pltpu.roll(ge, ne - 1, 0)[SUBLANES:SUBLANES + tm]
        gate = (cw_ref[0:1, fs] * g_prev + cw_ref[1:2, fs] * ge[SUBLANES:SUBLANES + tm]
                + cw_ref[2:3, fs] * g_next + cb_ref[:, fs])
        up = _dot(h2, wu_ref[:, fs])
        act_s[:, fs] = (_gelu_tanh(gate) * up).astype(BF16)
    acc = x + _dot(act_s[...], wd_ref[...])
    h3 = _rms(acc, n3_ref[...]).astype(BF16)
    gate3 = _sigmoid(_dot(h3, pg_ref[...]))
    o_ref[...] = acc + gate3 * _dot(p_ref[...].astype(BF16), pp_ref[...])


def _post(x, u, oa, obf, obb, gn, oc, od, w_out, p, bsz, seq, n2, wg, wu, cw, cb, wd, n3, pg, pp):
    tm = 512
    ns = seq // tm
    t = bsz * seq
    specs = list(_row_specs(tm, ns, t, 0, False, width=D_MODEL, halo=POST_HALO))
    arrays = [x] * 3
    for arr, cb_ in ((oa, 0), (obf, 0), (obb, 0), (u, CB_ZB), (oc, 0), (od, 0)):
        specs += list(_row_specs(tm, ns, t, cb_, False, halo=POST_HALO))
        arrays += [arr] * 3
    specs += [pl.BlockSpec((tm, PLE_DIM), lambda b, i: (b * ns + i, 0)),
              _const_spec((1, W_GROUP)), _const_spec((D_MODEL, D_MODEL)),
              _const_spec((1, D_MODEL)), _const_spec((D_MODEL, D_FF)), _const_spec((D_MODEL, D_FF)),
              _const_spec((3, D_FF)), _const_spec((1, D_FF)), _const_spec((D_FF, D_MODEL)),
              _const_spec((1, D_MODEL)), _const_spec((D_MODEL, D_MODEL)), _const_spec((PLE_DIM, D_MODEL))]
    return pl.pallas_call(
        _post_kernel,
        grid=(bsz, ns),
        in_specs=specs,
        out_specs=pl.BlockSpec((tm, D_MODEL), lambda b, i: (b * ns + i, 0)),
        out_shape=jax.ShapeDtypeStruct((t, D_MODEL), F32),
        compiler_params=_params(("parallel", "parallel")),
        scratch_shapes=[pltpu.VMEM((tm, D_FF), BF16)],
        name="post",
    )(*arrays, p, gn, w_out, n2, wg, wu, cw, cb, wd, n3, pg, pp)


def _block_diag(w):
    h, n, _ = w.shape
    eye = jnp.eye(h, dtype=w.dtype)
    return (eye[:, None, :, None] * w[:, :, None, :]).reshape(h * n, h * n)


def _row(v):
    return v.reshape(1, -1).astype(F32)


def _lane_pad(v8, offset):
    return jnp.zeros((1, LANES), F32).at[0, offset:offset + 2 * N_HEADS].set(v8.reshape(-1).astype(F32))


def _expander(offset):
    e = np.zeros((LANES, W_GROUP), np.float32)
    for h in range(N_HEADS):
        e[offset + h, h * HEAD_DIM:(h + 1) * HEAD_DIM] = 1.0
    return jnp.asarray(e, BF16)


def _layer_front(x, lw, bsz, seq):
    w_in = _reorder_in_cols(lw['w_in'].astype(BF16))
    u, qn, kn, vn = _inproj(x, _row(lw['norm1']), w_in,
                            _row(jnp.tile(lw['na_qnorm'], N_HEADS)), _row(jnp.tile(lw['na_knorm'], N_HEADS)))

    cw_a, cb_a = lw['conv_a_w'].astype(F32), _row(lw['conv_a_b'])
    h_dir = None
    for d in (1, 0):
        h_dir = _rglru(u, bsz, seq, d == 1, cw_a, cb_a,
                       _block_diag(lw['lru_wr'][d]).astype(BF16), _row(lw['lru_br'][d]),
                       _block_diag(lw['lru_wi'][d]).astype(BF16), _row(lw['lru_bi'][d]),
                       _row(lw['lru_lambda'][d]), hb=h_dir)
    out_a = h_dir

    gconv = lw['gdn_conv'].astype(F32)
    cwq, cwk, cwv = gconv[:, :W_GROUP], gconv[:, W_GROUP:2 * W_GROUP], gconv[:, 2 * W_GROUP:]
    alog = _lane_pad(lw['gdn_a_log'], 2 * N_HEADS)
    dtb = _lane_pad(lw['gdn_dt_bias'], 2 * N_HEADS)
    eb = jnp.stack([_expander(d * N_HEADS) for d in range(2)])
    eg = jnp.stack([_expander(2 * N_HEADS + d * N_HEADS) for d in range(2)])
    prep = _gdn_prep(u, bsz, seq, cwq, cwk, cwv, alog, dtb, eb, eg)

    half = HEAD_DIM // 2
    inv_freq = ROPE_BASE ** (-jnp.arange(half, dtype=F32) / half)
    ang = jnp.arange(seq, dtype=F32)[:, None] * inv_freq[None, :]
    cos = jnp.tile(jnp.cos(ang), (1, N_HEADS))
    sin = jnp.tile(jnp.sin(ang), (1, N_HEADS))
    log_gamma = jnp.log1p(-jnp.exp2(-lw['ret_decay'].astype(F32)))
    rn = _row(jnp.tile(lw['ret_norm'], N_HEADS))
    o_dir = None
    for d in (1, 0):
        lg = log_gamma[d]
        lgq = _row(jnp.tile(jnp.repeat(lg, half), 2))
        lgv = _row(jnp.repeat(lg, HEAD_DIM))
        if d == 1:
            o_dir = _ret(u, bsz, seq, True, cos, sin, lgq, lgv)
        else:
            o_dir = _ret(u, bsz, seq, False, cos, sin, lgq, lgv,
                         lgfc=_row(jnp.repeat(log_gamma[0], RET_CHUNK)), lgbc=_row(jnp.repeat(log_gamma[1], RET_CHUNK)),
                         rn=rn, ob=o_dir)
    out_c = o_dir

    out_d = _na(qn, kn, vn, _na_table(lw['na_rpb'].astype(F32)), bsz, seq)
    return dict(u=u, out_a=out_a, out_c=out_c, out_d=out_d, prep=prep)


def _layer_back(x, p, front, o_b, lw, bsz, seq):
    return _post(x, front['u'], front['out_a'], o_b[0], o_b[1], _row(jnp.tile(lw['gdn_norm'], N_HEADS)),
                 front['out_c'], front['out_d'], lw['w_out'].astype(BF16), p, bsz, seq,
                 _row(lw['norm2']), lw['ffn_wg'].astype(BF16), lw['ffn_wu'].astype(BF16),
                 lw['ffn_conv_w'].astype(F32), _row(lw['ffn_conv_b']), lw['ffn_wd'].astype(BF16),
                 _row(lw['norm3']), lw['ple_gate'].astype(BF16), lw['ple_proj'].astype(BF16))


def _layer_multi(xs, ps, lw, shapes):
    fronts = [_layer_front(x, lw, bsz, seq) for x, (bsz, seq) in zip(xs, shapes)]
    scans = _gdn_scan([(f['prep'], bsz, seq) for f, (bsz, seq) in zip(fronts, shapes)])
    return [_layer_back(x, p, f, o_b, lw, bsz, seq)
            for x, p, f, o_b, (bsz, seq) in zip(xs, ps, fronts, scans, shapes)]


def _layer(x, p, lw, bsz, seq):
    return _layer_multi([x], [p], lw, [(bsz, seq)])[0]


_WEIGHT_NAMES = ('norm1', 'norm2', 'norm3', 'w_in', 'w_out', 'conv_a_w', 'conv_a_b', 'lru_wr', 'lru_br',
                 'lru_wi', 'lru_bi', 'lru_lambda', 'gdn_conv', 'gdn_a_log', 'gdn_dt_bias', 'gdn_norm',
                 'ret_decay', 'ret_norm', 'na_qnorm', 'na_knorm', 'na_rpb', 'ffn_wg', 'ffn_wu',
                 'ffn_conv_w', 'ffn_conv_b', 'ffn_wd', 'ple_proj', 'ple_gate')


def _trunks(xs, ps, weights):
    shapes = [x.shape[:2] for x in xs]
    flat = [x.reshape(bsz * seq, D_MODEL) for x, (bsz, seq) in zip(xs, shapes)]
    for i in range(DEPTH):
        lw = {n: w[i] for n, w in weights.items()}
        flat = _layer_multi(flat, [p[i].reshape(bsz * seq, PLE_DIM) for p, (bsz, seq) in zip(ps, shapes)], lw, shapes)
    return tuple(x.reshape(bsz, seq, D_MODEL) for x, (bsz, seq) in zip(flat, shapes))


def kernel(x_prompt, x_sample, p_prompt, p_sample, norm1, norm2, norm3, w_in, w_out, conv_a_w, conv_a_b, lru_wr, lru_br, lru_wi, lru_bi, lru_lambda, gdn_conv, gdn_a_log, gdn_dt_bias, gdn_norm, ret_decay, ret_norm, na_qnorm, na_knorm, na_rpb, ffn_wg, ffn_wu, ffn_conv_w, ffn_conv_b, ffn_wd, ple_proj, ple_gate):
    weights = dict(zip(_WEIGHT_NAMES, (norm1, norm2, norm3, w_in, w_out, conv_a_w, conv_a_b, lru_wr, lru_br,
                                       lru_wi, lru_bi, lru_lambda, gdn_conv, gdn_a_log, gdn_dt_bias, gdn_norm,
                                       ret_decay, ret_norm, na_qnorm, na_knorm, na_rpb, ffn_wg, ffn_wu,
                                       ffn_conv_w, ffn_conv_b, ffn_wd, ple_proj, ple_gate)))
    return _trunks((x_prompt, x_sample), (p_prompt, p_sample), weights)
```

```python
import functools

import jax
import jax.numpy as jnp
import numpy as np
from jax import lax
from jax.experimental import pallas as pl
from jax.experimental.pallas import tpu as pltpu

F32 = jnp.float32
BF16 = jnp.bfloat16

D_MODEL = 1024
DEPTH = 2
PLE_DIM = 256
GRID_W = 64
W_GROUP = 256
HEAD_DIM = 64
N_HEADS = 4
LRU_C = 8.0
CHUNK = 64
ROPE_BASE = 10000.0
NA_KH = 8
NA_KW = 16
D_FF = 2816
EPS = 1e-6
NEG_BIG = -1e30

SUBLANES = 8
LANES = 128
VMEM_LIMIT = 56 * 1024 * 1024

CB_XA, CB_YA, CB_QB, CB_KB, CB_VB, CB_ZB, CB_QC, CB_KC, CB_VC, CB_GC = range(10)
CB128_AB = 20
U_COLS = 10 * W_GROUP + LANES
IN_COLS_PAD = U_COLS + 3 * W_GROUP
IN_COLS = 13 * W_GROUP + 4 * N_HEADS


def _reorder_in_cols(w):
    rows = w.shape[0]
    half = HEAD_DIM // 2

    def halves_first(cols):
        return cols.reshape(rows, N_HEADS, 2, half).transpose(0, 2, 1, 3).reshape(rows, W_GROUP)

    ab0 = 6 * W_GROUP
    qc0 = ab0 + 4 * N_HEADS
    qd0 = qc0 + 4 * W_GROUP
    parts = [w[:, :ab0], halves_first(w[:, qc0:qc0 + W_GROUP]), halves_first(w[:, qc0 + W_GROUP:qc0 + 2 * W_GROUP]),
             w[:, qc0 + 2 * W_GROUP:qd0], w[:, ab0:qc0], jnp.zeros((rows, LANES - 4 * N_HEADS), w.dtype), w[:, qd0:]]
    out = jnp.concatenate(parts, axis=1)
    assert out.shape[1] == IN_COLS_PAD
    return out


def _sigmoid(x):
    return 1.0 / (1.0 + jnp.exp(-x))


def _silu(x):
    return x * _sigmoid(x)


def _softplus(x):
    return jnp.maximum(x, 0.0) + jnp.log1p(jnp.exp(-jnp.abs(x)))


def _gelu_tanh(x):
    return x * (0.5 * (1.0 + jnp.tanh(0.7978845608028654 * (x + 0.044715 * (x * x * x)))))


def _rms(x, gain):
    ms = jnp.mean(x * x, axis=-1, keepdims=True)
    return x * lax.rsqrt(ms + EPS) * gain


def _dot(a, b):
    return jnp.dot(a, b, preferred_element_type=F32)


def _split_bf16(x, terms):
    parts = []
    for _ in range(terms):
        p = x.astype(BF16)
        parts.append(p)
        x = x - p.astype(F32)
    return parts


def _dot_sel(x, sel, terms):
    return sum(_dot(p, sel) for p in _split_bf16(x, terms))


def _sel_dot(sel, x, terms):
    return sum(_dot(sel, p) for p in _split_bf16(x, terms))


def _dot_nt(a, b):
    return lax.dot_general(a, b, (((1,), (1,)), ((), ())), preferred_element_type=F32)


def _dot_tn(a, b):
    return lax.dot_general(a, b, (((0,), (0,)), ((), ())), preferred_element_type=F32)


def _iota(shape, dim):
    return lax.broadcasted_iota(jnp.int32, shape, dim)


def _head_ones():
    return (_iota((W_GROUP, W_GROUP), 0) // HEAD_DIM == _iota((W_GROUP, W_GROUP), 1) // HEAD_DIM).astype(BF16)


def _stack4(x, mask):
    return jnp.where(mask, jnp.concatenate([x] * N_HEADS, axis=0), jnp.zeros((), x.dtype))


def _conv4(xm, xp, xn, w_ref, first, last):
    ts = xm.shape[0]
    xp = jnp.where(first, 0.0, xp)
    xn = jnp.where(last, 0.0, xn)
    xe = jnp.concatenate([xp, xm, xn], axis=0)
    n = ts + 2 * SUBLANES

    def tap(offset):
        rolled = xe if offset == 0 else pltpu.roll(xe, (-offset) % n, 0)
        return rolled[SUBLANES:SUBLANES + ts]

    out = w_ref[0:1, :] * tap(-2)
    out = out + w_ref[1:2, :] * tap(-1)
    out = out + w_ref[2:3, :] * tap(0)
    out = out + w_ref[3:4, :] * tap(1)
    return out


def _time_block(reverse):
    i = pl.program_id(1)
    n = pl.num_programs(1)
    tb = (n - 1 - i) if reverse else i
    return tb, tb == 0, tb == n - 1


def _row_specs(ts, ns, total_rows, col_block, reverse, width=W_GROUP, halo=SUBLANES):
    per_block = ts // halo
    last = total_rows // halo - 1

    def tb(i):
        return (ns - 1 - i) if reverse else i

    main = pl.BlockSpec((ts, width), lambda b, i: (b * ns + tb(i), col_block))
    prev = pl.BlockSpec((halo, width), lambda b, i: (jnp.maximum((b * ns + tb(i)) * per_block - 1, 0), col_block))
    nxt = pl.BlockSpec((halo, width), lambda b, i: (jnp.minimum((b * ns + tb(i) + 1) * per_block, last), col_block))
    return main, prev, nxt


def _const_spec(shape):
    nd = len(shape)
    return pl.BlockSpec(shape, lambda *_: (0,) * nd, pipeline_mode=pl.Buffered(1))


def _params(sem):
    return pltpu.CompilerParams(dimension_semantics=sem, vmem_limit_bytes=VMEM_LIMIT)


def _inproj_kernel(x_ref, g_ref, w_ref, gq_ref, gk_ref, u_ref, qd_ref, kd_ref, vd_ref):
    h = _rms(x_ref[...], g_ref[...]).astype(BF16)
    u_ref[...] = _dot(h, w_ref[:, :U_COLS])
    qkv = _dot(h, w_ref[:, U_COLS:])
    q, k = qkv[:, :W_GROUP], qkv[:, W_GROUP:2 * W_GROUP]
    ones_h = _head_ones()
    inv = 1.0 / HEAD_DIM
    qd_ref[...] = (q * lax.rsqrt(_dot_sel(q * q, ones_h, 2) * inv + EPS) * gq_ref[...] * (HEAD_DIM ** -0.5)).astype(BF16)
    kd_ref[...] = (k * lax.rsqrt(_dot_sel(k * k, ones_h, 2) * inv + EPS) * gk_ref[...]).astype(BF16)
    vd_ref[...] = qkv[:, 2 * W_GROUP:].astype(BF16)


def _inproj(x, gain, w, gq, gk):
    t = x.shape[0]
    tm = 1024
    grp = pl.BlockSpec((tm, W_GROUP), lambda i: (i, 0))
    grp_shape = jax.ShapeDtypeStruct((t, W_GROUP), BF16)
    return pl.pallas_call(
        _inproj_kernel,
        grid=(t // tm,),
        in_specs=[pl.BlockSpec((tm, D_MODEL), lambda i: (i, 0)),
                  _const_spec((1, D_MODEL)),
                  _const_spec((D_MODEL, IN_COLS_PAD)),
                  _const_spec((1, W_GROUP)), _const_spec((1, W_GROUP))],
        out_specs=[pl.BlockSpec((tm, U_COLS), lambda i: (i, 0)), grp, grp, grp],
        out_shape=[jax.ShapeDtypeStruct((t, U_COLS), F32), grp_shape, grp_shape, grp_shape],
        compiler_params=_params(("parallel",)),
        name="inproj",
    )(x, gain, w, gq, gk)


def _rglru_kernel(reverse, *refs):
    if reverse:
        (xm_ref, xp_ref, xn_ref, cw_ref, cb_ref, wr_ref, br_ref, wi_ref, bi_ref, lam_ref,
         o_ref, a_s, b_s, p_s, ac_s, c_s, carry, h_s) = refs
    else:
        (xm_ref, xp_ref, xn_ref, cw_ref, cb_ref, wr_ref, br_ref, wi_ref, bi_ref, lam_ref, ya_ref, hb_ref,
         o_ref, a_s, b_s, p_s, ac_s, c_s, carry, h_s) = refs
    ts = xm_ref.shape[0]
    ng = ts // SUBLANES
    tb, first, last = _time_block(reverse)

    @pl.when(pl.program_id(1) == 0)
    def _():
        carry[...] = jnp.zeros_like(carry)

    x = _conv4(xm_ref[...], xp_ref[...], xn_ref[...], cw_ref, first, last) + cb_ref[...]
    xb = x.astype(BF16)
    r = _sigmoid(_dot(xb, wr_ref[...]) + br_ref[...])
    ig = _sigmoid(_dot(xb, wi_ref[...]) + bi_ref[...])
    log_a = (-LRU_C) * r * _softplus(-lam_ref[...])
    a = jnp.exp(log_a)
    th = jnp.tanh(-log_a)
    bx = jnp.sqrt(2.0 * th / (1.0 + th)) * (ig * x)
    nl = W_GROUP // LANES
    for hf in range(nl):
        a_s[hf] = a[:, hf * LANES:(hf + 1) * LANES]
        b_s[hf] = bx[:, hf * LANES:(hf + 1) * LANES]

    def slab(ref, k):
        return jnp.concatenate([ref[hf, pl.ds(k, ng, stride=SUBLANES), :] for hf in range(nl)], axis=-1)

    order = list(range(SUBLANES))[::-1] if reverse else list(range(SUBLANES))
    p = None
    ac = None
    for k in order:
        a_k = slab(a_s, k)
        b_k = slab(b_s, k)
        if p is None:
            p, ac = b_k, a_k
        else:
            p = a_k * p + b_k
            ac = a_k * ac
        p_s[k] = p
        ac_s[k] = ac

    k_last = order[-1]

    def body(g, c):
        gi = (ng - 1 - g) if reverse else g
        c_s[pl.ds(gi, 1), :] = c
        return ac_s[k_last, pl.ds(gi, 1), :] * c + p_s[k_last, pl.ds(gi, 1), :]

    carry[...] = lax.fori_loop(0, ng, body, carry[...], unroll=True)
    c_in = c_s[...]
    for k in order:
        h_k = p_s[k] + ac_s[k] * c_in
        for hf in range(nl):
            h_s[hf, pl.ds(k, ng, stride=SUBLANES), :] = h_k[:, hf * LANES:(hf + 1) * LANES]
    h = jnp.concatenate([h_s[hf] for hf in range(nl)], axis=-1)
    if reverse:
        o_ref[...] = h
    else:
        o_ref[...] = ((h + hb_ref[...]) * _gelu_tanh(ya_ref[...])).astype(o_ref.dtype)


def _rglru(u, bsz, seq, reverse, cw, cb, wr, br, wi, bi, lam, hb=None):
    ts = 1024
    ns = seq // ts
    t = bsz * seq
    ng = ts // SUBLANES
    xm, xp, xn = _row_specs(ts, ns, t, CB_XA, reverse)
    in_specs = [xm, xp, xn, _const_spec((4, W_GROUP)), _const_spec((1, W_GROUP)),
                _const_spec((W_GROUP, W_GROUP)), _const_spec((1, W_GROUP)),
                _const_spec((W_GROUP, W_GROUP)), _const_spec((1, W_GROUP)), _const_spec((1, W_GROUP))]
    args = [u, u, u, cw, cb, wr, br, wi, bi, lam]
    halves = (W_GROUP // LANES, ts, LANES)
    scratch = [pltpu.VMEM(halves, F32), pltpu.VMEM(halves, F32),
               pltpu.VMEM((SUBLANES, ng, W_GROUP), F32), pltpu.VMEM((SUBLANES, ng, W_GROUP), F32),
               pltpu.VMEM((ng, W_GROUP), F32), pltpu.VMEM((1, W_GROUP), F32), pltpu.VMEM(halves, F32)]
    if reverse:
        out_dtype = F32
        out_spec = pl.BlockSpec((ts, W_GROUP), lambda b, i: (b * ns + ns - 1 - i, 0))
    else:
        ya = _row_specs(ts, ns, t, CB_YA, False)[0]
        in_specs += [ya, pl.BlockSpec((ts, W_GROUP), lambda b, i: (b * ns + i, 0))]
        args += [u, hb]
        out_dtype = BF16
        out_spec = pl.BlockSpec((ts, W_GROUP), lambda b, i: (b * ns + i, 0))
    return pl.pallas_call(
        functools.partial(_rglru_kernel, reverse),
        grid=(bsz, ns),
        in_specs=in_specs,
        out_specs=out_spec,
        out_shape=jax.ShapeDtypeStruct((t, W_GROUP), out_dtype),
        scratch_shapes=scratch,
        compiler_params=_params(("parallel", "arbitrary")),
        name="rglru_bwd" if reverse else "rglru_fwd",
    )(*args)


GDN_PREP_BLOCK = 512
GDN_SCAN_BLOCK = 512


def _gdn_prep_kernel(qm, qp, qn, km, kp, kn, vm, vp, vn, ab_ref, cwq, cwk, cwv, alog_ref, dtb_ref, eb_ref, eg_ref,
                     *outs):
    ts = qm.shape[0]
    nc = ts // CHUNK
    c = CHUNK
    i = pl.program_id(1)
    first = i == 0
    last = i == pl.num_programs(1) - 1
    out_refs = (outs[0:5], outs[5:10])

    ones_h = _head_ones()
    q = _silu(_conv4(qm[...], qp[...], qn[...], cwq, first, last))
    k = _silu(_conv4(km[...], kp[...], kn[...], cwk, first, last))
    v = _silu(_conv4(vm[...], vp[...], vn[...], cwv, first, last))
    q = q * lax.rsqrt(_dot_sel(q * q, ones_h, 2) + EPS) * (HEAD_DIM ** -0.5)
    k = k * lax.rsqrt(_dot_sel(k * k, ones_h, 2) + EPS)
    ab = ab_ref[...]
    sig_parts = _split_bf16(_sigmoid(ab), 2)
    g_parts = _split_bf16(-jnp.exp(alog_ref[...]) * _softplus(ab + dtb_ref[...]), 3)
    beta = [sum(_dot(p, eb_ref[d]) for p in sig_parts) for d in range(2)]
    gate = [sum(_dot(p, eg_ref[d]) for p in g_parts) for d in range(2)]

    ii = _iota((c, W_GROUP), 0)
    jj = _iota((c, W_GROUP), 1) % c
    r2 = _iota((c, c), 0)
    c2 = _iota((c, c), 1)
    incl = (jj <= ii, jj >= ii)
    strict = (jj < ii, jj > ii)
    tri = ((c2 <= r2).astype(BF16), (c2 >= r2).astype(BF16))
    ucat = ((ii > jj).astype(F32), (ii < jj).astype(F32))
    eye_cat = (ii == jj).astype(F32)
    blk = _iota((W_GROUP, W_GROUP), 0) // HEAD_DIM == _iota((W_GROUP, W_GROUP), 1) // HEAD_DIM

    sls = [slice(ci * c, (ci + 1) * c) for ci in range(nc)]
    kq = [_dot_nt(jnp.concatenate([k[sl], q[sl]], axis=0).astype(BF16), _stack4(k[sl].astype(BF16), blk))
          for sl in sls]
    units = [(d, ci) for ci in range(nc) for d in range(2)]
    gd = [_sel_dot(tri[d], jnp.concatenate([gate[d][sls[ci]], gate[d][sls[ci]] * ucat[d]], axis=1), 3)
          for d, ci in units]
    gcum = [x[:, :W_GROUP] for x in gd]
    decay = [jnp.exp(x[:, W_GROUP:]) for x in gd]
    lmat = [jnp.where(strict[d], beta[d][sls[ci]] * kq[ci][:c] * dec, 0.0) for (d, ci), dec in zip(units, decay)]
    attn = [jnp.where(incl[d], kq[ci][c:] * dec, 0.0) for (d, ci), dec in zip(units, decay)]
    pinv = [eye_cat - l for l in lmat]
    lb = [l.astype(BF16) for l in lmat]
    lpow = [_dot(x, _stack4(x, blk)) for x in lb]
    for lvl in range(5):
        z = [_stack4(x.astype(BF16), blk) for x in lpow]
        if lvl < 4:
            r = [_dot(jnp.concatenate([p, lp], axis=0).astype(BF16), zz) for p, lp, zz in zip(pinv, lpow, z)]
            pinv = [p + x[:c] for p, x in zip(pinv, r)]
            lpow = [x[c:] for x in r]
        else:
            pinv = [p + _dot(p.astype(BF16), zz) for p, zz in zip(pinv, z)]
    egc = [jnp.exp(x) for x in gcum]
    pb = [p.astype(BF16) for p in pinv]
    for n, (d, ci) in enumerate(units):
        sl = sls[ci]
        u_ref, wq_ref, at_ref, kg_ref, ds_ref = out_refs[d]
        bc = beta[d][sl]
        u_ref[sl, :] = _dot(pb[n], _stack4((v[sl] * bc).astype(BF16), blk))
        w_key = _dot(pb[n], _stack4((k[sl] * bc * egc[n]).astype(BF16), blk))
        g_last = gcum[n][0:1, :] if d else gcum[n][c - 1:c, :]
        wq_ref[2 * ci * c:(2 * ci + 1) * c, :] = w_key.astype(BF16)
        wq_ref[(2 * ci + 1) * c:(2 * ci + 2) * c, :] = (q[sl] * egc[n]).astype(BF16)
        at_ref[sl, :] = attn[n].astype(BF16)
        kg_ref[sl, :] = (k[sl] * jnp.exp(g_last - gcum[n])).astype(BF16)
        ds_ref[ci] = jnp.exp(g_last)


def _gdn_prep(u, bsz, seq, cwq, cwk, cwv, alog, dtb, eb, eg):
    ts = GDN_PREP_BLOCK
    ns = seq // ts
    t = bsz * seq
    nc = ts // CHUNK
    specs = []
    for cb in (CB_QB, CB_KB, CB_VB):
        specs += list(_row_specs(ts, ns, t, cb, False))
    specs.append(pl.BlockSpec((ts, LANES), lambda b, i: (b * ns + i, CB128_AB)))
    specs += [_const_spec((4, W_GROUP))] * 3
    specs += [_const_spec((1, LANES))] * 2 + [_const_spec((2, LANES, W_GROUP))] * 2

    def row(b, i):
        return (b * ns + i, 0)

    one_dir_specs = [pl.BlockSpec((ts, W_GROUP), row), pl.BlockSpec((2 * ts, W_GROUP), row),
                     pl.BlockSpec((ts, W_GROUP), row), pl.BlockSpec((ts, W_GROUP), row),
                     pl.BlockSpec((nc, 1, W_GROUP), lambda b, i: (b * ns + i, 0, 0))]
    one_dir_shapes = [jax.ShapeDtypeStruct((t, W_GROUP), F32), jax.ShapeDtypeStruct((2 * t, W_GROUP), BF16),
                      jax.ShapeDtypeStruct((t, W_GROUP), BF16), jax.ShapeDtypeStruct((t, W_GROUP), BF16),
                      jax.ShapeDtypeStruct((t // CHUNK, 1, W_GROUP), F32)]
    return pl.pallas_call(
        _gdn_prep_kernel,
        grid=(bsz, ns),
        in_specs=specs,
        out_specs=one_dir_specs * 2,
        out_shape=one_dir_shapes * 2,
        compiler_params=_params(("parallel", "parallel")),
        name="gdn_prep",
    )(*([u] * 10 + [cwq, cwk, cwv, alog, dtb, eb, eg]))


def _gdn_scan_kernel(group_sizes, periods, *refs):
    n_groups = len(group_sizes)
    ins = refs[:10 * n_groups]
    outs = refs[10 * n_groups:12 * n_groups]
    s_ref = refs[12 * n_groups]
    ts = ins[0].shape[1]
    nc = ts // CHUNK
    c = CHUNK

    chains = []
    for gi, nb in enumerate(group_sizes):
        first_chain = len(chains)
        for b in range(nb):
            for d in range(2):
                chains.append(tuple(ins[10 * gi + 5 * d:10 * gi + 5 * d + 5]) + (outs[2 * gi + d], d, b))

        @pl.when(pl.program_id(0) % periods[gi] == 0)
        def _(lo=first_chain, hi=len(chains)):
            s_ref[lo:hi] = jnp.zeros((hi - lo, W_GROUP, W_GROUP), F32)

    blk = _iota((W_GROUP, W_GROUP), 0) // HEAD_DIM == _iota((W_GROUP, W_GROUP), 1) // HEAD_DIM
    state = [s_ref[n] for n in range(len(chains))]
    for step in range(nc):
        pos = (step, nc - 1 - step)
        ws = [_dot(wq[b, 2 * pos[d] * c:(2 * pos[d] + 2) * c, :], s.astype(BF16))
              for (u_, wq, at, kg, ds, o, d, b), s in zip(chains, state)]
        vnb = [(u_[b, pos[d] * c:(pos[d] + 1) * c, :] - x[:c]).astype(BF16)
               for (u_, wq, at, kg, ds, o, d, b), x in zip(chains, ws)]
        for (u_, wq, at, kg, ds, o, d, b), x, vn in zip(chains, ws, vnb):
            sl = slice(pos[d] * c, (pos[d] + 1) * c)
            o[b, sl, :] = x[c:] + _dot(at[b, sl, :], _stack4(vn, blk))
        state = [s * ds[b, pos[d]] + jnp.where(blk, _dot_tn(kg[b, pos[d] * c:(pos[d] + 1) * c, :], vn), 0.0)
                 for (u_, wq, at, kg, ds, o, d, b), s, vn in zip(chains, state, vnb)]
    for n, s in enumerate(state):
        s_ref[n] = s


def _gdn_scan(trunks):
    ts = GDN_SCAN_BLOCK
    nc = ts // CHUNK
    n_steps = max(seq // ts for _, _, seq in trunks)
    specs, args, out_specs, out_shapes, group_sizes, periods = [], [], [], [], [], []
    for prep, bsz, seq in trunks:
        ns = seq // ts
        assert n_steps % ns == 0 and bsz % (n_steps // ns) == 0
        nb = bsz // (n_steps // ns)
        group_sizes.append(nb)
        periods.append(ns)
        for d in range(2):
            u_, wq, at, kg, ds = prep[5 * d:5 * d + 5]
            args += [u_.reshape(bsz, seq, W_GROUP), wq.reshape(bsz, 2 * seq, W_GROUP),
                     at.reshape(bsz, seq, W_GROUP), kg.reshape(bsz, seq, W_GROUP),
                     ds.reshape(bsz, seq // CHUNK, 1, W_GROUP)]

            def tb(i, ns=ns, d=d):
                return (ns - 1 - i % ns) if d else (i % ns)

            def blk3(rows, nb=nb, ns=ns, tb=tb):
                return pl.BlockSpec((nb, rows, W_GROUP), lambda i: (i // ns, tb(i), 0))

            specs += [blk3(ts), blk3(2 * ts), blk3(ts), blk3(ts),
                      pl.BlockSpec((nb, nc, 1, W_GROUP), lambda i, ns=ns, tb=tb: (i // ns, tb(i), 0, 0))]
            out_specs.append(blk3(ts))
            out_shapes.append(jax.ShapeDtypeStruct((bsz, seq, W_GROUP), F32))
    outs = pl.pallas_call(
        functools.partial(_gdn_scan_kernel, tuple(group_sizes), tuple(periods)),
        grid=(n_steps,),
        in_specs=specs,
        out_specs=out_specs,
        out_shape=out_shapes,
        scratch_shapes=[pltpu.VMEM((2 * sum(group_sizes), W_GROUP, W_GROUP), F32)],
        compiler_params=_params(("arbitrary",)),
        name="gdn_scan",
    )(*args)
    return [(outs[2 * n].reshape(bsz * seq, W_GROUP), outs[2 * n + 1].reshape(bsz * seq, W_GROUP))
            for n, (_, bsz, seq) in enumerate(trunks)]


RET_CHUNK = 128
RET_BLOCK = 1024


def _ret_kernel(reverse, *refs):
    if reverse:
        (q_ref, k_ref, v_ref, cos_ref, sin_ref, lgq_ref, lgv_ref, o_ref, s_ref) = refs
    else:
        (q_ref, k_ref, v_ref, cos_ref, sin_ref, lgq_ref, lgv_ref, lgfc_ref, lgbc_ref, g_ref, ob_ref, rn_ref,
         o_ref, s_ref, dm_ref) = refs
    ts = q_ref.shape[0]
    cr = RET_CHUNK
    nc = ts // cr
    nk = N_HEADS * cr

    @pl.when(pl.program_id(1) == 0)
    def _():
        s_ref[...] = jnp.zeros_like(s_ref)
        if not reverse:
            ii = _iota((cr, nk), 0)
            jj = _iota((cr, nk), 1) % cr
            dist = (ii - jj).astype(F32)
            lower, upper = jj <= ii, jj >= ii
            dm_ref[...] = (jnp.where(lower, jnp.exp(jnp.where(lower, dist * lgfc_ref[...], 0.0)), 0.0)
                           + jnp.where(upper, jnp.exp(jnp.where(upper, -dist * lgbc_ref[...], 0.0)), 0.0))

    cos = cos_ref[...]
    sin = sin_ref[...]

    def rope(t):
        t1, t2 = t[:, :LANES], t[:, LANES:]
        return jnp.concatenate([t1 * cos - t2 * sin, t1 * sin + t2 * cos], axis=-1)

    q = rope(q_ref[...])
    k = rope(k_ref[...]) * (HEAD_DIM ** -0.5)
    v = v_ref[...]
    lgq = lgq_ref[...]
    lgv = lgv_ref[...]
    pos = _iota((cr, W_GROUP), 0).astype(F32)
    rows_qk = (_iota((W_GROUP, W_GROUP), 0) % LANES) // 32
    blk_state = rows_qk == _iota((W_GROUP, W_GROUP), 1) // HEAD_DIM
    if reverse:
        q_fac = jnp.exp((cr - pos) * lgq)
        k_fac = jnp.exp(pos * lgq)
    else:
        q_fac = jnp.exp((pos + 1.0) * lgq)
        k_fac = jnp.exp((cr - 1.0 - pos) * lgq)
    s_fac = jnp.exp(cr * lgv)
    sls = [slice(ci * cr, (ci + 1) * cr) for ci in range(nc)]
    kv = [jnp.where(blk_state, _dot_tn((k[sl] * k_fac).astype(BF16), v[sl].astype(BF16)), 0.0) for sl in sls]
    s = s_ref[...]
    cross = [None] * nc
    for ci in (range(nc - 1, -1, -1) if reverse else range(nc)):
        cross[ci] = _dot((q[sls[ci]] * q_fac).astype(BF16), s.astype(BF16))
        s = s * s_fac + kv[ci]
    s_ref[...] = s
    if reverse:
        o_ref[...] = jnp.concatenate(cross, axis=0)
        return
    stack_head = _iota((nk, W_GROUP), 0) // cr
    k_mask = stack_head == (_iota((nk, W_GROUP), 1) % LANES) // 32
    v_mask = stack_head == _iota((nk, W_GROUP), 1) // HEAD_DIM
    qk = [_dot_nt(q[sl].astype(BF16), _stack4(k[sl].astype(BF16), k_mask)) for sl in sls]
    dm = dm_ref[...]
    intra = [_dot((x * dm).astype(BF16), _stack4(v[sl].astype(BF16), v_mask)) for x, sl in zip(qk, sls)]
    o = jnp.concatenate([a + b for a, b in zip(intra, cross)], axis=0) + ob_ref[...]
    ones_h = _head_ones()
    oc = o - _dot_sel(o, ones_h, 3) * (1.0 / HEAD_DIM)
    var = _dot_sel(oc * oc, ones_h, 2) * (1.0 / HEAD_DIM)
    o_ref[...] = (oc * lax.rsqrt(var + EPS) * rn_ref[...] * _silu(g_ref[...])).astype(o_ref.dtype)


def _ret(u, bsz, seq, reverse, cos, sin, lgq, lgv, lgfc=None, lgbc=None, rn=None, ob=None):
    cr = RET_CHUNK
    ts = RET_BLOCK
    ns = seq // ts
    t = bsz * seq
    tbf = (lambda i: ns - 1 - i) if reverse else (lambda i: i)

    def col(cb):
        return pl.BlockSpec((ts, W_GROUP), lambda b, i: (b * ns + tbf(i), cb))

    tab = pl.BlockSpec((ts, LANES), lambda b, i: (tbf(i), 0))
    specs = [col(CB_QC), col(CB_KC), col(CB_VC), tab, tab, _const_spec((1, W_GROUP)), _const_spec((1, W_GROUP))]
    args = [u, u, u, cos, sin, lgq, lgv]
    scratch = [pltpu.VMEM((W_GROUP, W_GROUP), F32)]
    if reverse:
        out_dtype = F32
    else:
        specs += [_const_spec((1, N_HEADS * cr)), _const_spec((1, N_HEADS * cr)), col(CB_GC),
                  pl.BlockSpec((ts, W_GROUP), lambda b, i: (b * ns + i, 0)), _const_spec((1, W_GROUP))]
        args += [lgfc, lgbc, u, ob, rn]
        scratch.append(pltpu.VMEM((cr, N_HEADS * cr), F32))
        out_dtype = BF16
    return pl.pallas_call(
        functools.partial(_ret_kernel, reverse),
        grid=(bsz, ns),
        in_specs=specs,
        out_specs=pl.BlockSpec((ts, W_GROUP), lambda b, i: (b * ns + tbf(i), 0)),
        out_shape=jax.ShapeDtypeStruct((t, W_GROUP), out_dtype),
        scratch_shapes=scratch,
        compiler_params=_params(("parallel", "arbitrary")),
        name="ret_bwd" if reverse else "ret_fwd",
    )(*args)


NA_GROUP = 16
NA_WIN = NA_KH * GRID_W
NA_ROWS_PER_PASS = 4


def _na_kernel(rows, q_ref, kp_ref, kc_ref, kn_ref, vp_ref, vc_ref, vn_ref, tab_ref, o_ref, kw, vw):
    g = pl.program_id(1)
    blk = NA_GROUP * GRID_W
    kw[0:blk, :] = kp_ref[...]
    kw[blk:2 * blk, :] = kc_ref[...]
    kw[2 * blk:3 * blk, :] = kn_ref[...]
    vw[0:blk, :] = vp_ref[...]
    vw[blk:2 * blk, :] = vc_ref[...]
    vw[2 * blk:3 * blk, :] = vn_ref[...]
    row_head = _iota((N_HEADS * GRID_W, W_GROUP), 0) // GRID_W
    lane_head = _iota((N_HEADS * GRID_W, W_GROUP), 1) // HEAD_DIM
    blk = row_head == lane_head
    out_head = _iota((GRID_W, W_GROUP), 1) // HEAD_DIM
    for j0 in range(0, NA_GROUP, NA_ROWS_PER_PASS):
        js = list(range(j0, j0 + NA_ROWS_PER_PASS))
        vwins, sc = [], []
        for j in js:
            r = g * NA_GROUP + j
            r0 = jnp.clip(r - NA_KH // 2, 0, rows - NA_KH)
            off = pl.multiple_of((r0 - g * NA_GROUP + NA_GROUP) * GRID_W, GRID_W)
            qst = _stack4(q_ref[j * GRID_W:(j + 1) * GRID_W, :], blk)
            vwins.append(vw[pl.ds(off, NA_WIN), :])
            sc.append(_dot_nt(qst, kw[pl.ds(off, NA_WIN), :]) + tab_ref[r - r0])
        mx = [jnp.max(s, axis=-1, keepdims=True) for s in sc]
        pr = [jnp.exp(s - m) for s, m in zip(sc, mx)]
        den = [jnp.sum(p, axis=-1, keepdims=True) for p in pr]
        pv = [_dot(p.astype(BF16), vw_) / l for p, vw_, l in zip(pr, vwins, den)]
        for j, o in zip(js, pv):
            res = o[0:GRID_W]
            for h in range(1, N_HEADS):
                res = jnp.where(out_head == h, o[h * GRID_W:(h + 1) * GRID_W], res)
            o_ref[j * GRID_W:(j + 1) * GRID_W, :] = res.astype(o_ref.dtype)


def _na(qn, kn, vn, table, bsz, seq):
    rows = seq // GRID_W
    ng = rows // NA_GROUP
    blk = NA_GROUP * GRID_W
    t = bsz * seq
    cur = pl.BlockSpec((blk, W_GROUP), lambda b, g: (b * ng + g, 0))
    prev = pl.BlockSpec((blk, W_GROUP), lambda b, g: (b * ng + jnp.maximum(g - 1, 0), 0))
    nxt = pl.BlockSpec((blk, W_GROUP), lambda b, g: (b * ng + jnp.minimum(g + 1, ng - 1), 0))
    return pl.pallas_call(
        functools.partial(_na_kernel, rows),
        grid=(bsz, ng),
        in_specs=[cur, prev, cur, nxt, prev, cur, nxt, _const_spec((NA_KH, N_HEADS * GRID_W, NA_WIN))],
        out_specs=cur,
        out_shape=jax.ShapeDtypeStruct((t, W_GROUP), BF16),
        scratch_shapes=[pltpu.VMEM((3 * blk, W_GROUP), BF16), pltpu.VMEM((3 * blk, W_GROUP), BF16)],
        compiler_params=_params(("parallel", "parallel")),
        name="na",
    )(qn, kn, kn, kn, vn, vn, vn, table)


def _na_table(rpb):
    didx = np.arange(NA_KH)[:, None]
    kk = np.arange(NA_KH)[None, :]
    dr = (kk + (NA_KH - 1) - didx).reshape(-1)
    cc = np.arange(GRID_W)[:, None]
    kc = np.arange(GRID_W)[None, :]
    c0 = np.clip(cc - NA_KW // 2, 0, GRID_W - NA_KW)
    valid = (kc >= c0) & (kc < c0 + NA_KW)
    dc = np.clip(kc - cc + (NA_KW - 1), 0, 2 * NA_KW - 2).reshape(-1)
    row_sel = np.zeros((dr.size, 2 * NA_KH - 1), np.float32)
    row_sel[np.arange(dr.size), dr] = 1.0
    col_sel = np.zeros((2 * NA_KW - 1, dc.size), np.float32)
    col_sel[dc, np.arange(dc.size)] = 1.0
    bias = jnp.einsum('ar,hrs,sb->hab', row_sel, rpb, col_sel, precision=lax.Precision.HIGHEST)
    bias = bias.reshape(N_HEADS, NA_KH, NA_KH, GRID_W, GRID_W)
    tab = jnp.where(valid[None, None, None], bias, NEG_BIG)
    return jnp.transpose(tab, (1, 0, 3, 2, 4)).reshape(NA_KH, N_HEADS * GRID_W, NA_WIN)


FF_CHUNK = 256
POST_HALO = 16


def _mix_residual(x, a, bf, bb, z, gn, c, d, w_ref):
    ob = bf + bb
    ms = _dot_sel(ob * ob, _head_ones(), 2) * (1.0 / HEAD_DIM)
    out_b = (ob * lax.rsqrt(ms + EPS) * gn * _silu(z)).astype(BF16)
    acc = x
    for gi, m in enumerate((a, out_b, c, d)):
        acc = acc + _dot(m, w_ref[gi * W_GROUP:(gi + 1) * W_GROUP, :])
    return acc


def _post_kernel(*refs):
    streams = [refs[3 * n:3 * n + 3] for n in range(7)]
    (p_ref, gn_ref, wo_ref, n2_ref, wg_ref, wu_ref, cw_ref, cb_ref, wd_ref, n3_ref, pg_ref, pp_ref,
     o_ref, act_s) = refs[21:]
    tm = streams[0][0].shape[0]
    i = pl.program_id(1)
    first = i == 0
    last = i == pl.num_programs(1) - 1
    gn = gn_ref[...]
    mains = [s[0][...] for s in streams]
    halos = [jnp.concatenate([s[1][...], s[2][...]], axis=0) for s in streams]
    x = _mix_residual(mains[0], *mains[1:5], gn, *mains[5:], wo_ref)
    xh = _mix_residual(halos[0], *halos[1:5], gn, *halos[5:], wo_ref)
    n2 = n2_ref[...]
    h2f = _rms(x, n2)
    h2h = _rms(xh, n2)
    h_before = jnp.where(first, 0.0, h2h[POST_HALO - SUBLANES:POST_HALO])
    h_after = jnp.where(last, 0.0, h2h[POST_HALO:POST_HALO + SUBLANES])
    h2 = h2f.astype(BF16)
    h2e = jnp.concatenate([h_before, h2f, h_after], axis=0).astype(BF16)
    for f in range(D_FF // FF_CHUNK):
        fs = slice(f * FF_CHUNK, (f + 1) * FF_CHUNK)
        ge = _dot(h2e, wg_ref[:, fs])
        ne = tm + 2 * SUBLANES
        g_prev = pltpu.roll(ge, 1, 0)[SUBLANES:SUBLANES + tm]
        g_next = pltpu.roll(ge, ne - 1, 0)[SUBLANES:SUBLANES + tm]
        gate = (cw_ref[0:1, fs] * g_prev + cw_ref[1:2, fs] * ge[SUBLANES:SUBLANES + tm]
                + cw_ref[2:3, fs] * g_next + cb_ref[:, fs])
        up = _dot(h2, wu_ref[:, fs])
        act_s[:, fs] = (_gelu_tanh(gate) * up).astype(BF16)
    acc = x + _dot(act_s[...], wd_ref[...])
    h3 = _rms(acc, n3_ref[...]).astype(BF16)
    gate3 = _sigmoid(_dot(h3, pg_ref[...]))
    o_ref[...] = acc + gate3 * _dot(p_ref[...].astype(BF16), pp_ref[...])


def _post(x, u, oa, obf, obb, gn, oc, od, w_out, p, p_layer, bsz, seq, n2, wg, wu, cw, cb, wd, n3, pg, pp):
    tm = 512
    ns = seq // tm
    t = bsz * seq
    specs = list(_row_specs(tm, ns, t, 0, False, width=D_MODEL, halo=POST_HALO))
    arrays = [x] * 3
    for arr, cb_ in ((oa, 0), (obf, 0), (obb, 0), (u, CB_ZB), (oc, 0), (od, 0)):
        specs += list(_row_specs(tm, ns, t, cb_, False, halo=POST_HALO))
        arrays += [arr] * 3
    specs += [pl.BlockSpec((tm, PLE_DIM), lambda b, i: (p_layer * (t // tm) + b * ns + i, 0)),
              _const_spec((1, W_GROUP)), _const_spec((D_MODEL, D_MODEL)),
              _const_spec((1, D_MODEL)), _const_spec((D_MODEL, D_FF)), _const_spec((D_MODEL, D_FF)),
              _const_spec((3, D_FF)), _const_spec((1, D_FF)), _const_spec((D_FF, D_MODEL)),
              _const_spec((1, D_MODEL)), _const_spec((D_MODEL, D_MODEL)), _const_spec((PLE_DIM, D_MODEL))]
    return pl.pallas_call(
        _post_kernel,
        grid=(bsz, ns),
        in_specs=specs,
        out_specs=pl.BlockSpec((tm, D_MODEL), lambda b, i: (b * ns + i, 0)),
        out_shape=jax.ShapeDtypeStruct((t, D_MODEL), F32),
        compiler_params=_params(("parallel", "parallel")),
        scratch_shapes=[pltpu.VMEM((tm, D_FF), BF16)],
        name="post",
    )(*arrays, p, gn, w_out, n2, wg, wu, cw, cb, wd, n3, pg, pp)


def _block_diag(w):
    h, n, _ = w.shape
    eye = jnp.eye(h, dtype=w.dtype)
    return (eye[:, None, :, None] * w[:, :, None, :]).reshape(h * n, h * n)


def _row(v):
    return v.reshape(1, -1).astype(F32)


def _lane_pad(v8, offset):
    return jnp.zeros((1, LANES), F32).at[0, offset:offset + 2 * N_HEADS].set(v8.reshape(-1).astype(F32))


def _expander(offset):
    e = np.zeros((LANES, W_GROUP), np.float32)
    for h in range(N_HEADS):
        e[offset + h, h * HEAD_DIM:(h + 1) * HEAD_DIM] = 1.0
    return jnp.asarray(e, BF16)


def _rope_tables(max_seq):
    half = HEAD_DIM // 2
    inv_freq = ROPE_BASE ** (-jnp.arange(half, dtype=F32) / half)
    ang = jnp.arange(max_seq, dtype=F32)[:, None] * inv_freq[None, :]
    return jnp.tile(jnp.cos(ang), (1, N_HEADS)), jnp.tile(jnp.sin(ang), (1, N_HEADS))


def _layer_front(x, lw, bsz, seq, rope):
    w_in = _reorder_in_cols(lw['w_in'].astype(BF16))
    u, qn, kn, vn = _inproj(x, _row(lw['norm1']), w_in,
                            _row(jnp.tile(lw['na_qnorm'], N_HEADS)), _row(jnp.tile(lw['na_knorm'], N_HEADS)))

    cw_a, cb_a = lw['conv_a_w'].astype(F32), _row(lw['conv_a_b'])
    h_dir = None
    for d in (1, 0):
        h_dir = _rglru(u, bsz, seq, d == 1, cw_a, cb_a,
                       _block_diag(lw['lru_wr'][d]).astype(BF16), _row(lw['lru_br'][d]),
                       _block_diag(lw['lru_wi'][d]).astype(BF16), _row(lw['lru_bi'][d]),
                       _row(lw['lru_lambda'][d]), hb=h_dir)
    out_a = h_dir

    gconv = lw['gdn_conv'].astype(F32)
    cwq, cwk, cwv = gconv[:, :W_GROUP], gconv[:, W_GROUP:2 * W_GROUP], gconv[:, 2 * W_GROUP:]
    alog = _lane_pad(lw['gdn_a_log'], 2 * N_HEADS)
    dtb = _lane_pad(lw['gdn_dt_bias'], 2 * N_HEADS)
    eb = jnp.stack([_expander(d * N_HEADS) for d in range(2)])
    eg = jnp.stack([_expander(2 * N_HEADS + d * N_HEADS) for d in range(2)])
    prep = _gdn_prep(u, bsz, seq, cwq, cwk, cwv, alog, dtb, eb, eg)

    half = HEAD_DIM // 2
    cos, sin = rope
    log_gamma = jnp.log1p(-jnp.exp2(-lw['ret_decay'].astype(F32)))
    rn = _row(jnp.tile(lw['ret_norm'], N_HEADS))
    o_dir = None
    for d in (1, 0):
        lg = log_gamma[d]
        lgq = _row(jnp.tile(jnp.repeat(lg, half), 2))
        lgv = _row(jnp.repeat(lg, HEAD_DIM))
        if d == 1:
            o_dir = _ret(u, bsz, seq, True, cos, sin, lgq, lgv)
        else:
            o_dir = _ret(u, bsz, seq, False, cos, sin, lgq, lgv,
                         lgfc=_row(jnp.repeat(log_gamma[0], RET_CHUNK)), lgbc=_row(jnp.repeat(log_gamma[1], RET_CHUNK)),
                         rn=rn, ob=o_dir)
    out_c = o_dir

    out_d = _na(qn, kn, vn, _na_table(lw['na_rpb'].astype(F32)), bsz, seq)
    return dict(u=u, out_a=out_a, out_c=out_c, out_d=out_d, prep=prep)


def _layer_back(x, p, p_layer, front, o_b, lw, bsz, seq):
    return _post(x, front['u'], front['out_a'], o_b[0], o_b[1], _row(jnp.tile(lw['gdn_norm'], N_HEADS)),
                 front['out_c'], front['out_d'], lw['w_out'].astype(BF16), p, p_layer, bsz, seq,
                 _row(lw['norm2']), lw['ffn_wg'].astype(BF16), lw['ffn_wu'].astype(BF16),
                 lw['ffn_conv_w'].astype(F32), _row(lw['ffn_conv_b']), lw['ffn_wd'].astype(BF16),
                 _row(lw['norm3']), lw['ple_gate'].astype(BF16), lw['ple_proj'].astype(BF16))


def _layer_multi(xs, ps, p_layer, lw, shapes):
    rope = _rope_tables(max(seq for _, seq in shapes))
    fronts = [_layer_front(x, lw, bsz, seq, rope) for x, (bsz, seq) in zip(xs, shapes)]
    scans = _gdn_scan([(f['prep'], bsz, seq) for f, (bsz, seq) in zip(fronts, shapes)])
    return [_layer_back(x, p, p_layer, f, o_b, lw, bsz, seq)
            for x, p, f, o_b, (bsz, seq) in zip(xs, ps, fronts, scans, shapes)]


def _layer(x, p, lw, bsz, seq):
    return _layer_multi([x], [p], 0, lw, [(bsz, seq)])[0]


_WEIGHT_NAMES = ('norm1', 'norm2', 'norm3', 'w_in', 'w_out', 'conv_a_w', 'conv_a_b', 'lru_wr', 'lru_br',
                 'lru_wi', 'lru_bi', 'lru_lambda', 'gdn_conv', 'gdn_a_log', 'gdn_dt_bias', 'gdn_norm',
                 'ret_decay', 'ret_norm', 'na_qnorm', 'na_knorm', 'na_rpb', 'ffn_wg', 'ffn_wu',
                 'ffn_conv_w', 'ffn_conv_b', 'ffn_wd', 'ple_proj', 'ple_gate')


def _trunks(xs, ps, weights):
    shapes = [x.shape[:2] for x in xs]
    flat = [x.reshape(bsz * seq, D_MODEL) for x, (bsz, seq) in zip(xs, shapes)]
    p_rows = [p.reshape(DEPTH * bsz * seq, PLE_DIM) for p, (bsz, seq) in zip(ps, shapes)]
    for i in range(DEPTH):
        lw = {n: w[i] for n, w in weights.items()}
        flat = _layer_multi(flat, p_rows, i, lw, shapes)
    return tuple(x.reshape(bsz, seq, D_MODEL) for x, (bsz, seq) in zip(flat, shapes))


def kernel(x_prompt, x_sample, p_prompt, p_sample, norm1, norm2, norm3, w_in, w_out, conv_a_w, conv_a_b, lru_wr, lru_br, lru_wi, lru_bi, lru_lambda, gdn_conv, gdn_a_log, gdn_dt_bias, gdn_norm, ret_decay, ret_norm, na_qnorm, na_knorm, na_rpb, ffn_wg, ffn_wu, ffn_conv_w, ffn_conv_b, ffn_wd, ple_proj, ple_gate):
    weights = dict(zip(_WEIGHT_NAMES, (norm1, norm2, norm3, w_in, w_out, conv_a_w, conv_a_b, lru_wr, lru_br,
                                       lru_wi, lru_bi, lru_lambda, gdn_conv, gdn_a_log, gdn_dt_bias, gdn_norm,
                                       ret_decay, ret_norm, na_qnorm, na_knorm, na_rpb, ffn_wg, ffn_wu,
                                       ffn_conv_w, ffn_conv_b, ffn_wd, ple_proj, ple_gate)))
    return _trunks((x_prompt, x_sample), (p_prompt, p_sample), weights)
```

```python
import functools

import jax
import jax.numpy as jnp
import numpy as np
from jax import lax
from jax.experimental import pallas as pl
from jax.experimental.pallas import tpu as pltpu

F32 = jnp.float32
BF16 = jnp.bfloat16

D_MODEL = 1024
DEPTH = 2
PLE_DIM = 256
GRID_W = 64
W_GROUP = 256
HEAD_DIM = 64
N_HEADS = 4
LRU_C = 8.0
CHUNK = 64
ROPE_BASE = 10000.0
NA_KH = 8
NA_KW = 16
D_FF = 2816
EPS = 1e-6
NEG_BIG = -1e30

SUBLANES = 8
LANES = 128
VMEM_LIMIT = 56 * 1024 * 1024

CB_XA, CB_YA, CB_QB, CB_KB, CB_VB, CB_ZB, CB_QC, CB_KC, CB_VC, CB_GC = range(10)
CB128_AB = 20
U_COLS = 10 * W_GROUP + LANES
IN_COLS_PAD = U_COLS + 3 * W_GROUP
IN_COLS = 13 * W_GROUP + 4 * N_HEADS


def _reorder_in_cols(w):
    rows = w.shape[0]
    half = HEAD_DIM // 2

    def halves_first(cols):
        return cols.reshape(rows, N_HEADS, 2, half).transpose(0, 2, 1, 3).reshape(rows, W_GROUP)

    ab0 = 6 * W_GROUP
    qc0 = ab0 + 4 * N_HEADS
    qd0 = qc0 + 4 * W_GROUP
    parts = [w[:, :ab0], halves_first(w[:, qc0:qc0 + W_GROUP]), halves_first(w[:, qc0 + W_GROUP:qc0 + 2 * W_GROUP]),
             w[:, qc0 + 2 * W_GROUP:qd0], w[:, ab0:qc0], jnp.zeros((rows, LANES - 4 * N_HEADS), w.dtype), w[:, qd0:]]
    out = jnp.concatenate(parts, axis=1)
    assert out.shape[1] == IN_COLS_PAD
    return out


def _sigmoid(x):
    return 1.0 / (1.0 + jnp.exp(-x))


def _silu(x):
    return x * _sigmoid(x)


def _softplus(x):
    return jnp.maximum(x, 0.0) + jnp.log1p(jnp.exp(-jnp.abs(x)))


def _gelu_tanh(x):
    return x * (0.5 * (1.0 + jnp.tanh(0.7978845608028654 * (x + 0.044715 * (x * x * x)))))


def _rms(x, gain):
    ms = jnp.mean(x * x, axis=-1, keepdims=True)
    return x * lax.rsqrt(ms + EPS) * gain


def _dot(a, b):
    return jnp.dot(a, b, preferred_element_type=F32)


def _split_bf16(x, terms):
    parts = []
    for _ in range(terms):
        p = x.astype(BF16)
        parts.append(p)
        x = x - p.astype(F32)
    return parts


def _dot_sel(x, sel, terms):
    return sum(_dot(p, sel) for p in _split_bf16(x, terms))


def _sel_dot(sel, x, terms):
    return sum(_dot(sel, p) for p in _split_bf16(x, terms))


def _dot_nt(a, b):
    return lax.dot_general(a, b, (((1,), (1,)), ((), ())), preferred_element_type=F32)


def _dot_tn(a, b):
    return lax.dot_general(a, b, (((0,), (0,)), ((), ())), preferred_element_type=F32)


def _iota(shape, dim):
    return lax.broadcasted_iota(jnp.int32, shape, dim)


def _head_ones():
    return (_iota((W_GROUP, W_GROUP), 0) // HEAD_DIM == _iota((W_GROUP, W_GROUP), 1) // HEAD_DIM).astype(BF16)


def _stack4(x, mask):
    return jnp.where(mask, jnp.concatenate([x] * N_HEADS, axis=0), jnp.zeros((), x.dtype))


def _conv4(xm, xp, xn, w_ref, first, last):
    ts = xm.shape[0]
    xp = jnp.where(first, 0.0, xp)
    xn = jnp.where(last, 0.0, xn)
    xe = jnp.concatenate([xp, xm, xn], axis=0)
    n = ts + 2 * SUBLANES

    def tap(offset):
        rolled = xe if offset == 0 else pltpu.roll(xe, (-offset) % n, 0)
        return rolled[SUBLANES:SUBLANES + ts]

    out = w_ref[0:1, :] * tap(-2)
    out = out + w_ref[1:2, :] * tap(-1)
    out = out + w_ref[2:3, :] * tap(0)
    out = out + w_ref[3:4, :] * tap(1)
    return out


def _time_block(reverse):
    i = pl.program_id(1)
    n = pl.num_programs(1)
    tb = (n - 1 - i) if reverse else i
    return tb, tb == 0, tb == n - 1


def _row_specs(ts, ns, total_rows, col_block, reverse, width=W_GROUP, halo=SUBLANES):
    per_block = ts // halo
    last = total_rows // halo - 1

    def tb(i):
        return (ns - 1 - i) if reverse else i

    main = pl.BlockSpec((ts, width), lambda b, i: (b * ns + tb(i), col_block))
    prev = pl.BlockSpec((halo, width), lambda b, i: (jnp.maximum((b * ns + tb(i)) * per_block - 1, 0), col_block))
    nxt = pl.BlockSpec((halo, width), lambda b, i: (jnp.minimum((b * ns + tb(i) + 1) * per_block, last), col_block))
    return main, prev, nxt


def _const_spec(shape):
    nd = len(shape)
    return pl.BlockSpec(shape, lambda *_: (0,) * nd, pipeline_mode=pl.Buffered(1))


def _params(sem):
    return pltpu.CompilerParams(dimension_semantics=sem, vmem_limit_bytes=VMEM_LIMIT)


def _inproj_kernel(x_ref, g_ref, w_ref, gq_ref, gk_ref, u_ref, qd_ref, kd_ref, vd_ref):
    h = _rms(x_ref[...], g_ref[...]).astype(BF16)
    u_ref[...] = _dot(h, w_ref[:, :U_COLS])
    qkv = _dot(h, w_ref[:, U_COLS:])
    q, k = qkv[:, :W_GROUP], qkv[:, W_GROUP:2 * W_GROUP]
    ones_h = _head_ones()
    inv = 1.0 / HEAD_DIM
    qd_ref[...] = (q * lax.rsqrt(_dot_sel(q * q, ones_h, 2) * inv + EPS) * gq_ref[...] * (HEAD_DIM ** -0.5)).astype(BF16)
    kd_ref[...] = (k * lax.rsqrt(_dot_sel(k * k, ones_h, 2) * inv + EPS) * gk_ref[...]).astype(BF16)
    vd_ref[...] = qkv[:, 2 * W_GROUP:].astype(BF16)


def _inproj(x, gain, w, gq, gk):
    t = x.shape[0]
    tm = 1024
    grp = pl.BlockSpec((tm, W_GROUP), lambda i: (i, 0))
    grp_shape = jax.ShapeDtypeStruct((t, W_GROUP), BF16)
    return pl.pallas_call(
        _inproj_kernel,
        grid=(t // tm,),
        in_specs=[pl.BlockSpec((tm, D_MODEL), lambda i: (i, 0)),
                  _const_spec((1, D_MODEL)),
                  _const_spec((D_MODEL, IN_COLS_PAD)),
                  _const_spec((1, W_GROUP)), _const_spec((1, W_GROUP))],
        out_specs=[pl.BlockSpec((tm, U_COLS), lambda i: (i, 0)), grp, grp, grp],
        out_shape=[jax.ShapeDtypeStruct((t, U_COLS), F32), grp_shape, grp_shape, grp_shape],
        compiler_params=_params(("parallel",)),
        name="inproj",
    )(x, gain, w, gq, gk)


def _rglru_kernel(reverse, *refs):
    if reverse:
        (xm_ref, xp_ref, xn_ref, cw_ref, cb_ref, wr_ref, br_ref, wi_ref, bi_ref, lam_ref,
         o_ref, a_s, b_s, p_s, ac_s, c_s, carry, h_s) = refs
    else:
        (xm_ref, xp_ref, xn_ref, cw_ref, cb_ref, wr_ref, br_ref, wi_ref, bi_ref, lam_ref, ya_ref, hb_ref,
         o_ref, a_s, b_s, p_s, ac_s, c_s, carry, h_s) = refs
    ts = xm_ref.shape[0]
    ng = ts // SUBLANES
    tb, first, last = _time_block(reverse)

    @pl.when(pl.program_id(1) == 0)
    def _():
        carry[...] = jnp.zeros_like(carry)

    x = _conv4(xm_ref[...], xp_ref[...], xn_ref[...], cw_ref, first, last) + cb_ref[...]
    xb = x.astype(BF16)
    r = _sigmoid(_dot(xb, wr_ref[...]) + br_ref[...])
    ig = _sigmoid(_dot(xb, wi_ref[...]) + bi_ref[...])
    log_a = (-LRU_C) * r * _softplus(-lam_ref[...])
    a = jnp.exp(log_a)
    th = jnp.tanh(-log_a)
    bx = jnp.sqrt(2.0 * th / (1.0 + th)) * (ig * x)
    nl = W_GROUP // LANES
    for hf in range(nl):
        a_s[hf] = a[:, hf * LANES:(hf + 1) * LANES]
        b_s[hf] = bx[:, hf * LANES:(hf + 1) * LANES]

    def slab(ref, k):
        return jnp.concatenate([ref[hf, pl.ds(k, ng, stride=SUBLANES), :] for hf in range(nl)], axis=-1)

    order = list(range(SUBLANES))[::-1] if reverse else list(range(SUBLANES))
    p = None
    ac = None
    for k in order:
        a_k = slab(a_s, k)
        b_k = slab(b_s, k)
        if p is None:
            p, ac = b_k, a_k
        else:
            p = a_k * p + b_k
            ac = a_k * ac
        p_s[k] = p
        ac_s[k] = ac

    k_last = order[-1]

    def body(g, c):
        gi = (ng - 1 - g) if reverse else g
        c_s[pl.ds(gi, 1), :] = c
        return ac_s[k_last, pl.ds(gi, 1), :] * c + p_s[k_last, pl.ds(gi, 1), :]

    carry[...] = lax.fori_loop(0, ng, body, carry[...], unroll=True)
    c_in = c_s[...]
    for k in order:
        h_k = p_s[k] + ac_s[k] * c_in
        for hf in range(nl):
            h_s[hf, pl.ds(k, ng, stride=SUBLANES), :] = h_k[:, hf * LANES:(hf + 1) * LANES]
    h = jnp.concatenate([h_s[hf] for hf in range(nl)], axis=-1)
    if reverse:
        o_ref[...] = h
    else:
        o_ref[...] = ((h + hb_ref[...]) * _gelu_tanh(ya_ref[...])).astype(o_ref.dtype)


def _rglru(u, bsz, seq, reverse, cw, cb, wr, br, wi, bi, lam, hb=None):
    ts = 1024
    ns = seq // ts
    t = bsz * seq
    ng = ts // SUBLANES
    xm, xp, xn = _row_specs(ts, ns, t, CB_XA, reverse)
    in_specs = [xm, xp, xn, _const_spec((4, W_GROUP)), _const_spec((1, W_GROUP)),
                _const_spec((W_GROUP, W_GROUP)), _const_spec((1, W_GROUP)),
                _const_spec((W_GROUP, W_GROUP)), _const_spec((1, W_GROUP)), _const_spec((1, W_GROUP))]
    args = [u, u, u, cw, cb, wr, br, wi, bi, lam]
    halves = (W_GROUP // LANES, ts, LANES)
    scratch = [pltpu.VMEM(halves, F32), pltpu.VMEM(halves, F32),
               pltpu.VMEM((SUBLANES, ng, W_GROUP), F32), pltpu.VMEM((SUBLANES, ng, W_GROUP), F32),
               pltpu.VMEM((ng, W_GROUP), F32), pltpu.VMEM((1, W_GROUP), F32), pltpu.VMEM(halves, F32)]
    if reverse:
        out_dtype = F32
        out_spec = pl.BlockSpec((ts, W_GROUP), lambda b, i: (b * ns + ns - 1 - i, 0))
    else:
        ya = _row_specs(ts, ns, t, CB_YA, False)[0]
        in_specs += [ya, pl.BlockSpec((ts, W_GROUP), lambda b, i: (b * ns + i, 0))]
        args += [u, hb]
        out_dtype = BF16
        out_spec = pl.BlockSpec((ts, W_GROUP), lambda b, i: (b * ns + i, 0))
    return pl.pallas_call(
        functools.partial(_rglru_kernel, reverse),
        grid=(bsz, ns),
        in_specs=in_specs,
        out_specs=out_spec,
        out_shape=jax.ShapeDtypeStruct((t, W_GROUP), out_dtype),
        scratch_shapes=scratch,
        compiler_params=_params(("parallel", "arbitrary")),
        name="rglru_bwd" if reverse else "rglru_fwd",
    )(*args)


GDN_PREP_BLOCK = 512
GDN_SCAN_BLOCK = 512


def _gdn_prep_kernel(qm, qp, qn, km, kp, kn, vm, vp, vn, ab_ref, cwq, cwk, cwv, alog_ref, dtb_ref, eb_ref, eg_ref,
                     *outs):
    ts = qm.shape[0]
    nc = ts // CHUNK
    c = CHUNK
    i = pl.program_id(1)
    first = i == 0
    last = i == pl.num_programs(1) - 1
    out_refs = (outs[0:5], outs[5:10])

    ones_h = _head_ones()
    q = _silu(_conv4(qm[...], qp[...], qn[...], cwq, first, last))
    k = _silu(_conv4(km[...], kp[...], kn[...], cwk, first, last))
    v = _silu(_conv4(vm[...], vp[...], vn[...], cwv, first, last))
    q = q * lax.rsqrt(_dot_sel(q * q, ones_h, 2) + EPS) * (HEAD_DIM ** -0.5)
    k = k * lax.rsqrt(_dot_sel(k * k, ones_h, 2) + EPS)
    ab = ab_ref[...]
    sig_parts = _split_bf16(_sigmoid(ab), 2)
    g_parts = _split_bf16(-jnp.exp(alog_ref[...]) * _softplus(ab + dtb_ref[...]), 3)
    beta = [sum(_dot(p, eb_ref[d]) for p in sig_parts) for d in range(2)]
    gate = [sum(_dot(p, eg_ref[d]) for p in g_parts) for d in range(2)]

    ii = _iota((c, W_GROUP), 0)
    jj = _iota((c, W_GROUP), 1) % c
    r2 = _iota((c, c), 0)
    c2 = _iota((c, c), 1)
    incl = (jj <= ii, jj >= ii)
    strict = (jj < ii, jj > ii)
    tri = ((c2 <= r2).astype(BF16), (c2 >= r2).astype(BF16))
    ucat = ((ii > jj).astype(F32), (ii < jj).astype(F32))
    eye_cat = (ii == jj).astype(F32)
    blk = _iota((W_GROUP, W_GROUP), 0) // HEAD_DIM == _iota((W_GROUP, W_GROUP), 1) // HEAD_DIM

    sls = [slice(ci * c, (ci + 1) * c) for ci in range(nc)]
    kq = [_dot_nt(jnp.concatenate([k[sl], q[sl]], axis=0).astype(BF16), _stack4(k[sl].astype(BF16), blk))
          for sl in sls]
    units = [(d, ci) for ci in range(nc) for d in range(2)]
    gd = [_sel_dot(tri[d], jnp.concatenate([gate[d][sls[ci]], gate[d][sls[ci]] * ucat[d]], axis=1), 3)
          for d, ci in units]
    gcum = [x[:, :W_GROUP] for x in gd]
    decay = [jnp.exp(x[:, W_GROUP:]) for x in gd]
    lmat = [jnp.where(strict[d], beta[d][sls[ci]] * kq[ci][:c] * dec, 0.0) for (d, ci), dec in zip(units, decay)]
    attn = [jnp.where(incl[d], kq[ci][c:] * dec, 0.0) for (d, ci), dec in zip(units, decay)]
    pinv = [eye_cat - l for l in lmat]
    lb = [l.astype(BF16) for l in lmat]
    lpow = [_dot(x, _stack4(x, blk)) for x in lb]
    for lvl in range(5):
        z = [_stack4(x.astype(BF16), blk) for x in lpow]
        if lvl < 4:
            r = [_dot(jnp.concatenate([p, lp], axis=0).astype(BF16), zz) for p, lp, zz in zip(pinv, lpow, z)]
            pinv = [p + x[:c] for p, x in zip(pinv, r)]
            lpow = [x[c:] for x in r]
        else:
            pinv = [p + _dot(p.astype(BF16), zz) for p, zz in zip(pinv, z)]
    egc = [jnp.exp(x) for x in gcum]
    pb = [p.astype(BF16) for p in pinv]
    for n, (d, ci) in enumerate(units):
        sl = sls[ci]
        u_ref, wq_ref, at_ref, kg_ref, ds_ref = out_refs[d]
        bc = beta[d][sl]
        u_ref[sl, :] = _dot(pb[n], _stack4((v[sl] * bc).astype(BF16), blk))
        w_key = _dot(pb[n], _stack4((k[sl] * bc * egc[n]).astype(BF16), blk))
        g_last = gcum[n][0:1, :] if d else gcum[n][c - 1:c, :]
        wq_ref[2 * ci * c:(2 * ci + 1) * c, :] = w_key.astype(BF16)
        wq_ref[(2 * ci + 1) * c:(2 * ci + 2) * c, :] = (q[sl] * egc[n]).astype(BF16)
        at_ref[sl, :] = attn[n].astype(BF16)
        kg_ref[sl, :] = (k[sl] * jnp.exp(g_last - gcum[n])).astype(BF16)
        ds_ref[ci] = jnp.exp(g_last)


def _gdn_prep(u, bsz, seq, cwq, cwk, cwv, alog, dtb, eb, eg):
    ts = GDN_PREP_BLOCK
    ns = seq // ts
    t = bsz * seq
    nc = ts // CHUNK
    specs = []
    for cb in (CB_QB, CB_KB, CB_VB):
        specs += list(_row_specs(ts, ns, t, cb, False))
    specs.append(pl.BlockSpec((ts, LANES), lambda b, i: (b * ns + i, CB128_AB)))
    specs += [_const_spec((4, W_GROUP))] * 3
    specs += [_const_spec((1, LANES))] * 2 + [_const_spec((2, LANES, W_GROUP))] * 2

    def row(b, i):
        return (b * ns + i, 0)

    one_dir_specs = [pl.BlockSpec((ts, W_GROUP), row), pl.BlockSpec((2 * ts, W_GROUP), row),
                     pl.BlockSpec((ts, W_GROUP), row), pl.BlockSpec((ts, W_GROUP), row),
                     pl.BlockSpec((nc, 1, W_GROUP), lambda b, i: (b * ns + i, 0, 0))]
    one_dir_shapes = [jax.ShapeDtypeStruct((t, W_GROUP), F32), jax.ShapeDtypeStruct((2 * t, W_GROUP), BF16),
                      jax.ShapeDtypeStruct((t, W_GROUP), BF16), jax.ShapeDtypeStruct((t, W_GROUP), BF16),
                      jax.ShapeDtypeStruct((t // CHUNK, 1, W_GROUP), F32)]
    return pl.pallas_call(
        _gdn_prep_kernel,
        grid=(bsz, ns),
        in_specs=specs,
        out_specs=one_dir_specs * 2,
        out_shape=one_dir_shapes * 2,
        compiler_params=_params(("parallel", "parallel")),
        name="gdn_prep",
    )(*([u] * 10 + [cwq, cwk, cwv, alog, dtb, eb, eg]))


def _gdn_scan_kernel(group_sizes, periods, *refs):
    n_groups = len(group_sizes)
    ins = refs[:10 * n_groups]
    outs = refs[10 * n_groups:12 * n_groups]
    s_ref = refs[12 * n_groups]
    ts = ins[0].shape[1]
    nc = ts // CHUNK
    c = CHUNK

    chains = []
    for gi, nb in enumerate(group_sizes):
        first_chain = len(chains)
        for b in range(nb):
            for d in range(2):
                chains.append(tuple(ins[10 * gi + 5 * d:10 * gi + 5 * d + 5]) + (outs[2 * gi + d], d, b))

        @pl.when(pl.program_id(0) % periods[gi] == 0)
        def _(lo=first_chain, hi=len(chains)):
            s_ref[lo:hi] = jnp.zeros((hi - lo, W_GROUP, W_GROUP), F32)

    blk = _iota((W_GROUP, W_GROUP), 0) // HEAD_DIM == _iota((W_GROUP, W_GROUP), 1) // HEAD_DIM
    state = [s_ref[n] for n in range(len(chains))]
    for step in range(nc):
        pos = (step, nc - 1 - step)
        ws = [_dot(wq[b, 2 * pos[d] * c:(2 * pos[d] + 2) * c, :], s.astype(BF16))
              for (u_, wq, at, kg, ds, o, d, b), s in zip(chains, state)]
        vnb = [(u_[b, pos[d] * c:(pos[d] + 1) * c, :] - x[:c]).astype(BF16)
               for (u_, wq, at, kg, ds, o, d, b), x in zip(chains, ws)]
        for (u_, wq, at, kg, ds, o, d, b), x, vn in zip(chains, ws, vnb):
            sl = slice(pos[d] * c, (pos[d] + 1) * c)
            o[b, sl, :] = x[c:] + _dot(at[b, sl, :], _stack4(vn, blk))
        state = [s * ds[b, pos[d]] + jnp.where(blk, _dot_tn(kg[b, pos[d] * c:(pos[d] + 1) * c, :], vn), 0.0)
                 for (u_, wq, at, kg, ds, o, d, b), s, vn in zip(chains, state, vnb)]
    for n, s in enumerate(state):
        s_ref[n] = s


def _gdn_scan(trunks):
    ts = GDN_SCAN_BLOCK
    nc = ts // CHUNK
    n_steps = max(seq // ts for _, _, seq in trunks)
    specs, args, out_specs, out_shapes, group_sizes, periods = [], [], [], [], [], []
    for prep, bsz, seq in trunks:
        ns = seq // ts
        assert n_steps % ns == 0 and bsz % (n_steps // ns) == 0
        nb = bsz // (n_steps // ns)
        group_sizes.append(nb)
        periods.append(ns)
        for d in range(2):
            u_, wq, at, kg, ds = prep[5 * d:5 * d + 5]
            args += [u_.reshape(bsz, seq, W_GROUP), wq.reshape(bsz, 2 * seq, W_GROUP),
                     at.reshape(bsz, seq, W_GROUP), kg.reshape(bsz, seq, W_GROUP),
                     ds.reshape(bsz, seq // CHUNK, 1, W_GROUP)]

            def tb(i, ns=ns, d=d):
                return (ns - 1 - i % ns) if d else (i % ns)

            def blk3(rows, nb=nb, ns=ns, tb=tb):
                return pl.BlockSpec((nb, rows, W_GROUP), lambda i: (i // ns, tb(i), 0))

            specs += [blk3(ts), blk3(2 * ts), blk3(ts), blk3(ts),
                      pl.BlockSpec((nb, nc, 1, W_GROUP), lambda i, ns=ns, tb=tb: (i // ns, tb(i), 0, 0))]
            out_specs.append(blk3(ts))
            out_shapes.append(jax.ShapeDtypeStruct((bsz, seq, W_GROUP), F32))
    outs = pl.pallas_call(
        functools.partial(_gdn_scan_kernel, tuple(group_sizes), tuple(periods)),
        grid=(n_steps,),
        in_specs=specs,
        out_specs=out_specs,
        out_shape=out_shapes,
        scratch_shapes=[pltpu.VMEM((2 * sum(group_sizes), W_GROUP, W_GROUP), F32)],
        compiler_params=_params(("arbitrary",)),
        name="gdn_scan",
    )(*args)
    return [(outs[2 * n].reshape(bsz * seq, W_GROUP), outs[2 * n + 1].reshape(bsz * seq, W_GROUP))
            for n, (_, bsz, seq) in enumerate(trunks)]


RET_CHUNK = 128
RET_BLOCK = 1024


def _ret_kernel(reverse, *refs):
    if reverse:
        (q_ref, k_ref, v_ref, cos_ref, sin_ref, lgq_ref, lgv_ref, o_ref, s_ref) = refs
    else:
        (q_ref, k_ref, v_ref, cos_ref, sin_ref, lgq_ref, lgv_ref, lgfc_ref, lgbc_ref, g_ref, ob_ref, rn_ref,
         o_ref, s_ref, dm_ref) = refs
    ts = q_ref.shape[0]
    cr = RET_CHUNK
    nc = ts // cr
    nk = N_HEADS * cr

    @pl.when(pl.program_id(1) == 0)
    def _():
        s_ref[...] = jnp.zeros_like(s_ref)
        if not reverse:
            ii = _iota((cr, nk), 0)
            jj = _iota((cr, nk), 1) % cr
            dist = (ii - jj).astype(F32)
            lower, upper = jj <= ii, jj >= ii
            dm_ref[...] = (jnp.where(lower, jnp.exp(jnp.where(lower, dist * lgfc_ref[...], 0.0)), 0.0)
                           + jnp.where(upper, jnp.exp(jnp.where(upper, -dist * lgbc_ref[...], 0.0)), 0.0))

    cos = cos_ref[...]
    sin = sin_ref[...]

    def rope(t):
        t1, t2 = t[:, :LANES], t[:, LANES:]
        return jnp.concatenate([t1 * cos - t2 * sin, t1 * sin + t2 * cos], axis=-1)

    q = rope(q_ref[...])
    k = rope(k_ref[...]) * (HEAD_DIM ** -0.5)
    v = v_ref[...]
    lgq = lgq_ref[...]
    lgv = lgv_ref[...]
    pos = _iota((cr, W_GROUP), 0).astype(F32)
    rows_qk = (_iota((W_GROUP, W_GROUP), 0) % LANES) // 32
    blk_state = rows_qk == _iota((W_GROUP, W_GROUP), 1) // HEAD_DIM
    if reverse:
        q_fac = jnp.exp((cr - pos) * lgq)
        k_fac = jnp.exp(pos * lgq)
    else:
        q_fac = jnp.exp((pos + 1.0) * lgq)
        k_fac = jnp.exp((cr - 1.0 - pos) * lgq)
    s_fac = jnp.exp(cr * lgv)
    sls = [slice(ci * cr, (ci + 1) * cr) for ci in range(nc)]
    kv = [jnp.where(blk_state, _dot_tn((k[sl] * k_fac).astype(BF16), v[sl].astype(BF16)), 0.0) for sl in sls]
    s = s_ref[...]
    cross = [None] * nc
    for ci in (range(nc - 1, -1, -1) if reverse else range(nc)):
        cross[ci] = _dot((q[sls[ci]] * q_fac).astype(BF16), s.astype(BF16))
        s = s * s_fac + kv[ci]
    s_ref[...] = s
    if reverse:
        o_ref[...] = jnp.concatenate(cross, axis=0)
        return
    stack_head = _iota((nk, W_GROUP), 0) // cr
    k_mask = stack_head == (_iota((nk, W_GROUP), 1) % LANES) // 32
    v_mask = stack_head == _iota((nk, W_GROUP), 1) // HEAD_DIM
    qk = [_dot_nt(q[sl].astype(BF16), _stack4(k[sl].astype(BF16), k_mask)) for sl in sls]
    dm = dm_ref[...]
    intra = [_dot((x * dm).astype(BF16), _stack4(v[sl].astype(BF16), v_mask)) for x, sl in zip(qk, sls)]
    o = jnp.concatenate([a + b for a, b in zip(intra, cross)], axis=0) + ob_ref[...]
    ones_h = _head_ones()
    oc = o - _dot_sel(o, ones_h, 3) * (1.0 / HEAD_DIM)
    var = _dot_sel(oc * oc, ones_h, 2) * (1.0 / HEAD_DIM)
    o_ref[...] = (oc * lax.rsqrt(var + EPS) * rn_ref[...] * _silu(g_ref[...])).astype(o_ref.dtype)


def _ret(u, bsz, seq, reverse, cos, sin, lgq, lgv, lgfc=None, lgbc=None, rn=None, ob=None):
    cr = RET_CHUNK
    ts = RET_BLOCK
    ns = seq // ts
    t = bsz * seq
    tbf = (lambda i: ns - 1 - i) if reverse else (lambda i: i)

    def col(cb):
        return pl.BlockSpec((ts, W_GROUP), lambda b, i: (b * ns + tbf(i), cb))

    tab = pl.BlockSpec((ts, LANES), lambda b, i: (tbf(i), 0))
    specs = [col(CB_QC), col(CB_KC), col(CB_VC), tab, tab, _const_spec((1, W_GROUP)), _const_spec((1, W_GROUP))]
    args = [u, u, u, cos, sin, lgq, lgv]
    scratch = [pltpu.VMEM((W_GROUP, W_GROUP), F32)]
    if reverse:
        out_dtype = F32
    else:
        specs += [_const_spec((1, N_HEADS * cr)), _const_spec((1, N_HEADS * cr)), col(CB_GC),
                  pl.BlockSpec((ts, W_GROUP), lambda b, i: (b * ns + i, 0)), _const_spec((1, W_GROUP))]
        args += [lgfc, lgbc, u, ob, rn]
        scratch.append(pltpu.VMEM((cr, N_HEADS * cr), F32))
        out_dtype = BF16
    return pl.pallas_call(
        functools.partial(_ret_kernel, reverse),
        grid=(bsz, ns),
        in_specs=specs,
        out_specs=pl.BlockSpec((ts, W_GROUP), lambda b, i: (b * ns + tbf(i), 0)),
        out_shape=jax.ShapeDtypeStruct((t, W_GROUP), out_dtype),
        scratch_shapes=scratch,
        compiler_params=_params(("parallel", "arbitrary")),
        name="ret_bwd" if reverse else "ret_fwd",
    )(*args)


NA_GROUP = 16
NA_WIN = NA_KH * GRID_W
NA_ROWS_PER_PASS = 4


def _na_kernel(rows, q_ref, kp_ref, kc_ref, kn_ref, vp_ref, vc_ref, vn_ref, tab_ref, o_ref, kw, vw):
    g = pl.program_id(1)
    blk = NA_GROUP * GRID_W
    kw[0:blk, :] = kp_ref[...]
    kw[blk:2 * blk, :] = kc_ref[...]
    kw[2 * blk:3 * blk, :] = kn_ref[...]
    vw[0:blk, :] = vp_ref[...]
    vw[blk:2 * blk, :] = vc_ref[...]
    vw[2 * blk:3 * blk, :] = vn_ref[...]
    row_head = _iota((N_HEADS * GRID_W, W_GROUP), 0) // GRID_W
    lane_head = _iota((N_HEADS * GRID_W, W_GROUP), 1) // HEAD_DIM
    blk = row_head == lane_head
    out_head = _iota((GRID_W, W_GROUP), 1) // HEAD_DIM
    for j0 in range(0, NA_GROUP, NA_ROWS_PER_PASS):
        js = list(range(j0, j0 + NA_ROWS_PER_PASS))
        vwins, sc = [], []
        for j in js:
            r = g * NA_GROUP + j
            r0 = jnp.clip(r - NA_KH // 2, 0, rows - NA_KH)
            off = pl.multiple_of((r0 - g * NA_GROUP + NA_GROUP) * GRID_W, GRID_W)
            qst = _stack4(q_ref[j * GRID_W:(j + 1) * GRID_W, :], blk)
            vwins.append(vw[pl.ds(off, NA_WIN), :])
            sc.append(_dot_nt(qst, kw[pl.ds(off, NA_WIN), :]) + tab_ref[r - r0])
        mx = [jnp.max(s, axis=-1, keepdims=True) for s in sc]
        pr = [jnp.exp(s - m) for s, m in zip(sc, mx)]
        den = [jnp.sum(p, axis=-1, keepdims=True) for p in pr]
        pv = [_dot(p.astype(BF16), vw_) / l for p, vw_, l in zip(pr, vwins, den)]
        for j, o in zip(js, pv):
            res = o[0:GRID_W]
            for h in range(1, N_HEADS):
                res = jnp.where(out_head == h, o[h * GRID_W:(h + 1) * GRID_W], res)
            o_ref[j * GRID_W:(j + 1) * GRID_W, :] = res.astype(o_ref.dtype)


def _na(qn, kn, vn, table, bsz, seq):
    rows = seq // GRID_W
    ng = rows // NA_GROUP
    blk = NA_GROUP * GRID_W
    t = bsz * seq
    cur = pl.BlockSpec((blk, W_GROUP), lambda b, g: (b * ng + g, 0))
    prev = pl.BlockSpec((blk, W_GROUP), lambda b, g: (b * ng + jnp.maximum(g - 1, 0), 0))
    nxt = pl.BlockSpec((blk, W_GROUP), lambda b, g: (b * ng + jnp.minimum(g + 1, ng - 1), 0))
    return pl.pallas_call(
        functools.partial(_na_kernel, rows),
        grid=(bsz, ng),
        in_specs=[cur, prev, cur, nxt, prev, cur, nxt, _const_spec((NA_KH, N_HEADS * GRID_W, NA_WIN))],
        out_specs=cur,
        out_shape=jax.ShapeDtypeStruct((t, W_GROUP), BF16),
        scratch_shapes=[pltpu.VMEM((3 * blk, W_GROUP), BF16), pltpu.VMEM((3 * blk, W_GROUP), BF16)],
        compiler_params=_params(("parallel", "parallel")),
        name="na",
    )(qn, kn, kn, kn, vn, vn, vn, table)


def _na_table(rpb):
    didx = np.arange(NA_KH)[:, None]
    kk = np.arange(NA_KH)[None, :]
    dr = (kk + (NA_KH - 1) - didx).reshape(-1)
    cc = np.arange(GRID_W)[:, None]
    kc = np.arange(GRID_W)[None, :]
    c0 = np.clip(cc - NA_KW // 2, 0, GRID_W - NA_KW)
    valid = (kc >= c0) & (kc < c0 + NA_KW)
    dc = np.clip(kc - cc + (NA_KW - 1), 0, 2 * NA_KW - 2).reshape(-1)
    row_sel = np.zeros((dr.size, 2 * NA_KH - 1), np.float32)
    row_sel[np.arange(dr.size), dr] = 1.0
    col_sel = np.zeros((2 * NA_KW - 1, dc.size), np.float32)
    col_sel[dc, np.arange(dc.size)] = 1.0
    bias = jnp.einsum('ar,hrs,sb->hab', row_sel, rpb, col_sel, precision=lax.Precision.HIGHEST)
    bias = bias.reshape(N_HEADS, NA_KH, NA_KH, GRID_W, GRID_W)
    tab = jnp.where(valid[None, None, None], bias, NEG_BIG)
    return jnp.transpose(tab, (1, 0, 3, 2, 4)).reshape(NA_KH, N_HEADS * GRID_W, NA_WIN)


FF_CHUNK = 256
POST_HALO = 16


def _mix_residual(x, a, bf, bb, z, gn, c, d, w_ref):
    ob = bf + bb
    ms = _dot_sel(ob * ob, _head_ones(), 2) * (1.0 / HEAD_DIM)
    out_b = (ob * lax.rsqrt(ms + EPS) * gn * _silu(z)).astype(BF16)
    acc = x
    for gi, m in enumerate((a, out_b, c, d)):
        acc = acc + _dot(m, w_ref[gi * W_GROUP:(gi + 1) * W_GROUP, :])
    return acc


def _post_kernel(*refs):
    streams = [refs[3 * n:3 * n + 3] for n in range(7)]
    (p_ref, gn_ref, wo_ref, n2_ref, wg_ref, wu_ref, cw_ref, cb_ref, wd_ref, n3_ref, pg_ref, pp_ref,
     o_ref, act_s) = refs[21:]
    tm = streams[0][0].shape[0]
    i = pl.program_id(1)
    first = i == 0
    last = i == pl.num_programs(1) - 1
    gn = gn_ref[...]
    mains = [s[0][...] for s in streams]
    halos = [jnp.concatenate([s[1][...], s[2][...]], axis=0) for s in streams]
    x = _mix_residual(mains[0], *mains[1:5], gn, *mains[5:], wo_ref)
    xh = _mix_residual(halos[0], *halos[1:5], gn, *halos[5:], wo_ref)
    n2 = n2_ref[...]
    h2f = _rms(x, n2)
    h2h = _rms(xh, n2)
    h_before = jnp.where(first, 0.0, h2h[POST_HALO - SUBLANES:POST_HALO])
    h_after = jnp.where(last, 0.0, h2h[POST_HALO:POST_HALO + SUBLANES])
    h2 = h2f.astype(BF16)
    h2e = jnp.concatenate([h_before, h2f, h_after], axis=0).astype(BF16)
    for f in range(D_FF // FF_CHUNK):
        fs = slice(f * FF_CHUNK, (f + 1) * FF_CHUNK)
        ge = _dot(h2e, wg_ref[:, fs])
        ne = tm + 2 * SUBLANES
        g_prev = pltpu.roll(ge, 1, 0)[SUBLANES:SUBLANES + tm]
        g_next = pltpu.roll(ge, ne - 1, 0)[SUBLANES:SUBLANES + tm]
        gate = (cw_ref[0:1, fs] * g_prev + cw_ref[1:2, fs] * ge[SUBLANES:SUBLANES + tm]
                + cw_ref[2:3, fs] * g_next + cb_ref[:, fs])
        up = _dot(h2, wu_ref[:, fs])
        act_s[:, fs] = (_gelu_tanh(gate) * up).astype(BF16)
    acc = x + _dot(act_s[...], wd_ref[...])
    h3 = _rms(acc, n3_ref[...]).astype(BF16)
    gate3 = _sigmoid(_dot(h3, pg_ref[...]))
    o_ref[...] = acc + gate3 * _dot(p_ref[...].astype(BF16), pp_ref[...])


def _post(x, u, oa, obf, obb, gn, oc, od, w_out, p, p_layer, bsz, seq, n2, wg, wu, cw, cb, wd, n3, pg, pp):
    tm = 512

    def layer_spec(rows, cols):
        return pl.BlockSpec((None, rows, cols), lambda *_: (p_layer, 0, 0), pipeline_mode=pl.Buffered(1))

    ns = seq // tm
    t = bsz * seq
    specs = list(_row_specs(tm, ns, t, 0, False, width=D_MODEL, halo=POST_HALO))
    arrays = [x] * 3
    for arr, cb_ in ((oa, 0), (obf, 0), (obb, 0), (u, CB_ZB), (oc, 0), (od, 0)):
        specs += list(_row_specs(tm, ns, t, cb_, False, halo=POST_HALO))
        arrays += [arr] * 3
    specs += [pl.BlockSpec((tm, PLE_DIM), lambda b, i: (p_layer * (t // tm) + b * ns + i, 0)),
              _const_spec((1, W_GROUP)), layer_spec(D_MODEL, D_MODEL),
              _const_spec((1, D_MODEL)), layer_spec(D_MODEL, D_FF), layer_spec(D_MODEL, D_FF),
              _const_spec((3, D_FF)), _const_spec((1, D_FF)), layer_spec(D_FF, D_MODEL),
              _const_spec((1, D_MODEL)), layer_spec(D_MODEL, D_MODEL), layer_spec(PLE_DIM, D_MODEL)]
    return pl.pallas_call(
        _post_kernel,
        grid=(bsz, ns),
        in_specs=specs,
        out_specs=pl.BlockSpec((tm, D_MODEL), lambda b, i: (b * ns + i, 0)),
        out_shape=jax.ShapeDtypeStruct((t, D_MODEL), F32),
        compiler_params=_params(("parallel", "parallel")),
        scratch_shapes=[pltpu.VMEM((tm, D_FF), BF16)],
        name="post",
    )(*arrays, p, gn, w_out, n2, wg, wu, cw, cb, wd, n3, pg, pp)


def _block_diag(w):
    h, n, _ = w.shape
    eye = jnp.eye(h, dtype=w.dtype)
    return (eye[:, None, :, None] * w[:, :, None, :]).reshape(h * n, h * n)


def _row(v):
    return v.reshape(1, -1).astype(F32)


def _lane_pad(v8, offset):
    return jnp.zeros((1, LANES), F32).at[0, offset:offset + 2 * N_HEADS].set(v8.reshape(-1).astype(F32))


def _expander(offset):
    e = np.zeros((LANES, W_GROUP), np.float32)
    for h in range(N_HEADS):
        e[offset + h, h * HEAD_DIM:(h + 1) * HEAD_DIM] = 1.0
    return jnp.asarray(e, BF16)


def _rope_tables(max_seq):
    half = HEAD_DIM // 2
    inv_freq = ROPE_BASE ** (-jnp.arange(half, dtype=F32) / half)
    ang = jnp.arange(max_seq, dtype=F32)[:, None] * inv_freq[None, :]
    return jnp.tile(jnp.cos(ang), (1, N_HEADS)), jnp.tile(jnp.sin(ang), (1, N_HEADS))


def _layer_front(x, lw, bsz, seq, rope):
    w_in = _reorder_in_cols(lw['w_in'].astype(BF16))
    u, qn, kn, vn = _inproj(x, _row(lw['norm1']), w_in,
                            _row(jnp.tile(lw['na_qnorm'], N_HEADS)), _row(jnp.tile(lw['na_knorm'], N_HEADS)))

    cw_a, cb_a = lw['conv_a_w'].astype(F32), _row(lw['conv_a_b'])
    h_dir = None
    for d in (1, 0):
        h_dir = _rglru(u, bsz, seq, d == 1, cw_a, cb_a,
                       _block_diag(lw['lru_wr'][d]).astype(BF16), _row(lw['lru_br'][d]),
                       _block_diag(lw['lru_wi'][d]).astype(BF16), _row(lw['lru_bi'][d]),
                       _row(lw['lru_lambda'][d]), hb=h_dir)
    out_a = h_dir

    gconv = lw['gdn_conv'].astype(F32)
    cwq, cwk, cwv = gconv[:, :W_GROUP], gconv[:, W_GROUP:2 * W_GROUP], gconv[:, 2 * W_GROUP:]
    alog = _lane_pad(lw['gdn_a_log'], 2 * N_HEADS)
    dtb = _lane_pad(lw['gdn_dt_bias'], 2 * N_HEADS)
    eb = jnp.stack([_expander(d * N_HEADS) for d in range(2)])
    eg = jnp.stack([_expander(2 * N_HEADS + d * N_HEADS) for d in range(2)])
    prep = _gdn_prep(u, bsz, seq, cwq, cwk, cwv, alog, dtb, eb, eg)

    half = HEAD_DIM // 2
    cos, sin = rope
    log_gamma = jnp.log1p(-jnp.exp2(-lw['ret_decay'].astype(F32)))
    rn = _row(jnp.tile(lw['ret_norm'], N_HEADS))
    o_dir = None
    for d in (1, 0):
        lg = log_gamma[d]
        lgq = _row(jnp.tile(jnp.repeat(lg, half), 2))
        lgv = _row(jnp.repeat(lg, HEAD_DIM))
        if d == 1:
            o_dir = _ret(u, bsz, seq, True, cos, sin, lgq, lgv)
        else:
            o_dir = _ret(u, bsz, seq, False, cos, sin, lgq, lgv,
                         lgfc=_row(jnp.repeat(log_gamma[0], RET_CHUNK)), lgbc=_row(jnp.repeat(log_gamma[1], RET_CHUNK)),
                         rn=rn, ob=o_dir)
    out_c = o_dir

    out_d = _na(qn, kn, vn, _na_table(lw['na_rpb'].astype(F32)), bsz, seq)
    return dict(u=u, out_a=out_a, out_c=out_c, out_d=out_d, prep=prep)


_STACKED = ('w_out', 'ffn_wg', 'ffn_wu', 'ffn_wd', 'ple_gate', 'ple_proj')


def _layer_back(x, p, p_layer, front, o_b, lw, stacks, bsz, seq):
    return _post(x, front['u'], front['out_a'], o_b[0], o_b[1], _row(jnp.tile(lw['gdn_norm'], N_HEADS)),
                 front['out_c'], front['out_d'], stacks['w_out'], p, p_layer, bsz, seq,
                 _row(lw['norm2']), stacks['ffn_wg'], stacks['ffn_wu'],
                 lw['ffn_conv_w'].astype(F32), _row(lw['ffn_conv_b']), stacks['ffn_wd'],
                 _row(lw['norm3']), stacks['ple_gate'], stacks['ple_proj'])


def _layer_multi(xs, ps, p_layer, lw, stacks, shapes):
    rope = _rope_tables(max(seq for _, seq in shapes))
    fronts = [_layer_front(x, lw, bsz, seq, rope) for x, (bsz, seq) in zip(xs, shapes)]
    scans = _gdn_scan([(f['prep'], bsz, seq) for f, (bsz, seq) in zip(fronts, shapes)])
    return [_layer_back(x, p, p_layer, f, o_b, lw, stacks, bsz, seq)
            for x, p, f, o_b, (bsz, seq) in zip(xs, ps, fronts, scans, shapes)]


def _layer(x, p, lw, bsz, seq):
    stacks = {n: lw[n][None].astype(BF16) for n in _STACKED}
    return _layer_multi([x], [p], 0, lw, stacks, [(bsz, seq)])[0]


_WEIGHT_NAMES = ('norm1', 'norm2', 'norm3', 'w_in', 'w_out', 'conv_a_w', 'conv_a_b', 'lru_wr', 'lru_br',
                 'lru_wi', 'lru_bi', 'lru_lambda', 'gdn_conv', 'gdn_a_log', 'gdn_dt_bias', 'gdn_norm',
                 'ret_decay', 'ret_norm', 'na_qnorm', 'na_knorm', 'na_rpb', 'ffn_wg', 'ffn_wu',
                 'ffn_conv_w', 'ffn_conv_b', 'ffn_wd', 'ple_proj', 'ple_gate')


def _trunks(xs, ps, weights):
    shapes = [x.shape[:2] for x in xs]
    flat = [x.reshape(bsz * seq, D_MODEL) for x, (bsz, seq) in zip(xs, shapes)]
    p_rows = [p.reshape(DEPTH * bsz * seq, PLE_DIM) for p, (bsz, seq) in zip(ps, shapes)]
    stacks = {n: weights[n].astype(BF16) for n in _STACKED}
    for i in range(DEPTH):
        lw = {n: w[i] for n, w in weights.items() if n not in _STACKED}
        flat = _layer_multi(flat, p_rows, i, lw, stacks, shapes)
    return tuple(x.reshape(bsz, seq, D_MODEL) for x, (bsz, seq) in zip(flat, shapes))


def kernel(x_prompt, x_sample, p_prompt, p_sample, norm1, norm2, norm3, w_in, w_out, conv_a_w, conv_a_b, lru_wr, lru_br, lru_wi, lru_bi, lru_lambda, gdn_conv, gdn_a_log, gdn_dt_bias, gdn_norm, ret_decay, ret_norm, na_qnorm, na_knorm, na_rpb, ffn_wg, ffn_wu, ffn_conv_w, ffn_conv_b, ffn_wd, ple_proj, ple_gate):
    weights = dict(zip(_WEIGHT_NAMES, (norm1, norm2, norm3, w_in, w_out, conv_a_w, conv_a_b, lru_wr, lru_br,
                                       lru_wi, lru_bi, lru_lambda, gdn_conv, gdn_a_log, gdn_dt_bias, gdn_norm,
                                       ret_decay, ret_norm, na_qnorm, na_knorm, na_rpb, ffn_wg, ffn_wu,
                                       ffn_conv_w, ffn_conv_b, ffn_wd, ple_proj, ple_gate)))
    return _trunks((x_prompt, x_sample), (p_prompt, p_sample), weights)
```

```python
import functools

import jax
import jax.numpy as jnp
import numpy as np
from jax import lax
from jax.experimental import pallas as pl
from jax.experimental.pallas import tpu as pltpu

F32 = jnp.float32
BF16 = jnp.bfloat16

D_MODEL = 1024
DEPTH = 2
PLE_DIM = 256
GRID_W = 64
W_GROUP = 256
HEAD_DIM = 64
N_HEADS = 4
LRU_C = 8.0
CHUNK = 64
ROPE_BASE = 10000.0
NA_KH = 8
NA_KW = 16
D_FF = 2816
EPS = 1e-6
NEG_BIG = -1e30

SUBLANES = 8
LANES = 128
VMEM_LIMIT = 56 * 1024 * 1024

CB_XA, CB_YA, CB_QB, CB_KB, CB_VB, CB_ZB, CB_QC, CB_KC, CB_VC, CB_GC = range(10)
CB128_AB = 20
U_COLS = 10 * W_GROUP + LANES
IN_COLS_PAD = U_COLS + 3 * W_GROUP
IN_COLS = 13 * W_GROUP + 4 * N_HEADS


def _reorder_in_cols(w):
    rows = w.shape[0]
    half = HEAD_DIM // 2

    def halves_first(cols):
        return cols.reshape(rows, N_HEADS, 2, half).transpose(0, 2, 1, 3).reshape(rows, W_GROUP)

    ab0 = 6 * W_GROUP
    qc0 = ab0 + 4 * N_HEADS
    qd0 = qc0 + 4 * W_GROUP
    parts = [w[:, :ab0], halves_first(w[:, qc0:qc0 + W_GROUP]), halves_first(w[:, qc0 + W_GROUP:qc0 + 2 * W_GROUP]),
             w[:, qc0 + 2 * W_GROUP:qd0], w[:, ab0:qc0], jnp.zeros((rows, LANES - 4 * N_HEADS), w.dtype), w[:, qd0:]]
    out = jnp.concatenate(parts, axis=1)
    assert out.shape[1] == IN_COLS_PAD
    return out


def _sigmoid(x):
    return 1.0 / (1.0 + jnp.exp(-x))


def _silu(x):
    return x * _sigmoid(x)


def _softplus(x):
    return jnp.maximum(x, 0.0) + jnp.log1p(jnp.exp(-jnp.abs(x)))


def _gelu_tanh(x):
    return x * (0.5 * (1.0 + jnp.tanh(0.7978845608028654 * (x + 0.044715 * (x * x * x)))))


def _rms(x, gain):
    ms = jnp.mean(x * x, axis=-1, keepdims=True)
    return x * lax.rsqrt(ms + EPS) * gain


def _dot(a, b):
    return jnp.dot(a, b, preferred_element_type=F32)


def _split_bf16(x, terms):
    parts = []
    for _ in range(terms):
        p = x.astype(BF16)
        parts.append(p)
        x = x - p.astype(F32)
    return parts


def _dot_sel(x, sel, terms):
    return sum(_dot(p, sel) for p in _split_bf16(x, terms))


def _sel_dot(sel, x, terms):
    return sum(_dot(sel, p) for p in _split_bf16(x, terms))


def _dot_nt(a, b):
    return lax.dot_general(a, b, (((1,), (1,)), ((), ())), preferred_element_type=F32)


def _dot_tn(a, b):
    return lax.dot_general(a, b, (((0,), (0,)), ((), ())), preferred_element_type=F32)


def _iota(shape, dim):
    return lax.broadcasted_iota(jnp.int32, shape, dim)


def _head_ones():
    return (_iota((W_GROUP, W_GROUP), 0) // HEAD_DIM == _iota((W_GROUP, W_GROUP), 1) // HEAD_DIM).astype(BF16)


def _stack4(x, mask):
    return jnp.where(mask, jnp.concatenate([x] * N_HEADS, axis=0), jnp.zeros((), x.dtype))


def _conv4(xm, xp, xn, w_ref, first, last):
    ts = xm.shape[0]
    xp = jnp.where(first, 0.0, xp)
    xn = jnp.where(last, 0.0, xn)
    xe = jnp.concatenate([xp, xm, xn], axis=0)
    n = ts + 2 * SUBLANES

    def tap(offset):
        rolled = xe if offset == 0 else pltpu.roll(xe, (-offset) % n, 0)
        return rolled[SUBLANES:SUBLANES + ts]

    out = w_ref[0:1, :] * tap(-2)
    out = out + w_ref[1:2, :] * tap(-1)
    out = out + w_ref[2:3, :] * tap(0)
    out = out + w_ref[3:4, :] * tap(1)
    return out


def _time_block(reverse):
    i = pl.program_id(1)
    n = pl.num_programs(1)
    tb = (n - 1 - i) if reverse else i
    return tb, tb == 0, tb == n - 1


def _row_specs(ts, ns, total_rows, col_block, reverse, width=W_GROUP, halo=SUBLANES):
    per_block = ts // halo
    last = total_rows // halo - 1

    def tb(i):
        return (ns - 1 - i) if reverse else i

    main = pl.BlockSpec((ts, width), lambda b, i: (b * ns + tb(i), col_block))
    prev = pl.BlockSpec((halo, width), lambda b, i: (jnp.maximum((b * ns + tb(i)) * per_block - 1, 0), col_block))
    nxt = pl.BlockSpec((halo, width), lambda b, i: (jnp.minimum((b * ns + tb(i) + 1) * per_block, last), col_block))
    return main, prev, nxt


def _const_spec(shape):
    nd = len(shape)
    return pl.BlockSpec(shape, lambda *_: (0,) * nd, pipeline_mode=pl.Buffered(1))


def _params(sem):
    return pltpu.CompilerParams(dimension_semantics=sem, vmem_limit_bytes=VMEM_LIMIT)


def _inproj_kernel(x_ref, g_ref, w_ref, gq_ref, gk_ref, u_ref, qd_ref, kd_ref, vd_ref):
    h = _rms(x_ref[...], g_ref[...]).astype(BF16)
    u_ref[...] = _dot(h, w_ref[:, :U_COLS])
    qkv = _dot(h, w_ref[:, U_COLS:])
    q, k = qkv[:, :W_GROUP], qkv[:, W_GROUP:2 * W_GROUP]
    ones_h = _head_ones()
    inv = 1.0 / HEAD_DIM
    qd_ref[...] = (q * lax.rsqrt(_dot_sel(q * q, ones_h, 2) * inv + EPS) * gq_ref[...] * (HEAD_DIM ** -0.5)).astype(BF16)
    kd_ref[...] = (k * lax.rsqrt(_dot_sel(k * k, ones_h, 2) * inv + EPS) * gk_ref[...]).astype(BF16)
    vd_ref[...] = qkv[:, 2 * W_GROUP:].astype(BF16)


def _inproj(x, gain, w, gq, gk):
    t = x.shape[0]
    tm = 1024
    grp = pl.BlockSpec((tm, W_GROUP), lambda i: (i, 0))
    grp_shape = jax.ShapeDtypeStruct((t, W_GROUP), BF16)
    return pl.pallas_call(
        _inproj_kernel,
        grid=(t // tm,),
        in_specs=[pl.BlockSpec((tm, D_MODEL), lambda i: (i, 0)),
                  _const_spec((1, D_MODEL)),
                  _const_spec((D_MODEL, IN_COLS_PAD)),
                  _const_spec((1, W_GROUP)), _const_spec((1, W_GROUP))],
        out_specs=[pl.BlockSpec((tm, U_COLS), lambda i: (i, 0)), grp, grp, grp],
        out_shape=[jax.ShapeDtypeStruct((t, U_COLS), F32), grp_shape, grp_shape, grp_shape],
        compiler_params=_params(("parallel",)),
        name="inproj",
    )(x, gain, w, gq, gk)


def _rglru_kernel(reverse, *refs):
    if reverse:
        (xm_ref, xp_ref, xn_ref, cw_ref, cb_ref, wr_ref, br_ref, wi_ref, bi_ref, lam_ref,
         o_ref, a_s, b_s, p_s, ac_s, c_s, carry, h_s) = refs
    else:
        (xm_ref, xp_ref, xn_ref, cw_ref, cb_ref, wr_ref, br_ref, wi_ref, bi_ref, lam_ref, ya_ref, hb_ref,
         o_ref, a_s, b_s, p_s, ac_s, c_s, carry, h_s) = refs
    ts = xm_ref.shape[0]
    ng = ts // SUBLANES
    tb, first, last = _time_block(reverse)

    @pl.when(pl.program_id(1) == 0)
    def _():
        carry[...] = jnp.zeros_like(carry)

    x = _conv4(xm_ref[...], xp_ref[...], xn_ref[...], cw_ref, first, last) + cb_ref[...]
    xb = x.astype(BF16)
    r = _sigmoid(_dot(xb, wr_ref[...]) + br_ref[...])
    ig = _sigmoid(_dot(xb, wi_ref[...]) + bi_ref[...])
    log_a = (-LRU_C) * r * _softplus(-lam_ref[...])
    a = jnp.exp(log_a)
    th = jnp.tanh(-log_a)
    bx = jnp.sqrt(2.0 * th / (1.0 + th)) * (ig * x)
    nl = W_GROUP // LANES
    for hf in range(nl):
        a_s[hf] = a[:, hf * LANES:(hf + 1) * LANES]
        b_s[hf] = bx[:, hf * LANES:(hf + 1) * LANES]

    def slab(ref, k):
        return jnp.concatenate([ref[hf, pl.ds(k, ng, stride=SUBLANES), :] for hf in range(nl)], axis=-1)

    order = list(range(SUBLANES))[::-1] if reverse else list(range(SUBLANES))
    p = None
    ac = None
    for k in order:
        a_k = slab(a_s, k)
        b_k = slab(b_s, k)
        if p is None:
            p, ac = b_k, a_k
        else:
            p = a_k * p + b_k
            ac = a_k * ac
        p_s[k] = p
        ac_s[k] = ac

    k_last = order[-1]

    def body(g, c):
        gi = (ng - 1 - g) if reverse else g
        c_s[pl.ds(gi, 1), :] = c
        return ac_s[k_last, pl.ds(gi, 1), :] * c + p_s[k_last, pl.ds(gi, 1), :]

    carry[...] = lax.fori_loop(0, ng, body, carry[...], unroll=True)
    c_in = c_s[...]
    for k in order:
        h_k = p_s[k] + ac_s[k] * c_in
        for hf in range(nl):
            h_s[hf, pl.ds(k, ng, stride=SUBLANES), :] = h_k[:, hf * LANES:(hf + 1) * LANES]
    h = jnp.concatenate([h_s[hf] for hf in range(nl)], axis=-1)
    if reverse:
        o_ref[...] = h
    else:
        o_ref[...] = ((h + hb_ref[...]) * _gelu_tanh(ya_ref[...])).astype(o_ref.dtype)


def _rglru(u, bsz, seq, reverse, cw, cb, wr, br, wi, bi, lam, hb=None):
    ts = 2048
    ns = seq // ts
    t = bsz * seq
    ng = ts // SUBLANES
    xm, xp, xn = _row_specs(ts, ns, t, CB_XA, reverse)
    in_specs = [xm, xp, xn, _const_spec((4, W_GROUP)), _const_spec((1, W_GROUP)),
                _const_spec((W_GROUP, W_GROUP)), _const_spec((1, W_GROUP)),
                _const_spec((W_GROUP, W_GROUP)), _const_spec((1, W_GROUP)), _const_spec((1, W_GROUP))]
    args = [u, u, u, cw, cb, wr, br, wi, bi, lam]
    halves = (W_GROUP // LANES, ts, LANES)
    scratch = [pltpu.VMEM(halves, F32), pltpu.VMEM(halves, F32),
               pltpu.VMEM((SUBLANES, ng, W_GROUP), F32), pltpu.VMEM((SUBLANES, ng, W_GROUP), F32),
               pltpu.VMEM((ng, W_GROUP), F32), pltpu.VMEM((1, W_GROUP), F32), pltpu.VMEM(halves, F32)]
    if reverse:
        out_dtype = F32
        out_spec = pl.BlockSpec((ts, W_GROUP), lambda b, i: (b * ns + ns - 1 - i, 0))
    else:
        ya = _row_specs(ts, ns, t, CB_YA, False)[0]
        in_specs += [ya, pl.BlockSpec((ts, W_GROUP), lambda b, i: (b * ns + i, 0))]
        args += [u, hb]
        out_dtype = BF16
        out_spec = pl.BlockSpec((ts, W_GROUP), lambda b, i: (b * ns + i, 0))
    return pl.pallas_call(
        functools.partial(_rglru_kernel, reverse),
        grid=(bsz, ns),
        in_specs=in_specs,
        out_specs=out_spec,
        out_shape=jax.ShapeDtypeStruct((t, W_GROUP), out_dtype),
        scratch_shapes=scratch,
        compiler_params=_params(("parallel", "arbitrary")),
        name="rglru_bwd" if reverse else "rglru_fwd",
    )(*args)


GDN_PREP_BLOCK = 1024
GDN_SCAN_BLOCK = 512


def _gdn_prep_kernel(qm, qp, qn, km, kp, kn, vm, vp, vn, ab_ref, cwq, cwk, cwv, alog_ref, dtb_ref, eb_ref, eg_ref,
                     *outs):
    ts = qm.shape[0]
    nc = ts // CHUNK
    c = CHUNK
    i = pl.program_id(1)
    first = i == 0
    last = i == pl.num_programs(1) - 1
    out_refs = (outs[0:5], outs[5:10])

    ones_h = _head_ones()
    q = _silu(_conv4(qm[...], qp[...], qn[...], cwq, first, last))
    k = _silu(_conv4(km[...], kp[...], kn[...], cwk, first, last))
    v = _silu(_conv4(vm[...], vp[...], vn[...], cwv, first, last))
    q = q * lax.rsqrt(_dot_sel(q * q, ones_h, 2) + EPS) * (HEAD_DIM ** -0.5)
    k = k * lax.rsqrt(_dot_sel(k * k, ones_h, 2) + EPS)
    ab = ab_ref[...]
    sig_parts = _split_bf16(_sigmoid(ab), 2)
    g_parts = _split_bf16(-jnp.exp(alog_ref[...]) * _softplus(ab + dtb_ref[...]), 3)
    beta = [sum(_dot(p, eb_ref[d]) for p in sig_parts) for d in range(2)]
    gate = [sum(_dot(p, eg_ref[d]) for p in g_parts) for d in range(2)]

    ii = _iota((c, W_GROUP), 0)
    jj = _iota((c, W_GROUP), 1) % c
    r2 = _iota((c, c), 0)
    c2 = _iota((c, c), 1)
    incl = (jj <= ii, jj >= ii)
    strict = (jj < ii, jj > ii)
    tri = ((c2 <= r2).astype(BF16), (c2 >= r2).astype(BF16))
    ucat = ((ii > jj).astype(F32), (ii < jj).astype(F32))
    eye_cat = (ii == jj).astype(F32)
    blk = _iota((W_GROUP, W_GROUP), 0) // HEAD_DIM == _iota((W_GROUP, W_GROUP), 1) // HEAD_DIM

    sls = [slice(ci * c, (ci + 1) * c) for ci in range(nc)]
    kq = [_dot_nt(jnp.concatenate([k[sl], q[sl]], axis=0).astype(BF16), _stack4(k[sl].astype(BF16), blk))
          for sl in sls]
    units = [(d, ci) for ci in range(nc) for d in range(2)]
    gd = [_sel_dot(tri[d], jnp.concatenate([gate[d][sls[ci]], gate[d][sls[ci]] * ucat[d]], axis=1), 3)
          for d, ci in units]
    gcum = [x[:, :W_GROUP] for x in gd]
    decay = [jnp.exp(x[:, W_GROUP:]) for x in gd]
    lmat = [jnp.where(strict[d], beta[d][sls[ci]] * kq[ci][:c] * dec, 0.0) for (d, ci), dec in zip(units, decay)]
    attn = [jnp.where(incl[d], kq[ci][c:] * dec, 0.0) for (d, ci), dec in zip(units, decay)]
    pinv = [eye_cat - l for l in lmat]
    lb = [l.astype(BF16) for l in lmat]
    lpow = [_dot(x, _stack4(x, blk)) for x in lb]
    for lvl in range(5):
        z = [_stack4(x.astype(BF16), blk) for x in lpow]
        if lvl < 4:
            r = [_dot(jnp.concatenate([p, lp], axis=0).astype(BF16), zz) for p, lp, zz in zip(pinv, lpow, z)]
            pinv = [p + x[:c] for p, x in zip(pinv, r)]
            lpow = [x[c:] for x in r]
        else:
            pinv = [p + _dot(p.astype(BF16), zz) for p, zz in zip(pinv, z)]
    egc = [jnp.exp(x) for x in gcum]
    pb = [p.astype(BF16) for p in pinv]
    for n, (d, ci) in enumerate(units):
        sl = sls[ci]
        u_ref, wq_ref, at_ref, kg_ref, ds_ref = out_refs[d]
        bc = beta[d][sl]
        u_ref[sl, :] = _dot(pb[n], _stack4((v[sl] * bc).astype(BF16), blk))
        w_key = _dot(pb[n], _stack4((k[sl] * bc * egc[n]).astype(BF16), blk))
        g_last = gcum[n][0:1, :] if d else gcum[n][c - 1:c, :]
        wq_ref[2 * ci * c:(2 * ci + 1) * c, :] = w_key.astype(BF16)
        wq_ref[(2 * ci + 1) * c:(2 * ci + 2) * c, :] = (q[sl] * egc[n]).astype(BF16)
        at_ref[sl, :] = attn[n].astype(BF16)
        kg_ref[sl, :] = (k[sl] * jnp.exp(g_last - gcum[n])).astype(BF16)
        ds_ref[ci] = jnp.exp(g_last)


def _gdn_prep(u, bsz, seq, cwq, cwk, cwv, alog, dtb, eb, eg):
    ts = GDN_PREP_BLOCK
    ns = seq // ts
    t = bsz * seq
    nc = ts // CHUNK
    specs = []
    for cb in (CB_QB, CB_KB, CB_VB):
        specs += list(_row_specs(ts, ns, t, cb, False))
    specs.append(pl.BlockSpec((ts, LANES), lambda b, i: (b * ns + i, CB128_AB)))
    specs += [_const_spec((4, W_GROUP))] * 3
    specs += [_const_spec((1, LANES))] * 2 + [_const_spec((2, LANES, W_GROUP))] * 2

    def row(b, i):
        return (b * ns + i, 0)

    one_dir_specs = [pl.BlockSpec((ts, W_GROUP), row), pl.BlockSpec((2 * ts, W_GROUP), row),
                     pl.BlockSpec((ts, W_GROUP), row), pl.BlockSpec((ts, W_GROUP), row),
                     pl.BlockSpec((nc, 1, W_GROUP), lambda b, i: (b * ns + i, 0, 0))]
    one_dir_shapes = [jax.ShapeDtypeStruct((t, W_GROUP), F32), jax.ShapeDtypeStruct((2 * t, W_GROUP), BF16),
                      jax.ShapeDtypeStruct((t, W_GROUP), BF16), jax.ShapeDtypeStruct((t, W_GROUP), BF16),
                      jax.ShapeDtypeStruct((t // CHUNK, 1, W_GROUP), F32)]
    return pl.pallas_call(
        _gdn_prep_kernel,
        grid=(bsz, ns),
        in_specs=specs,
        out_specs=one_dir_specs * 2,
        out_shape=one_dir_shapes * 2,
        compiler_params=_params(("parallel", "parallel")),
        name="gdn_prep",
    )(*([u] * 10 + [cwq, cwk, cwv, alog, dtb, eb, eg]))


def _gdn_scan_kernel(group_sizes, periods, *refs):
    n_groups = len(group_sizes)
    ins = refs[:10 * n_groups]
    outs = refs[10 * n_groups:12 * n_groups]
    s_ref = refs[12 * n_groups]
    ts = ins[0].shape[1]
    nc = ts // CHUNK
    c = CHUNK

    chains = []
    for gi, nb in enumerate(group_sizes):
        first_chain = len(chains)
        for b in range(nb):
            for d in range(2):
                chains.append(tuple(ins[10 * gi + 5 * d:10 * gi + 5 * d + 5]) + (outs[2 * gi + d], d, b))

        @pl.when(pl.program_id(0) % periods[gi] == 0)
        def _(lo=first_chain, hi=len(chains)):
            s_ref[lo:hi] = jnp.zeros((hi - lo, W_GROUP, W_GROUP), F32)

    blk = _iota((W_GROUP, W_GROUP), 0) // HEAD_DIM == _iota((W_GROUP, W_GROUP), 1) // HEAD_DIM
    state = [s_ref[n] for n in range(len(chains))]
    for step in range(nc):
        pos = (step, nc - 1 - step)
        ws = [_dot(wq[b, 2 * pos[d] * c:(2 * pos[d] + 2) * c, :], s.astype(BF16))
              for (u_, wq, at, kg, ds, o, d, b), s in zip(chains, state)]
        vnb = [(u_[b, pos[d] * c:(pos[d] + 1) * c, :] - x[:c]).astype(BF16)
               for (u_, wq, at, kg, ds, o, d, b), x in zip(chains, ws)]
        for (u_, wq, at, kg, ds, o, d, b), x, vn in zip(chains, ws, vnb):
            sl = slice(pos[d] * c, (pos[d] + 1) * c)
            o[b, sl, :] = x[c:] + _dot(at[b, sl, :], _stack4(vn, blk))
        state = [s * ds[b, pos[d]] + jnp.where(blk, _dot_tn(kg[b, pos[d] * c:(pos[d] + 1) * c, :], vn), 0.0)
                 for (u_, wq, at, kg, ds, o, d, b), s, vn in zip(chains, state, vnb)]
    for n, s in enumerate(state):
        s_ref[n] = s


def _gdn_scan(trunks):
    ts = GDN_SCAN_BLOCK
    nc = ts // CHUNK
    n_steps = max(seq // ts for _, _, seq in trunks)
    specs, args, out_specs, out_shapes, group_sizes, periods = [], [], [], [], [], []
    for prep, bsz, seq in trunks:
        ns = seq // ts
        assert n_steps % ns == 0 and bsz % (n_steps // ns) == 0
        nb = bsz // (n_steps // ns)
        group_sizes.append(nb)
        periods.append(ns)
        for d in range(2):
            u_, wq, at, kg, ds = prep[5 * d:5 * d + 5]
            args += [u_.reshape(bsz, seq, W_GROUP), wq.reshape(bsz, 2 * seq, W_GROUP),
                     at.reshape(bsz, seq, W_GROUP), kg.reshape(bsz, seq, W_GROUP),
                     ds.reshape(bsz, seq // CHUNK, 1, W_GROUP)]

            def tb(i, ns=ns, d=d):
                return (ns - 1 - i % ns) if d else (i % ns)

            def blk3(rows, nb=nb, ns=ns, tb=tb):
                return pl.BlockSpec((nb, rows, W_GROUP), lambda i: (i // ns, tb(i), 0))

            specs += [blk3(ts), blk3(2 * ts), blk3(ts), blk3(ts),
                      pl.BlockSpec((nb, nc, 1, W_GROUP), lambda i, ns=ns, tb=tb: (i // ns, tb(i), 0, 0))]
            out_specs.append(blk3(ts))
            out_shapes.append(jax.ShapeDtypeStruct((bsz, seq, W_GROUP), F32))
    outs = pl.pallas_call(
        functools.partial(_gdn_scan_kernel, tuple(group_sizes), tuple(periods)),
        grid=(n_steps,),
        in_specs=specs,
        out_specs=out_specs,
        out_shape=out_shapes,
        scratch_shapes=[pltpu.VMEM((2 * sum(group_sizes), W_GROUP, W_GROUP), F32)],
        compiler_params=_params(("arbitrary",)),
        name="gdn_scan",
    )(*args)
    return [(outs[2 * n].reshape(bsz * seq, W_GROUP), outs[2 * n + 1].reshape(bsz * seq, W_GROUP))
            for n, (_, bsz, seq) in enumerate(trunks)]


RET_CHUNK = 128
RET_BLOCK = 1024


def _ret_kernel(reverse, *refs):
    if reverse:
        (q_ref, k_ref, v_ref, cos_ref, sin_ref, lgq_ref, lgv_ref, o_ref, s_ref) = refs
    else:
        (q_ref, k_ref, v_ref, cos_ref, sin_ref, lgq_ref, lgv_ref, lgfc_ref, lgbc_ref, g_ref, ob_ref, rn_ref,
         o_ref, s_ref, dm_ref) = refs
    ts = q_ref.shape[0]
    cr = RET_CHUNK
    nc = ts // cr
    nk = N_HEADS * cr

    @pl.when(pl.program_id(1) == 0)
    def _():
        s_ref[...] = jnp.zeros_like(s_ref)
        if not reverse:
            ii = _iota((cr, nk), 0)
            jj = _iota((cr, nk), 1) % cr
            dist = (ii - jj).astype(F32)
            lower, upper = jj <= ii, jj >= ii
            dm_ref[...] = (jnp.where(lower, jnp.exp(jnp.where(lower, dist * lgfc_ref[...], 0.0)), 0.0)
                           + jnp.where(upper, jnp.exp(jnp.where(upper, -dist * lgbc_ref[...], 0.0)), 0.0))

    cos = cos_ref[...]
    sin = sin_ref[...]

    def rope(t):
        t1, t2 = t[:, :LANES], t[:, LANES:]
        return jnp.concatenate([t1 * cos - t2 * sin, t1 * sin + t2 * cos], axis=-1)

    q = rope(q_ref[...])
    k = rope(k_ref[...]) * (HEAD_DIM ** -0.5)
    v = v_ref[...]
    lgq = lgq_ref[...]
    lgv = lgv_ref[...]
    pos = _iota((cr, W_GROUP), 0).astype(F32)
    rows_qk = (_iota((W_GROUP, W_GROUP), 0) % LANES) // 32
    blk_state = rows_qk == _iota((W_GROUP, W_GROUP), 1) // HEAD_DIM
    if reverse:
        q_fac = jnp.exp((cr - pos) * lgq)
        k_fac = jnp.exp(pos * lgq)
    else:
        q_fac = jnp.exp((pos + 1.0) * lgq)
        k_fac = jnp.exp((cr - 1.0 - pos) * lgq)
    s_fac = jnp.exp(cr * lgv)
    sls = [slice(ci * cr, (ci + 1) * cr) for ci in range(nc)]
    kv = [jnp.where(blk_state, _dot_tn((k[sl] * k_fac).astype(BF16), v[sl].astype(BF16)), 0.0) for sl in sls]
    s = s_ref[...]
    cross = [None] * nc
    for ci in (range(nc - 1, -1, -1) if reverse else range(nc)):
        cross[ci] = _dot((q[sls[ci]] * q_fac).astype(BF16), s.astype(BF16))
        s = s * s_fac + kv[ci]
    s_ref[...] = s
    if reverse:
        o_ref[...] = jnp.concatenate(cross, axis=0)
        return
    stack_head = _iota((nk, W_GROUP), 0) // cr
    k_mask = stack_head == (_iota((nk, W_GROUP), 1) % LANES) // 32
    v_mask = stack_head == _iota((nk, W_GROUP), 1) // HEAD_DIM
    qk = [_dot_nt(q[sl].astype(BF16), _stack4(k[sl].astype(BF16), k_mask)) for sl in sls]
    dm = dm_ref[...]
    intra = [_dot((x * dm).astype(BF16), _stack4(v[sl].astype(BF16), v_mask)) for x, sl in zip(qk, sls)]
    o = jnp.concatenate([a + b for a, b in zip(intra, cross)], axis=0) + ob_ref[...]
    ones_h = _head_ones()
    oc = o - _dot_sel(o, ones_h, 3) * (1.0 / HEAD_DIM)
    var = _dot_sel(oc * oc, ones_h, 2) * (1.0 / HEAD_DIM)
    o_ref[...] = (oc * lax.rsqrt(var + EPS) * rn_ref[...] * _silu(g_ref[...])).astype(o_ref.dtype)


def _ret(u, bsz, seq, reverse, cos, sin, lgq, lgv, lgfc=None, lgbc=None, rn=None, ob=None):
    cr = RET_CHUNK
    ts = RET_BLOCK
    ns = seq // ts
    t = bsz * seq
    tbf = (lambda i: ns - 1 - i) if reverse else (lambda i: i)

    def col(cb):
        return pl.BlockSpec((ts, W_GROUP), lambda b, i: (b * ns + tbf(i), cb))

    tab = pl.BlockSpec((ts, LANES), lambda b, i: (tbf(i), 0))
    specs = [col(CB_QC), col(CB_KC), col(CB_VC), tab, tab, _const_spec((1, W_GROUP)), _const_spec((1, W_GROUP))]
    args = [u, u, u, cos, sin, lgq, lgv]
    scratch = [pltpu.VMEM((W_GROUP, W_GROUP), F32)]
    if reverse:
        out_dtype = F32
    else:
        specs += [_const_spec((1, N_HEADS * cr)), _const_spec((1, N_HEADS * cr)), col(CB_GC),
                  pl.BlockSpec((ts, W_GROUP), lambda b, i: (b * ns + i, 0)), _const_spec((1, W_GROUP))]
        args += [lgfc, lgbc, u, ob, rn]
        scratch.append(pltpu.VMEM((cr, N_HEADS * cr), F32))
        out_dtype = BF16
    return pl.pallas_call(
        functools.partial(_ret_kernel, reverse),
        grid=(bsz, ns),
        in_specs=specs,
        out_specs=pl.BlockSpec((ts, W_GROUP), lambda b, i: (b * ns + tbf(i), 0)),
        out_shape=jax.ShapeDtypeStruct((t, W_GROUP), out_dtype),
        scratch_shapes=scratch,
        compiler_params=_params(("parallel", "arbitrary")),
        name="ret_bwd" if reverse else "ret_fwd",
    )(*args)


NA_GROUP = 16
NA_WIN = NA_KH * GRID_W
NA_ROWS_PER_PASS = 4


def _na_kernel(rows, q_ref, kp_ref, kc_ref, kn_ref, vp_ref, vc_ref, vn_ref, tab_ref, o_ref, kw, vw):
    g = pl.program_id(1)
    blk = NA_GROUP * GRID_W
    kw[0:blk, :] = kp_ref[...]
    kw[blk:2 * blk, :] = kc_ref[...]
    kw[2 * blk:3 * blk, :] = kn_ref[...]
    vw[0:blk, :] = vp_ref[...]
    vw[blk:2 * blk, :] = vc_ref[...]
    vw[2 * blk:3 * blk, :] = vn_ref[...]
    row_head = _iota((N_HEADS * GRID_W, W_GROUP), 0) // GRID_W
    lane_head = _iota((N_HEADS * GRID_W, W_GROUP), 1) // HEAD_DIM
    blk = row_head == lane_head
    out_head = _iota((GRID_W, W_GROUP), 1) // HEAD_DIM
    for j0 in range(0, NA_GROUP, NA_ROWS_PER_PASS):
        js = list(range(j0, j0 + NA_ROWS_PER_PASS))
        vwins, sc = [], []
        for j in js:
            r = g * NA_GROUP + j
            r0 = jnp.clip(r - NA_KH // 2, 0, rows - NA_KH)
            off = pl.multiple_of((r0 - g * NA_GROUP + NA_GROUP) * GRID_W, GRID_W)
            qst = _stack4(q_ref[j * GRID_W:(j + 1) * GRID_W, :], blk)
            vwins.append(vw[pl.ds(off, NA_WIN), :])
            sc.append(_dot_nt(qst, kw[pl.ds(off, NA_WIN), :]) + tab_ref[r - r0])
        mx = [jnp.max(s, axis=-1, keepdims=True) for s in sc]
        pr = [jnp.exp(s - m) for s, m in zip(sc, mx)]
        den = [jnp.sum(p, axis=-1, keepdims=True) for p in pr]
        pv = [_dot(p.astype(BF16), vw_) / l for p, vw_, l in zip(pr, vwins, den)]
        for j, o in zip(js, pv):
            res = o[0:GRID_W]
            for h in range(1, N_HEADS):
                res = jnp.where(out_head == h, o[h * GRID_W:(h + 1) * GRID_W], res)
            o_ref[j * GRID_W:(j + 1) * GRID_W, :] = res.astype(o_ref.dtype)


def _na(qn, kn, vn, table, bsz, seq):
    rows = seq // GRID_W
    ng = rows // NA_GROUP
    blk = NA_GROUP * GRID_W
    t = bsz * seq
    cur = pl.BlockSpec((blk, W_GROUP), lambda b, g: (b * ng + g, 0))
    prev = pl.BlockSpec((blk, W_GROUP), lambda b, g: (b * ng + jnp.maximum(g - 1, 0), 0))
    nxt = pl.BlockSpec((blk, W_GROUP), lambda b, g: (b * ng + jnp.minimum(g + 1, ng - 1), 0))
    return pl.pallas_call(
        functools.partial(_na_kernel, rows),
        grid=(bsz, ng),
        in_specs=[cur, prev, cur, nxt, prev, cur, nxt, _const_spec((NA_KH, N_HEADS * GRID_W, NA_WIN))],
        out_specs=cur,
        out_shape=jax.ShapeDtypeStruct((t, W_GROUP), BF16),
        scratch_shapes=[pltpu.VMEM((3 * blk, W_GROUP), BF16), pltpu.VMEM((3 * blk, W_GROUP), BF16)],
        compiler_params=_params(("parallel", "parallel")),
        name="na",
    )(qn, kn, kn, kn, vn, vn, vn, table)


def _na_table(rpb):
    didx = np.arange(NA_KH)[:, None]
    kk = np.arange(NA_KH)[None, :]
    dr = (kk + (NA_KH - 1) - didx).reshape(-1)
    cc = np.arange(GRID_W)[:, None]
    kc = np.arange(GRID_W)[None, :]
    c0 = np.clip(cc - NA_KW // 2, 0, GRID_W - NA_KW)
    valid = (kc >= c0) & (kc < c0 + NA_KW)
    dc = np.clip(kc - cc + (NA_KW - 1), 0, 2 * NA_KW - 2).reshape(-1)
    row_sel = np.zeros((dr.size, 2 * NA_KH - 1), np.float32)
    row_sel[np.arange(dr.size), dr] = 1.0
    col_sel = np.zeros((2 * NA_KW - 1, dc.size), np.float32)
    col_sel[dc, np.arange(dc.size)] = 1.0
    bias = jnp.einsum('ar,hrs,sb->hab', row_sel, rpb, col_sel, precision=lax.Precision.HIGHEST)
    bias = bias.reshape(N_HEADS, NA_KH, NA_KH, GRID_W, GRID_W)
    tab = jnp.where(valid[None, None, None], bias, NEG_BIG)
    return jnp.transpose(tab, (1, 0, 3, 2, 4)).reshape(NA_KH, N_HEADS * GRID_W, NA_WIN)


FF_CHUNK = 256
POST_HALO = 16


def _mix_residual(x, a, bf, bb, z, gn, c, d, w_ref):
    ob = bf + bb
    ms = _dot_sel(ob * ob, _head_ones(), 2) * (1.0 / HEAD_DIM)
    out_b = (ob * lax.rsqrt(ms + EPS) * gn * _silu(z)).astype(BF16)
    acc = x
    for gi, m in enumerate((a, out_b, c, d)):
        acc = acc + _dot(m, w_ref[gi * W_GROUP:(gi + 1) * W_GROUP, :])
    return acc


def _post_kernel(*refs):
    streams = [refs[3 * n:3 * n + 3] for n in range(7)]
    (p_ref, gn_ref, wo_ref, n2_ref, wg_ref, wu_ref, cw_ref, cb_ref, wd_ref, n3_ref, pg_ref, pp_ref,
     o_ref, act_s) = refs[21:]
    tm = streams[0][0].shape[0]
    i = pl.program_id(1)
    first = i == 0
    last = i == pl.num_programs(1) - 1
    gn = gn_ref[...]
    mains = [s[0][...] for s in streams]
    halos = [jnp.concatenate([s[1][...], s[2][...]], axis=0) for s in streams]
    x = _mix_residual(mains[0], *mains[1:5], gn, *mains[5:], wo_ref)
    xh = _mix_residual(halos[0], *halos[1:5], gn, *halos[5:], wo_ref)
    n2 = n2_ref[...]
    h2f = _rms(x, n2)
    h2h = _rms(xh, n2)
    h_before = jnp.where(first, 0.0, h2h[POST_HALO - SUBLANES:POST_HALO])
    h_after = jnp.where(last, 0.0, h2h[POST_HALO:POST_HALO + SUBLANES])
    h2 = h2f.astype(BF16)
    h2e = jnp.concatenate([h_before, h2f, h_after], axis=0).astype(BF16)
    for f0 in range(0, D_FF, FF_CHUNK):
        fs = slice(f0, min(f0 + FF_CHUNK, D_FF))
        ge = _dot(h2e, wg_ref[:, fs])
        ne = tm + 2 * SUBLANES
        g_prev = pltpu.roll(ge, 1, 0)[SUBLANES:SUBLANES + tm]
        g_next = pltpu.roll(ge, ne - 1, 0)[SUBLANES:SUBLANES + tm]
        gate = (cw_ref[0:1, fs] * g_prev + cw_ref[1:2, fs] * ge[SUBLANES:SUBLANES + tm]
                + cw_ref[2:3, fs] * g_next + cb_ref[:, fs])
        up = _dot(h2, wu_ref[:, fs])
        act_s[:, fs] = (_gelu_tanh(gate) * up).astype(BF16)
    acc = x + _dot(act_s[...], wd_ref[...])
    h3 = _rms(acc, n3_ref[...]).astype(BF16)
    gate3 = _sigmoid(_dot(h3, pg_ref[...]))
    o_ref[...] = acc + gate3 * _dot(p_ref[...].astype(BF16), pp_ref[...])


def _post(x, u, oa, obf, obb, gn, oc, od, w_out, p, p_layer, bsz, seq, n2, wg, wu, cw, cb, wd, n3, pg, pp):
    tm = 512

    def layer_spec(rows, cols):
        return pl.BlockSpec((None, rows, cols), lambda *_: (p_layer, 0, 0), pipeline_mode=pl.Buffered(1))

    ns = seq // tm
    t = bsz * seq
    specs = list(_row_specs(tm, ns, t, 0, False, width=D_MODEL, halo=POST_HALO))
    arrays = [x] * 3
    for arr, cb_ in ((oa, 0), (obf, 0), (obb, 0), (u, CB_ZB), (oc, 0), (od, 0)):
        specs += list(_row_specs(tm, ns, t, cb_, False, halo=POST_HALO))
        arrays += [arr] * 3
    specs += [pl.BlockSpec((tm, PLE_DIM), lambda b, i: (p_layer * (t // tm) + b * ns + i, 0)),
              _const_spec((1, W_GROUP)), layer_spec(D_MODEL, D_MODEL),
              _const_spec((1, D_MODEL)), layer_spec(D_MODEL, D_FF), layer_spec(D_MODEL, D_FF),
              _const_spec((3, D_FF)), _const_spec((1, D_FF)), layer_spec(D_FF, D_MODEL),
              _const_spec((1, D_MODEL)), layer_spec(D_MODEL, D_MODEL), layer_spec(PLE_DIM, D_MODEL)]
    return pl.pallas_call(
        _post_kernel,
        grid=(bsz, ns),
        in_specs=specs,
        out_specs=pl.BlockSpec((tm, D_MODEL), lambda b, i: (b * ns + i, 0)),
        out_shape=jax.ShapeDtypeStruct((t, D_MODEL), F32),
        compiler_params=_params(("parallel", "parallel")),
        scratch_shapes=[pltpu.VMEM((tm, D_FF), BF16)],
        name="post",
    )(*arrays, p, gn, w_out, n2, wg, wu, cw, cb, wd, n3, pg, pp)


def _block_diag(w):
    h, n, _ = w.shape
    eye = jnp.eye(h, dtype=w.dtype)
    return (eye[:, None, :, None] * w[:, :, None, :]).reshape(h * n, h * n)


def _row(v):
    return v.reshape(1, -1).astype(F32)


def _lane_pad(v8, offset):
    return jnp.zeros((1, LANES), F32).at[0, offset:offset + 2 * N_HEADS].set(v8.reshape(-1).astype(F32))


def _expander(offset):
    e = np.zeros((LANES, W_GROUP), np.float32)
    for h in range(N_HEADS):
        e[offset + h, h * HEAD_DIM:(h + 1) * HEAD_DIM] = 1.0
    return jnp.asarray(e, BF16)


def _rope_tables(max_seq):
    half = HEAD_DIM // 2
    inv_freq = ROPE_BASE ** (-jnp.arange(half, dtype=F32) / half)
    ang = jnp.arange(max_seq, dtype=F32)[:, None] * inv_freq[None, :]
    return jnp.tile(jnp.cos(ang), (1, N_HEADS)), jnp.tile(jnp.sin(ang), (1, N_HEADS))


def _layer_front(x, lw, bsz, seq, rope):
    w_in = _reorder_in_cols(lw['w_in'].astype(BF16))
    u, qn, kn, vn = _inproj(x, _row(lw['norm1']), w_in,
                            _row(jnp.tile(lw['na_qnorm'], N_HEADS)), _row(jnp.tile(lw['na_knorm'], N_HEADS)))

    cw_a, cb_a = lw['conv_a_w'].astype(F32), _row(lw['conv_a_b'])
    h_dir = None
    for d in (1, 0):
        h_dir = _rglru(u, bsz, seq, d == 1, cw_a, cb_a,
                       _block_diag(lw['lru_wr'][d]).astype(BF16), _row(lw['lru_br'][d]),
                       _block_diag(lw['lru_wi'][d]).astype(BF16), _row(lw['lru_bi'][d]),
                       _row(lw['lru_lambda'][d]), hb=h_dir)
    out_a = h_dir

    gconv = lw['gdn_conv'].astype(F32)
    cwq, cwk, cwv = gconv[:, :W_GROUP], gconv[:, W_GROUP:2 * W_GROUP], gconv[:, 2 * W_GROUP:]
    alog = _lane_pad(lw['gdn_a_log'], 2 * N_HEADS)
    dtb = _lane_pad(lw['gdn_dt_bias'], 2 * N_HEADS)
    eb = jnp.stack([_expander(d * N_HEADS) for d in range(2)])
    eg = jnp.stack([_expander(2 * N_HEADS + d * N_HEADS) for d in range(2)])
    prep = _gdn_prep(u, bsz, seq, cwq, cwk, cwv, alog, dtb, eb, eg)

    half = HEAD_DIM // 2
    cos, sin = rope
    log_gamma = jnp.log1p(-jnp.exp2(-lw['ret_decay'].astype(F32)))
    rn = _row(jnp.tile(lw['ret_norm'], N_HEADS))
    o_dir = None
    for d in (1, 0):
        lg = log_gamma[d]
        lgq = _row(jnp.tile(jnp.repeat(lg, half), 2))
        lgv = _row(jnp.repeat(lg, HEAD_DIM))
        if d == 1:
            o_dir = _ret(u, bsz, seq, True, cos, sin, lgq, lgv)
        else:
            o_dir = _ret(u, bsz, seq, False, cos, sin, lgq, lgv,
                         lgfc=_row(jnp.repeat(log_gamma[0], RET_CHUNK)), lgbc=_row(jnp.repeat(log_gamma[1], RET_CHUNK)),
                         rn=rn, ob=o_dir)
    out_c = o_dir

    out_d = _na(qn, kn, vn, _na_table(lw['na_rpb'].astype(F32)), bsz, seq)
    return dict(u=u, out_a=out_a, out_c=out_c, out_d=out_d, prep=prep)


_STACKED = ('w_out', 'ffn_wg', 'ffn_wu', 'ffn_wd', 'ple_gate', 'ple_proj')


def _layer_back(x, p, p_layer, front, o_b, lw, stacks, bsz, seq):
    return _post(x, front['u'], front['out_a'], o_b[0], o_b[1], _row(jnp.tile(lw['gdn_norm'], N_HEADS)),
                 front['out_c'], front['out_d'], stacks['w_out'], p, p_layer, bsz, seq,
                 _row(lw['norm2']), stacks['ffn_wg'], stacks['ffn_wu'],
                 lw['ffn_conv_w'].astype(F32), _row(lw['ffn_conv_b']), stacks['ffn_wd'],
                 _row(lw['norm3']), stacks['ple_gate'], stacks['ple_proj'])


def _layer_multi(xs, ps, p_layer, lw, stacks, shapes):
    rope = _rope_tables(max(seq for _, seq in shapes))
    fronts = [_layer_front(x, lw, bsz, seq, rope) for x, (bsz, seq) in zip(xs, shapes)]
    scans = _gdn_scan([(f['prep'], bsz, seq) for f, (bsz, seq) in zip(fronts, shapes)])
    return [_layer_back(x, p, p_layer, f, o_b, lw, stacks, bsz, seq)
            for x, p, f, o_b, (bsz, seq) in zip(xs, ps, fronts, scans, shapes)]


def _layer(x, p, lw, bsz, seq):
    stacks = {n: lw[n][None].astype(BF16) for n in _STACKED}
    return _layer_multi([x], [p], 0, lw, stacks, [(bsz, seq)])[0]


_WEIGHT_NAMES = ('norm1', 'norm2', 'norm3', 'w_in', 'w_out', 'conv_a_w', 'conv_a_b', 'lru_wr', 'lru_br',
                 'lru_wi', 'lru_bi', 'lru_lambda', 'gdn_conv', 'gdn_a_log', 'gdn_dt_bias', 'gdn_norm',
                 'ret_decay', 'ret_norm', 'na_qnorm', 'na_knorm', 'na_rpb', 'ffn_wg', 'ffn_wu',
                 'ffn_conv_w', 'ffn_conv_b', 'ffn_wd', 'ple_proj', 'ple_gate')


def _trunks(xs, ps, weights):
    shapes = [x.shape[:2] for x in xs]
    flat = [x.reshape(bsz * seq, D_MODEL) for x, (bsz, seq) in zip(xs, shapes)]
    p_rows = [p.reshape(DEPTH * bsz * seq, PLE_DIM) for p, (bsz, seq) in zip(ps, shapes)]
    stacks = {n: weights[n].astype(BF16) for n in _STACKED}
    for i in range(DEPTH):
        lw = {n: w[i] for n, w in weights.items() if n not in _STACKED}
        flat = _layer_multi(flat, p_rows, i, lw, stacks, shapes)
    return tuple(x.reshape(bsz, seq, D_MODEL) for x, (bsz, seq) in zip(flat, shapes))


def kernel(x_prompt, x_sample, p_prompt, p_sample, norm1, norm2, norm3, w_in, w_out, conv_a_w, conv_a_b, lru_wr, lru_br, lru_wi, lru_bi, lru_lambda, gdn_conv, gdn_a_log, gdn_dt_bias, gdn_norm, ret_decay, ret_norm, na_qnorm, na_knorm, na_rpb, ffn_wg, ffn_wu, ffn_conv_w, ffn_conv_b, ffn_wd, ple_proj, ple_gate):
    weights = dict(zip(_WEIGHT_NAMES, (norm1, norm2, norm3, w_in, w_out, conv_a_w, conv_a_b, lru_wr, lru_br,
                                       lru_wi, lru_bi, lru_lambda, gdn_conv, gdn_a_log, gdn_dt_bias, gdn_norm,
                                       ret_decay, ret_norm, na_qnorm, na_knorm, na_rpb, ffn_wg, ffn_wu,
                                       ffn_conv_w, ffn_conv_b, ffn_wd, ple_proj, ple_gate)))
    return _trunks((x_prompt, x_sample), (p_prompt, p_sample), weights)
```
